```python
import math
import jax
import jax.numpy as jnp
from jax import lax
import numpy as np

D_MODEL = 2048
BATCH = 4
SEQ = 2048
DEPTH = 4

GRID_W = 64
CTX_LEN = 256
EPS = 1e-6
MIX_W = D_MODEL // 2

RET_DK = 128
RET_HEADS = MIX_W // RET_DK
RET_CHUNK = 128
ROPE_BASE = 10000.0

SSD_HD = 64
SSD_HEADS = MIX_W // SSD_HD
SSD_GROUPS = 4
SSD_STATE = 128
SSD_CHUNK = 128
CONV_W = 4

LRU_BLOCKS = 8
LRU_C = 8.0

HG_DK = 128
HG_HEADS = MIX_W // HG_DK
HG_CHUNK = 64

N_BRANCH = 4
FFN_DIM = 5632
N_EXPERTS = 8
TOP_K = 2
EXPERT_DIM = 4096
N_MOD = 6

PROJ_LAYOUT = (
    ('ret_q', MIX_W), ('ret_k', MIX_W), ('ret_v', MIX_W), ('ret_g', MIX_W),
    ('ssd_z', MIX_W), ('ssd_xbc', MIX_W + 2 * SSD_GROUPS * SSD_STATE), ('ssd_dt', 2 * SSD_HEADS),
    ('lru_x', MIX_W), ('lru_y', MIX_W),
    ('hg_q', MIX_W), ('hg_f', 2 * MIX_W), ('hg_i', MIX_W), ('hg_g', MIX_W),
    ('gates', N_BRANCH * D_MODEL),
)
PROJ_NAMES = tuple(name for name, _ in PROJ_LAYOUT)
PROJ_SPLITS = tuple(sum(w for _, w in PROJ_LAYOUT[:i + 1]) for i in range(len(PROJ_LAYOUT) - 1))
PROJ_W = sum(w for _, w in PROJ_LAYOUT)

kernel_name = 'hybrid_prefix_flow_trunk'

F32 = jnp.float32


def rms_norm(x, w=None):
    xf = x.astype(F32)
    y = xf * lax.rsqrt(jnp.mean(xf * xf, axis=-1, keepdims=True) + EPS)
    if w is not None:
        y = y * w.astype(F32)
    return y.astype(x.dtype)


def group_rms_norm(x, groups, w):
    shp = x.shape
    return rms_norm(x.reshape(*shp[:-1], groups, shp[-1] // groups)).reshape(shp) * w


def per_token(v_ctx, v_lat, n_ctx, start, total):
    is_ctx = (jnp.arange(start, total) < n_ctx)[None, :, None]
    return jnp.where(is_ctx, v_ctx[:, None, :], v_lat[:, None, :])


def axial_rope_tables(rows):
    r, col = jnp.meshgrid(jnp.arange(rows), jnp.arange(GRID_W), indexing='ij')
    n_freq = RET_DK // 4
    freqs = ROPE_BASE ** (-jnp.arange(n_freq, dtype=F32) / n_freq)
    ang = jnp.concatenate([r.reshape(-1, 1) * freqs, col.reshape(-1, 1) * freqs], axis=-1)
    return jnp.cos(ang), jnp.sin(ang)


def apply_rope(z, cos, sin):
    z1, z2 = jnp.split(z.astype(F32), 2, axis=-1)
    c, s = cos[None, :, None], sin[None, :, None]
    return jnp.concatenate([z1 * c - z2 * s, z1 * s + z2 * c], axis=-1)


def dwconv(u, w, b):
    left = (CONV_W - 1) // 2
    out = lax.conv_general_dilated(u, w[:, None, :].astype(u.dtype), (1,), [(left, CONV_W - 1 - left)],
                                   dimension_numbers=('NWC', 'WIO', 'NWC'), feature_group_count=u.shape[-1])
    return out + b


def masked_decay(diff, mask):
    return jnp.where(mask, jnp.exp(jnp.where(mask, diff, 0.0)), 0.0)


def chunk_scan_scalar(q, k, v, log_a, s0, chunk):
    b, t, h, dk = q.shape
    dv = v.shape[-1]
    n = t // chunk
    q, k, v = (z.astype(F32).reshape(b, n, chunk, h, z.shape[-1]) for z in (q, k, v))
    cum = jnp.cumsum(log_a.astype(F32).reshape(b, n, chunk, h), axis=2)
    causal = jnp.tril(jnp.ones((chunk, chunk), dtype=bool))[None, None, :, :, None]
    decay = masked_decay(cum[:, :, :, None, :] - cum[:, :, None, :, :], causal)
    scores = jnp.einsum('bnihk,bnjhk->bnijh', q, k) * decay
    o = jnp.einsum('bnijh,bnjhv->bnihv', scores, v)
    contrib = jnp.einsum('bnjhk,bnjh,bnjhv->bnhkv', k, jnp.exp(cum[:, :, -1:] - cum), v)
    if s0 is None:
        s0 = jnp.zeros((b, h, dk, dv), F32)

    def step(s, inp):
        a_tot, con = inp
        return a_tot[:, :, None, None] * s + con, s

    s_fin, s_in = lax.scan(step, s0, (jnp.moveaxis(jnp.exp(cum[:, :, -1]), 1, 0), jnp.moveaxis(contrib, 1, 0)))
    o = o + jnp.einsum('bnihk,bnih,bnhkv->bnihv', q, jnp.exp(cum), jnp.moveaxis(s_in, 0, 1))
    return o.reshape(b, t, h, dv), s_fin


def chunk_scan_vector(q, k, v, log_f, s0, chunk):
    b, t, h, dk = q.shape
    dv = v.shape[-1]
    n = t // chunk

    def blocks(z):
        return jnp.moveaxis(z.astype(F32).reshape(b, n, chunk, h, z.shape[-1]), 1, 0)

    causal = jnp.tril(jnp.ones((chunk, chunk), dtype=bool))[None, :, :, None, None]
    if s0 is None:
        s0 = jnp.zeros((b, h, dk, dv), F32)

    def step(s, inp):
        qc, kc, vc, gc = inp
        cum = jnp.cumsum(gc, axis=1)
        decay = masked_decay(cum[:, :, None] - cum[:, None], causal)
        scores = jnp.einsum('bihk,bjhk,bijhk->bijh', qc, kc, decay)
        o = jnp.einsum('bijh,bjhv->bihv', scores, vc) + jnp.einsum('bihk,bhkv->bihv', qc * jnp.exp(cum), s)
        last = cum[:, -1]
        s_new = jnp.exp(last)[..., None] * s + jnp.einsum('bjhk,bjhv->bhkv', kc * jnp.exp(last[:, None] - cum), vc)
        return s_new, o

    s_fin, o = lax.scan(step, s0, tuple(blocks(z) for z in (q, k, v, log_f)))
    return jnp.moveaxis(o, 0, 1).reshape(b, t, h, dv), s_fin


def lru_scan(log_a, b, h0):
    a = jnp.exp(log_a.astype(F32))

    def combine(left, right):
        return left[0] * right[0], right[0] * left[1] + right[1]

    a_cum, h = lax.associative_scan(combine, (a, b.astype(F32)), axis=1)
    if h0 is not None:
        h = h + a_cum * h0[:, None]
    return h, h[:, -1]


def bidirectional(scan, ctx_fwd, ctx_bwd, lat_fwd, lat_bwd):
    flip = lambda args: tuple(jnp.flip(a, axis=1) for a in args)
    oc_f, sc_f = scan(ctx_fwd, None)
    ol_f, _ = scan(lat_fwd, sc_f)
    oc_b, sc_b = scan(flip(ctx_bwd), None)
    ol_b, _ = scan(flip(lat_bwd), sc_b)
    return oc_f + jnp.flip(oc_b, axis=1), ol_f + jnp.flip(ol_b, axis=1)


def retention(pc, pl, cos, sin):
    log_gamma = jnp.log1p(-jnp.exp2(-5.0 - jnp.arange(RET_HEADS, dtype=F32)))

    def prep(p, rotate):
        b, t, _ = p['ret_q'].shape
        q = p['ret_q'].reshape(b, t, RET_HEADS, RET_DK)
        k = p['ret_k'].reshape(b, t, RET_HEADS, RET_DK)
        if rotate:
            q, k = apply_rope(q, cos, sin), apply_rope(k, cos, sin)
        v = p['ret_v'].reshape(b, t, RET_HEADS, RET_DK)
        return (q, k * RET_DK ** -0.5, v, jnp.broadcast_to(log_gamma, (b, t, RET_HEADS)))

    ac, al = prep(pc, False), prep(pl, True)
    scan = lambda args, s0: chunk_scan_scalar(*args, s0, RET_CHUNK)
    oc, ol = bidirectional(scan, ac, ac, al, al)

    def post(o, p):
        b, t = o.shape[:2]
        return rms_norm(o).reshape(b, t, MIX_W) * jax.nn.silu(p['ret_g'])

    return post(oc, pc), post(ol, pl)


def ssd(pc, pl, conv_w, conv_b, a_log, dt_bias, d_skip, norm_w):
    a = -jnp.exp(a_log.astype(F32))
    rep = SSD_HEADS // SSD_GROUPS

    def prep(p):
        b, t, _ = p['ssd_z'].shape
        xbc = jax.nn.silu(dwconv(p['ssd_xbc'], conv_w, conv_b))
        xs, bm, cm = jnp.split(xbc, [MIX_W, MIX_W + SSD_GROUPS * SSD_STATE], axis=-1)
        xs = xs.reshape(b, t, SSD_HEADS, SSD_HD)
        bm = jnp.repeat(bm.reshape(b, t, SSD_GROUPS, SSD_STATE), rep, axis=2)
        cm = jnp.repeat(cm.reshape(b, t, SSD_GROUPS, SSD_STATE), rep, axis=2)
        dt = jax.nn.softplus(p['ssd_dt'].astype(F32).reshape(b, t, 2, SSD_HEADS) + dt_bias)
        dirs = tuple((cm, bm * dt[:, :, d, :, None], xs, dt[:, :, d] * a[d]) for d in range(2))
        return dirs, xs

    (cf, cb), xc = prep(pc)
    (lf, lb), xl = prep(pl)
    scan = lambda args, s0: chunk_scan_scalar(*args, s0, SSD_CHUNK)
    oc, ol = bidirectional(scan, cf, cb, lf, lb)

    def post(o, xs, p):
        b, t = o.shape[:2]
        y = (o + d_skip[:, None] * xs).reshape(b, t, MIX_W)
        return group_rms_norm(y * jax.nn.silu(p['ssd_z']), SSD_GROUPS, norm_w)

    return post(oc, xc, pc), post(ol, xl, pl)


def rglru(pc, pl, conv_w, conv_b, wa, ba, wx, bx, lam):
    log_sig = -jax.nn.softplus(-lam.astype(F32))

    def blockdiag(z, w):
        b, t, _ = z.shape
        return jnp.einsum('btgi,gij->btgj', z.reshape(b, t, LRU_BLOCKS, -1), w).reshape(b, t, MIX_W)

    def prep(p):
        xc = dwconv(p['lru_x'], conv_w, conv_b)

        def direction(d):
            r = jax.nn.sigmoid(blockdiag(xc, wa[d]) + ba[d])
            i = jax.nn.sigmoid(blockdiag(xc, wx[d]) + bx[d])
            log_a = LRU_C * r.astype(F32) * log_sig[d]
            return (log_a, jnp.sqrt(-jnp.expm1(2.0 * log_a)) * (i * xc))

        return direction(0), direction(1)

    cf, cb = prep(pc)
    lf, lb = prep(pl)
    scan = lambda args, s0: lru_scan(*args, s0)
    oc, ol = bidirectional(scan, cf, cb, lf, lb)
    return oc * jax.nn.gelu(pc['lru_y']), ol * jax.nn.gelu(pl['lru_y'])


def hgrn2(pc, pl, lb, norm_w):
    def prep(p):
        b, t, _ = p['hg_q'].shape
        hd = lambda z: z.reshape(b, t, HG_HEADS, HG_DK)
        q = hd(jax.nn.silu(p['hg_q']))
        v = hd(p['hg_i'])
        f = p['hg_f'].astype(F32).reshape(b, t, 2, MIX_W)
        log_f = jnp.log(lb + (1.0 - lb) * jax.nn.sigmoid(f))
        k = (1.0 - lb) * jax.nn.sigmoid(-f)
        return tuple((q, hd(k[:, :, d]), v, hd(log_f[:, :, d])) for d in range(2))

    cf, cb = prep(pc)
    lf, lbk = prep(pl)
    scan = lambda args, s0: chunk_scan_vector(*args, s0, HG_CHUNK)
    oc, ol = bidirectional(scan, cf, cb, lf, lbk)

    def post(o, p):
        b, t = o.shape[:2]
        return group_rms_norm(o.reshape(b, t, MIX_W), HG_HEADS, norm_w) * jax.nn.silu(p['hg_g'])

    return post(oc, pc), post(ol, pl)


def merge(branches, gates, wb, wo):
    g = jax.nn.sigmoid(gates.reshape(*gates.shape[:-1], N_BRANCH, D_MODEL))
    y = sum(g[..., n, :] * (branches[n] @ wb[n]) for n in range(N_BRANCH))
    return y @ wo


def swiglu(x, wgu, w2):
    gate, up = jnp.split(x @ wgu, 2, axis=-1)
    return (jax.nn.silu(gate) * up) @ w2


def moe(x, router_w, wgu, w2):
    b, t, d = x.shape
    xf = x.reshape(b * t, d)
    logits = (xf @ router_w).astype(F32)
    top_val, top_idx = lax.top_k(logits, TOP_K)
    weights = jax.nn.softmax(top_val, axis=-1)
    gate = jnp.einsum('nk,nke->ne', weights, jax.nn.one_hot(top_idx, N_EXPERTS, dtype=F32)).astype(xf.dtype)
    y = sum(gate[:, e:e + 1] * swiglu(xf, wgu[e], w2[e]) for e in range(N_EXPERTS))
    return y.reshape(b, t, d)


def setup_inputs(seed: int = 0) -> dict:
    key = jax.random.key(seed)
    k = jax.random.split(key, 40)
    n_dense, n_moe = (DEPTH + 1) // 2, DEPTH // 2
    xbc_w = MIX_W + 2 * SSD_GROUPS * SSD_STATE
    bw = MIX_W // LRU_BLOCKS

    def nrm(i, shape, scale):
        return jax.random.normal(k[i], shape, F32) * scale

    def gain(i, shape):
        return 1.0 + nrm(i, shape, 0.02)

    a_init = jax.random.uniform(k[10], (DEPTH, 2, SSD_HEADS), F32, 1.0, 16.0)
    dt0 = jnp.exp(jax.random.uniform(k[11], (DEPTH, 2, SSD_HEADS), F32, math.log(1e-3), math.log(1e-1)))
    a_c = jax.random.uniform(k[20], (DEPTH, 2, MIX_W), F32, 0.9, 0.999)
    a_base = a_c ** (1.0 / LRU_C)
    return {
        'x': nrm(0, (BATCH, SEQ, D_MODEL), 1.0),
        'c': nrm(1, (BATCH, D_MODEL), 1.0),
        'ctx': nrm(2, (BATCH, CTX_LEN, D_MODEL), 1.0),
        'c_ctx': nrm(3, (D_MODEL,), 1.0),
        'ada_w': nrm(4, (DEPTH, D_MODEL, N_MOD * D_MODEL), 0.5 * D_MODEL ** -0.5),
        'ada_b': nrm(5, (DEPTH, N_MOD * D_MODEL), 0.01),
        'norm1_w': gain(6, (DEPTH, D_MODEL)),
        'norm2_w': gain(7, (DEPTH, D_MODEL)),
        'w_in': nrm(8, (DEPTH, D_MODEL, PROJ_W), D_MODEL ** -0.5),
        'ssd_conv_w': nrm(9, (DEPTH, CONV_W, xbc_w), CONV_W ** -0.5),
        'ssd_conv_b': nrm(12, (DEPTH, xbc_w), 0.01),
        'ssd_a_log': jnp.log(a_init),
        'ssd_dt_bias': dt0 + jnp.log(-jnp.expm1(-dt0)),
        'ssd_d': 1.0 + nrm(13, (DEPTH, SSD_HEADS), 0.1),
        'ssd_norm_w': gain(14, (DEPTH, MIX_W)),
        'lru_conv_w': nrm(15, (DEPTH, CONV_W, MIX_W), CONV_W ** -0.5),
        'lru_conv_b': nrm(16, (DEPTH, MIX_W), 0.01),
        'lru_wa': nrm(17, (DEPTH, 2, LRU_BLOCKS, bw, bw), bw ** -0.5),
        'lru_ba': nrm(18, (DEPTH, 2, MIX_W), 0.01),
        'lru_wx': nrm(19, (DEPTH, 2, LRU_BLOCKS, bw, bw), bw ** -0.5),
        'lru_bx': nrm(21, (DEPTH, 2, MIX_W), 0.01),
        'lru_lambda': jnp.log(a_base) - jnp.log1p(-a_base),
        'hg_lb_logits': 1.0 + nrm(22, (2, DEPTH, MIX_W), 0.1),
        'hg_norm_w': gain(23, (DEPTH, MIX_W)),
        'w_branch': nrm(24, (DEPTH, N_BRANCH, MIX_W, D_MODEL), MIX_W ** -0.5),
        'w_out': nrm(25, (DEPTH, D_MODEL, D_MODEL), D_MODEL ** -0.5),
        'ffn_wgu': nrm(26, (n_dense, D_MODEL, 2 * FFN_DIM), D_MODEL ** -0.5),
        'ffn_w2': nrm(27, (n_dense, FFN_DIM, D_MODEL), FFN_DIM ** -0.5),
        'router_w': nrm(28, (n_moe, D_MODEL, N_EXPERTS), D_MODEL ** -0.5),
        'moe_wgu': nrm(29, (n_moe, N_EXPERTS, D_MODEL, 2 * EXPERT_DIM), D_MODEL ** -0.5),
        'moe_w2': nrm(30, (n_moe, N_EXPERTS, EXPERT_DIM, D_MODEL), EXPERT_DIM ** -0.5),
        'final_norm_w': gain(31, (D_MODEL,)),
    }


def reference(x, c, ctx, c_ctx, ada_w, ada_b, norm1_w, norm2_w, w_in,
              ssd_conv_w, ssd_conv_b, ssd_a_log, ssd_dt_bias, ssd_d, ssd_norm_w,
              lru_conv_w, lru_conv_b, lru_wa, lru_ba, lru_wx, lru_bx, lru_lambda,
              hg_lb_logits, hg_norm_w, w_branch, w_out, ffn_wgu, ffn_w2,
              router_w, moe_wgu, moe_w2, final_norm_w):
    n_ctx, n_lat = ctx.shape[1], x.shape[1]
    total = n_ctx + n_lat
    rows = n_lat // GRID_W
    cos, sin = axial_rope_tables(rows)
    p_lb = jax.nn.softmax(hg_lb_logits.astype(F32), axis=1)
    lower_bounds = jnp.cumsum(p_lb, axis=1) - p_lb[:, :1]
    cond_lat = jax.nn.silu(c)
    cond_ctx = jax.nn.silu(c_ctx)[None]
    h = jnp.concatenate([ctx.astype(x.dtype), x], axis=1)
    for layer in range(DEPTH):
        start = n_ctx if layer == DEPTH - 1 else 0
        mc = jnp.split(cond_ctx @ ada_w[layer] + ada_b[layer], N_MOD, axis=-1)
        ml = jnp.split(cond_lat @ ada_w[layer] + ada_b[layer], N_MOD, axis=-1)
        xn = rms_norm(h, norm1_w[layer]) * (1.0 + per_token(mc[1], ml[1], n_ctx, 0, total)) \
            + per_token(mc[0], ml[0], n_ctx, 0, total)
        u = dict(zip(PROJ_NAMES, jnp.split(xn @ w_in[layer], PROJ_SPLITS, axis=-1)))
        pc = {name: val[:, :n_ctx] for name, val in u.items()}
        pl = {name: val[:, n_ctx:] for name, val in u.items()}
        branches = (
            retention(pc, pl, cos, sin),
            ssd(pc, pl, ssd_conv_w[layer], ssd_conv_b[layer], ssd_a_log[layer], ssd_dt_bias[layer],
                ssd_d[layer], ssd_norm_w[layer]),
            rglru(pc, pl, lru_conv_w[layer], lru_conv_b[layer], lru_wa[layer], lru_ba[layer],
                  lru_wx[layer], lru_bx[layer], lru_lambda[layer]),
            hgrn2(pc, pl, lower_bounds[:, layer], hg_norm_w[layer]),
        )
        br = [bl if start else jnp.concatenate([bc, bl], axis=1) for bc, bl in branches]
        mixed = merge(br, u['gates'][:, start:], w_branch[layer], w_out[layer])
        h = h[:, start:] + per_token(mc[2], ml[2], n_ctx, start, total) * mixed
        xn = rms_norm(h, norm2_w[layer]) * (1.0 + per_token(mc[4], ml[4], n_ctx, start, total)) \
            + per_token(mc[3], ml[3], n_ctx, start, total)
        if layer % 2 == 0:
            ffn = swiglu(xn, ffn_wgu[layer // 2], ffn_w2[layer // 2])
        else:
            ffn = moe(xn, router_w[layer // 2], moe_wgu[layer // 2], moe_w2[layer // 2])
        h = h + per_token(mc[5], ml[5], n_ctx, start, total) * ffn
    return rms_norm(h, final_norm_w)
```

```python
import functools
import math

import numpy as np
import jax
import jax.numpy as jnp
from jax import lax
from jax.experimental import pallas as pl
from jax.experimental.pallas import tpu as pltpu

F32 = jnp.float32
BF16 = jnp.bfloat16
HI = lax.Precision.HIGHEST

EPS = 1e-6
GRID_W = 64
ROPE_BASE = 10000.0
RET_DK = 128
SSD_HD = 64
SSD_GROUPS = 4
SSD_STATE = 128
CONV_W = 4
LRU_BLOCKS = 8
LRU_C = 8.0
HG_DK = 128
N_BRANCH = 4
N_EXPERTS = 8
N_MOD = 6

LANES = 128
SUBLANES = 8
VMEM_LIMIT = 52 * 1024 * 1024

SCAN_CHUNK = 256
HG_CHUNK = 128
HG_LEVELS = 7


def _params(*sem):
    return pltpu.CompilerParams(dimension_semantics=sem, vmem_limit_bytes=VMEM_LIMIT)


def _dot(a, b, prec=None):
    return jnp.dot(a, b, preferred_element_type=F32, precision=prec)


def _dot_nt(a, b, prec=None):
    return lax.dot_general(a, b, (((1,), (1,)), ((), ())), preferred_element_type=F32, precision=prec)


def _dot_tn(a, b, prec=None):
    return lax.dot_general(a, b, (((0,), (0,)), ((), ())), preferred_element_type=F32, precision=prec)


def _sigmoid(x):
    return 1.0 / (1.0 + jnp.exp(-x))


def _silu(x):
    return x * _sigmoid(x)


def _softplus(x):
    return jnp.maximum(x, 0.0) + jnp.log1p(jnp.exp(-jnp.abs(x)))


def _gelu_tanh(x):
    return 0.5 * x * (1.0 + jnp.tanh(math.sqrt(2.0 / math.pi) * (x + 0.044715 * (x * x * x))))


def _bwd_chunk(s, n_chunks, n_ctx_chunks):
    return jnp.where(s < n_ctx_chunks, n_ctx_chunks - 1 - s, n_chunks - 1 - (s - n_ctx_chunks))


def _norm_kernel(h_ref, w_ref, mod_ref, o_ref, *, shift, scale):
    x = h_ref[...]
    y = x * lax.rsqrt(jnp.mean(x * x, axis=-1, keepdims=True) + EPS) * w_ref[...]
    o_ref[...] = (y * (1.0 + mod_ref[scale:scale + 1, :]) + mod_ref[shift:shift + 1, :]).astype(o_ref.dtype)


def _norm_router_kernel(h_ref, w_ref, mod_ref, rw_ref, o_ref, gate_ref, *, shift, scale):
    x = h_ref[...]
    y = x * lax.rsqrt(jnp.mean(x * x, axis=-1, keepdims=True) + EPS) * w_ref[...]
    xn = y * (1.0 + mod_ref[scale:scale + 1, :]) + mod_ref[shift:shift + 1, :]
    o_ref[...] = xn.astype(o_ref.dtype)
    logits = _dot(xn.astype(BF16), rw_ref[...])
    lane = lax.broadcasted_iota(jnp.int32, logits.shape, 1)
    neg = jnp.float32(-jnp.inf)
    logits = jnp.where(lane < N_EXPERTS, logits, neg)
    v1 = jnp.max(logits, axis=-1, keepdims=True)
    i1 = jnp.min(jnp.where(logits == v1, lane, LANES), axis=-1, keepdims=True)
    rest = jnp.where(lane == i1, neg, logits)
    v2 = jnp.max(rest, axis=-1, keepdims=True)
    i2 = jnp.min(jnp.where(rest == v2, lane, LANES), axis=-1, keepdims=True)
    e2 = jnp.exp(v2 - v1)
    w1 = 1.0 / (1.0 + e2)
    w2 = e2 / (1.0 + e2)
    gate_ref[...] = jnp.where(lane == i1, w1, 0.0) + jnp.where(lane == i2, w2, 0.0)


def _norm_call(h, w, mod, *, shift, scale, n_ctx, router_w=None, out_dtype=BF16):
    b, t, d = h.shape
    tr = n_ctx
    grid = (b, t // tr)
    h_spec = pl.BlockSpec((None, tr, d), lambda i, j: (i, j, 0))
    w_spec = pl.BlockSpec((1, d), lambda i, j: (0, 0))
    mod_spec = pl.BlockSpec((None, None, N_MOD, d), lambda i, j: (i, jnp.minimum(j, 1), 0, 0))
    if router_w is None:
        return pl.pallas_call(
            functools.partial(_norm_kernel, shift=shift, scale=scale),
            grid=grid, in_specs=[h_spec, w_spec, mod_spec], out_specs=h_spec,
            out_shape=jax.ShapeDtypeStruct(h.shape, out_dtype),
            compiler_params=_params("parallel", "parallel"), name="norm_mod",
        )(h, w.reshape(1, d), mod)
    rw = jnp.zeros((d, LANES), BF16).at[:, :N_EXPERTS].set(router_w.astype(BF16))
    return pl.pallas_call(
        functools.partial(_norm_router_kernel, shift=shift, scale=scale),
        grid=grid,
        in_specs=[h_spec, w_spec, mod_spec, pl.BlockSpec((d, LANES), lambda i, j: (0, 0))],
        out_specs=[h_spec, pl.BlockSpec((None, tr, LANES), lambda i, j: (i, j, 0))],
        out_shape=[jax.ShapeDtypeStruct(h.shape, out_dtype), jax.ShapeDtypeStruct((b, t, LANES), F32)],
        compiler_params=_params("parallel", "parallel"), name="norm_mod_router",
    )(h, w.reshape(1, d), mod, rw)


def _final_norm_kernel(h_ref, w_ref, o_ref):
    x = h_ref[...]
    o_ref[...] = x * lax.rsqrt(jnp.mean(x * x, axis=-1, keepdims=True) + EPS) * w_ref[...]


def _final_norm_call(h, w, n_ctx):
    b, t, d = h.shape
    tr = n_ctx
    return pl.pallas_call(
        _final_norm_kernel,
        grid=(b, (t - n_ctx) // tr),
        in_specs=[pl.BlockSpec((None, tr, d), lambda i, j: (i, j + 1, 0)), pl.BlockSpec((1, d), lambda i, j: (0, 0))],
        out_specs=pl.BlockSpec((None, tr, d), lambda i, j: (i, j, 0)),
        out_shape=jax.ShapeDtypeStruct((b, t - n_ctx, d), F32),
        compiler_params=_params("parallel", "parallel"), name="final_norm",
    )(h, w.reshape(1, d))


def _pick(n, prefs):
    for p in prefs:
        if n % p == 0:
            return p
    return n


def _mm_kernel(a_ref, w_ref, o_ref, acc_ref, *, nk):
    k = pl.program_id(2)
    part = _dot(a_ref[...], w_ref[...])
    if nk == 1:
        o_ref[...] = part.astype(o_ref.dtype)
        return

    @pl.when(k == 0)
    def _():
        acc_ref[...] = part

    @pl.when(k > 0)
    def _():
        acc_ref[...] += part

    @pl.when(k == nk - 1)
    def _():
        o_ref[...] = acc_ref[...].astype(o_ref.dtype)


def _matmul(a, w, out_dtype=F32, tm=None, tn=None, tk=None):
    m, kd = a.shape
    n = w.shape[1]
    tm = tm or _pick(m, (1152, 1024, 768, 512, 256))
    tn = tn or _pick(n, (1024, 512, 256, 128))
    tk = tk or (kd if kd <= 2048 else _pick(kd, (512, 256, 128)))
    nk = kd // tk
    return pl.pallas_call(
        functools.partial(_mm_kernel, nk=nk),
        grid=(n // tn, m // tm, nk),
        in_specs=[pl.BlockSpec((tm, tk), lambda j, i, k: (i, k)), pl.BlockSpec((tk, tn), lambda j, i, k: (k, j))],
        out_specs=pl.BlockSpec((tm, tn), lambda j, i, k: (i, j)),
        out_shape=jax.ShapeDtypeStruct((m, n), out_dtype),
        scratch_shapes=[pltpu.VMEM((tm, tn), F32)],
        compiler_params=_params("parallel", "parallel", "arbitrary"), name="matmul",
    )(a, w)


def _row_is_ctx(tm, t_total, n_ctx):
    row0 = pl.program_id(1) * tm
    pos = (row0 + lax.broadcasted_iota(jnp.int32, (tm, 1), 0)) % t_total
    return pos < n_ctx


def _mm_res_kernel(a_ref, w_ref, res_ref, mod_ref, o_ref, acc_ref, *, nk, tm, t_total, n_ctx):
    k = pl.program_id(2)
    part = _dot(a_ref[...], w_ref[...])

    @pl.when(k == 0)
    def _():
        acc_ref[...] = part

    @pl.when(k > 0)
    def _():
        acc_ref[...] += part

    @pl.when(k == nk - 1)
    def _():
        mod = jnp.where(_row_is_ctx(tm, t_total, n_ctx), mod_ref[0:1, :], mod_ref[1:2, :])
        o_ref[...] = res_ref[...] + mod * acc_ref[...]


def _matmul_residual(a, w, res, mod, mod_idx, *, t_total, n_ctx, tm=None, tn=None, tk=None):
    m, kd = a.shape
    n = w.shape[1]
    tm = tm or _pick(t_total, (1152, 768, 256))
    tn = tn or _pick(n, (1024, 512, 256, 128))
    tk = tk or (kd if kd <= 2048 else _pick(kd, (512, 256, 128)))
    nk = kd // tk
    per_b = t_total // tm
    return pl.pallas_call(
        functools.partial(_mm_res_kernel, nk=nk, tm=tm, t_total=t_total, n_ctx=n_ctx),
        grid=(n // tn, m // tm, nk),
        in_specs=[pl.BlockSpec((tm, tk), lambda j, i, k: (i, k)),
                  pl.BlockSpec((tk, tn), lambda j, i, k: (k, j)),
                  pl.BlockSpec((tm, tn), lambda j, i, k: (i, j)),
                  pl.BlockSpec((None, None, 2, tn), lambda j, i, k: (i // per_b, mod_idx, 0, j))],
        out_specs=pl.BlockSpec((tm, tn), lambda j, i, k: (i, j)),
        out_shape=jax.ShapeDtypeStruct((m, n), F32),
        scratch_shapes=[pltpu.VMEM((tm, tn), F32)],
        compiler_params=_params("parallel", "parallel", "arbitrary"), name="matmul_residual",
    )(a, w, res, mod)


def _swiglu_kernel(a_ref, wg_ref, wu_ref, o_ref):
    a = a_ref[...]
    g = _dot(a, wg_ref[...])
    u = _dot(a, wu_ref[...])
    o_ref[...] = (_silu(g) * u).astype(o_ref.dtype)


def _swiglu_up(a, wgu, tm=None, tn=None):
    m, kd = a.shape
    f = wgu.shape[1] // 2
    tm = tm or _pick(m, (1152, 1024, 768, 512, 256))
    tn = tn or _pick(f, (512, 256, 128))
    nf = f // tn
    return pl.pallas_call(
        _swiglu_kernel,
        grid=(nf, m // tm),
        in_specs=[pl.BlockSpec((tm, kd), lambda j, i: (i, 0)),
                  pl.BlockSpec((kd, tn), lambda j, i: (0, j)),
                  pl.BlockSpec((kd, tn), lambda j, i: (0, j + nf))],
        out_specs=pl.BlockSpec((tm, tn), lambda j, i: (i, j)),
        out_shape=jax.ShapeDtypeStruct((m, f), BF16),
        compiler_params=_params("parallel", "parallel"), name="swiglu_up",
    )(a, wgu, wgu)


def _moe_up_kernel(a_ref, wg_ref, wu_ref, gate_ref, o_ref, *, n_sub):
    e = pl.program_id(0) // n_sub
    a = a_ref[...]
    g = _dot(a, wg_ref[...])
    u = _dot(a, wu_ref[...])
    gates = gate_ref[...]
    lane = lax.broadcasted_iota(jnp.int32, gates.shape, 1)
    ge = jnp.sum(jnp.where(lane == e, gates, 0.0), axis=-1, keepdims=True)
    o_ref[...] = (_silu(g) * u * ge).astype(o_ref.dtype)


def _moe_up(a, wgu, gates, tm=None, tn=None):
    m, kd = a.shape
    ne, _, f2 = wgu.shape
    f = f2 // 2
    tm = tm or _pick(m, (1152, 1024, 768, 512, 256))
    tn = tn or _pick(f, (512, 256, 128))
    nf = f // tn
    return pl.pallas_call(
        functools.partial(_moe_up_kernel, n_sub=nf),
        grid=(ne * nf, m // tm),
        in_specs=[pl.BlockSpec((tm, kd), lambda j, i: (i, 0)),
                  pl.BlockSpec((None, kd, tn), lambda j, i: (j // nf, 0, j % nf)),
                  pl.BlockSpec((None, kd, tn), lambda j, i: (j // nf, 0, j % nf + nf)),
                  pl.BlockSpec((tm, LANES), lambda j, i: (i, 0))],
        out_specs=pl.BlockSpec((tm, tn), lambda j, i: (i, j)),
        out_shape=jax.ShapeDtypeStruct((m, ne * f), BF16),
        compiler_params=_params("parallel", "parallel"), name="moe_up",
    )(a, wgu, wgu, gates)


def _merge_kernel(b0, b1, b2, b3, w_ref, g0, g1, g2, g3, o_ref):
    acc = None
    for n, (br, g) in enumerate(((b0, g0), (b1, g1), (b2, g2), (b3, g3))):
        term = _sigmoid(g[...]) * _dot(br[...], w_ref[n])
        acc = term if acc is None else acc + term
    o_ref[...] = acc.astype(o_ref.dtype)


def _merge(branches, gates, wb, tm=None, tn=None):
    m, kd = branches[0].shape
    d = wb.shape[-1]
    tm = tm or _pick(m, (1152, 1024, 768, 512, 256))
    tn = tn or _pick(d, (512, 256, 128))
    nd = d // tn
    br_spec = pl.BlockSpec((tm, kd), lambda j, i: (i, 0))
    gate_specs = [pl.BlockSpec((tm, tn), functools.partial(lambda j, i, n: (i, j + n * nd), n=n))
                  for n in range(N_BRANCH)]
    return pl.pallas_call(
        _merge_kernel,
        grid=(nd, m // tm),
        in_specs=[br_spec] * N_BRANCH + [pl.BlockSpec((N_BRANCH, kd, tn), lambda j, i: (0, 0, j))] + gate_specs,
        out_specs=pl.BlockSpec((tm, tn), lambda j, i: (i, j)),
        out_shape=jax.ShapeDtypeStruct((m, d), BF16),
        compiler_params=_params("parallel", "parallel"), name="merge",
    )(*branches, wb, gates, gates, gates, gates)


def _conv_kernel(u_ref, w_ref, b_ref, o_ref, *, t_total, n_ctx, act):
    x = u_ref[...]
    pos = lax.broadcasted_iota(jnp.int32, (t_total, 1), 0)
    seg_lo = jnp.where(pos < n_ctx, 0, n_ctx)
    seg_hi = jnp.where(pos < n_ctx, n_ctx, t_total)
    left = (CONV_W - 1) // 2
    acc = None
    for j in range(CONV_W):
        d = j - left
        xs = x if d == 0 else pltpu.roll(x, (-d) % t_total, 0)
        ok = (pos + d >= seg_lo) & (pos + d < seg_hi)
        term = jnp.where(ok, xs, 0.0) * w_ref[j:j + 1, :]
        acc = term if acc is None else acc + term
    acc = acc + b_ref[...]
    o_ref[...] = _silu(acc) if act else acc


def _dwconv(u, col0, width, w, b, *, n_ctx, act, wt=256):
    bsz, t, _ = u.shape
    c0 = col0 // wt
    return pl.pallas_call(
        functools.partial(_conv_kernel, t_total=t, n_ctx=n_ctx, act=act),
        grid=(bsz, width // wt),
        in_specs=[pl.BlockSpec((None, t, wt), lambda i, j: (i, 0, j + c0)),
                  pl.BlockSpec((CONV_W, wt), lambda i, j: (0, j)),
                  pl.BlockSpec((1, wt), lambda i, j: (0, j))],
        out_specs=pl.BlockSpec((None, t, wt), lambda i, j: (i, 0, j)),
        out_shape=jax.ShapeDtypeStruct((bsz, t, width), F32),
        compiler_params=_params("parallel", "parallel"), name="dwconv",
    )(u, w, b.reshape(1, width))


def _ret_kernel(lg_ref, q_ref, k_ref, v_ref, g_ref, cos_ref, sin_ref, o_ref, qs, ks, acc, *, t_total, n_ctx):
    c = SCAN_CHUNK
    n_chunks, n_ctx_chunks = t_total // c, n_ctx // c
    lg = lg_ref[pl.program_id(1)]
    cos, sin = cos_ref[...], sin_ref[...]
    q, k = q_ref[...], k_ref[...]
    qs[...] = q * cos + pltpu.roll(q, RET_DK // 2, 1) * sin
    ks[...] = (k * cos + pltpu.roll(k, RET_DK // 2, 1) * sin) * (RET_DK ** -0.5)

    ii = lax.broadcasted_iota(jnp.int32, (c, c), 0)
    jj = lax.broadcasted_iota(jnp.int32, (c, c), 1)
    dist = (ii - jj).astype(F32)
    r = lax.broadcasted_iota(jnp.int32, (c, 1), 0).astype(F32)
    decay_all = jnp.exp(c * lg)

    def chunk(off, s_in, fwd):
        qc = qs[pl.ds(off, c), :]
        kc = ks[pl.ds(off, c), :]
        vc = v_ref[pl.ds(off, c), :].astype(BF16)
        if fwd:
            mask, steps, q_pow, k_pow = ii >= jj, dist, r + 1.0, c - 1.0 - r
        else:
            mask, steps, q_pow, k_pow = jj >= ii, -dist, c - r, r
        decay = jnp.where(mask, jnp.exp(jnp.where(mask, steps, 0.0) * lg), 0.0)
        scores = _dot_nt(qc.astype(BF16), kc.astype(BF16)) * decay
        o = _dot(scores.astype(BF16), vc) + jnp.exp(q_pow * lg) * _dot(qc.astype(BF16), s_in.astype(BF16))
        s_out = decay_all * s_in + _dot_tn((kc * jnp.exp(k_pow * lg)).astype(BF16), vc)
        return o, s_out

    def fwd_body(s, state):
        off = pl.multiple_of(s * c, c)
        o, state = chunk(off, state, True)
        acc[pl.ds(off, c), :] = o
        return state

    def bwd_body(s, state):
        off = pl.multiple_of(_bwd_chunk(s, n_chunks, n_ctx_chunks) * c, c)
        o, state = chunk(off, state, False)
        o = o + acc[pl.ds(off, c), :]
        y = o * lax.rsqrt(jnp.mean(o * o, axis=-1, keepdims=True) + EPS)
        o_ref[pl.ds(off, c), :] = (y * _silu(g_ref[pl.ds(off, c), :])).astype(o_ref.dtype)
        return state

    zero = jnp.zeros((RET_DK, RET_DK), F32)
    lax.fori_loop(0, n_chunks, fwd_body, zero)
    lax.fori_loop(0, n_chunks, bwd_body, zero)


def _retention(u, col0, mix_w, cos2, sin2, *, n_ctx):
    bsz, t, _ = u.shape
    heads = mix_w // RET_DK
    c0 = col0 // RET_DK
    log_gamma = jnp.log1p(-jnp.exp2(-5.0 - jnp.arange(heads, dtype=F32)))

    def spec(k):
        return pl.BlockSpec((None, t, RET_DK), lambda i, h, lg, k=k: (i, 0, c0 + k * heads + h))

    table = pl.BlockSpec((t, RET_DK), lambda i, h, lg: (0, 0))
    return pl.pallas_call(
        functools.partial(_ret_kernel, t_total=t, n_ctx=n_ctx),
        grid_spec=pltpu.PrefetchScalarGridSpec(
            num_scalar_prefetch=1, grid=(bsz, heads),
            in_specs=[spec(0), spec(1), spec(2), spec(3), table, table],
            out_specs=pl.BlockSpec((None, t, RET_DK), lambda i, h, lg: (i, 0, h)),
            scratch_shapes=[pltpu.VMEM((t, RET_DK), F32)] * 3),
        out_shape=jax.ShapeDtypeStruct((bsz, t, mix_w), BF16),
        compiler_params=_params("parallel", "arbitrary"), name="retention",
    )(log_gamma, u, u, u, u, cos2, sin2)


def _ssd_kernel(xs_ref, bm_ref, cm_ref, z_ref, dt_ref, dtb_ref, alog_ref, dsk_ref, nw_ref, sel_ref, o_ref,
                acc, dts, las, *, t_total, n_ctx):
    c = SCAN_CHUNK
    rep = sel_ref.shape[1] // 2
    gw = rep * SSD_HD
    n_chunks, n_ctx_chunks = t_total // c, n_ctx // c
    dt_all = _softplus(dt_ref[...] + dtb_ref[...])
    dts[...] = dt_all
    las[...] = dt_all * (-jnp.exp(alog_ref[...]))

    ii = lax.broadcasted_iota(jnp.int32, (c, c), 0)
    jj = lax.broadcasted_iota(jnp.int32, (c, c), 1)
    lane_head = lax.broadcasted_iota(jnp.int32, (1, gw), 1) // SSD_HD

    def chunk(off, s_in, d):
        fwd = d == 0
        mask = (ii >= jj) if fwd else (jj >= ii)
        tri = mask.astype(F32)
        la = las[pl.ds(off, c), :]
        cum = _dot(tri, la, HI)
        mask_t = (jj >= ii) if fwd else (ii >= jj)
        cum_rows = _dot_tn(la, mask_t.astype(F32), HI)
        sel = sel_ref[d]
        cum_e = _dot(cum, sel, HI)
        dt_e = _dot(dts[pl.ds(off, c), :], sel, HI)
        last_e = cum_e[c - 1:c, :] if fwd else cum_e[0:1, :]
        xs = xs_ref[pl.ds(off, c), :]
        vdt = xs * dt_e
        cm = cm_ref[pl.ds(off, c), :].astype(BF16)
        bm = bm_ref[pl.ds(off, c), :].astype(BF16)
        qk = _dot_nt(cm, bm)
        o = jnp.exp(cum_e) * _dot(cm, s_in.astype(BF16))
        for i in range(rep):
            col = d * rep + i
            diff = cum[:, col:col + 1] - cum_rows[col:col + 1, :]
            decay = jnp.where(mask, jnp.exp(jnp.where(mask, diff, 0.0)), 0.0)
            o = o + _dot((qk * decay).astype(BF16), jnp.where(lane_head == i, vdt, 0.0).astype(BF16))
        s_out = jnp.exp(last_e) * s_in + _dot_tn(bm, (vdt * jnp.exp(last_e - cum_e)).astype(BF16))
        return o, s_out, xs

    def fwd_body(s, state):
        off = pl.multiple_of(s * c, c)
        o, state, _ = chunk(off, state, 0)
        acc[pl.ds(off, c), :] = o
        return state

    def bwd_body(s, state):
        off = pl.multiple_of(_bwd_chunk(s, n_chunks, n_ctx_chunks) * c, c)
        o, state, xs = chunk(off, state, 1)
        y = (o + acc[pl.ds(off, c), :] + dsk_ref[...] * xs) * _silu(z_ref[pl.ds(off, c), :])
        y = y * lax.rsqrt(jnp.mean(y * y, axis=-1, keepdims=True) + EPS)
        o_ref[pl.ds(off, c), :] = (y * nw_ref[...]).astype(o_ref.dtype)
        return state

    zero = jnp.zeros((SSD_STATE, gw), F32)
    lax.fori_loop(0, n_chunks, fwd_body, zero)
    lax.fori_loop(0, n_chunks, bwd_body, zero)


def _ssd(u_z, z_col0, xact, u_dt, dt_bias, a_log, d_skip, norm_w, mix_w, *, n_ctx):
    bsz, t, _ = xact.shape
    heads = mix_w // SSD_HD
    rep = heads // SSD_GROUPS
    gw = rep * SSD_HD
    def per_group(v):
        lead = v.shape[:-1]
        v = v.reshape(*lead, 2, SSD_GROUPS, rep)
        return jnp.moveaxis(v, -2, 0).reshape(SSD_GROUPS, *lead, 2 * rep)
    dt_g = jnp.moveaxis(per_group(u_dt), 0, 1)
    dtb_g = per_group(dt_bias.reshape(1, 2 * heads))
    alog_g = per_group(a_log.reshape(1, 2 * heads))
    sel = np.zeros((2, 2 * rep, gw), np.float32)
    for d in range(2):
        for i in range(rep):
            sel[d, d * rep + i, i * SSD_HD:(i + 1) * SSD_HD] = 1.0
    dsk = jnp.repeat(d_skip, SSD_HD).reshape(1, mix_w)
    zc, bc, cc = z_col0 // gw, mix_w // SSD_STATE, mix_w // SSD_STATE + SSD_GROUPS
    small = lambda: pl.BlockSpec((None, 1, 2 * rep), lambda i, g: (g, 0, 0))
    return pl.pallas_call(
        functools.partial(_ssd_kernel, t_total=t, n_ctx=n_ctx),
        grid=(bsz, SSD_GROUPS),
        in_specs=[pl.BlockSpec((None, t, gw), lambda i, g: (i, 0, g)),
                  pl.BlockSpec((None, t, SSD_STATE), lambda i, g: (i, 0, bc + g)),
                  pl.BlockSpec((None, t, SSD_STATE), lambda i, g: (i, 0, cc + g)),
                  pl.BlockSpec((None, t, gw), lambda i, g: (i, 0, zc + g)),
                  pl.BlockSpec((None, None, t, 2 * rep), lambda i, g: (i, g, 0, 0)),
                  small(), small(),
                  pl.BlockSpec((1, gw), lambda i, g: (0, g)),
                  pl.BlockSpec((1, gw), lambda i, g: (0, g)),
                  pl.BlockSpec((2, 2 * rep, gw), lambda i, g: (0, 0, 0))],
        out_specs=pl.BlockSpec((None, t, gw), lambda i, g: (i, 0, g)),
        out_shape=jax.ShapeDtypeStruct((bsz, t, mix_w), BF16),
        scratch_shapes=[pltpu.VMEM((t, gw), F32), pltpu.VMEM((t, 2 * rep), F32), pltpu.VMEM((t, 2 * rep), F32)],
        compiler_params=_params("parallel", "arbitrary"), name="ssd",
    )(xact, xact, xact, u_z, dt_g, dtb_g, alog_g, dsk, norm_w.reshape(1, mix_w), jnp.asarray(sel))


def _tile_scan(a, b, row, reverse):
    for s in (1, 2, 4):
        if reverse:
            ok, shift = row < SUBLANES - s, SUBLANES - s
        else:
            ok, shift = row >= s, s
        b = b + a * jnp.where(ok, pltpu.roll(b, shift, 0), 0.0)
        a = a * jnp.where(ok, pltpu.roll(a, shift, 0), 1.0)
    return a, b


def _lru_kernel(xc_ref, y_ref, w_ref, bias_ref, lam_ref, o_ref, acc, a_f, b_f, a_b, b_b, *, t_total, n_ctx):
    c = SCAN_CHUNK
    nb, bw = w_ref.shape[0], w_ref.shape[1]
    n_chunks, n_ctx_chunks = t_total // c, n_ctx // c
    n_tiles = c // SUBLANES
    log_sig = -_softplus(-lam_ref[...])
    acc[...] = jnp.zeros_like(acc)
    row = lax.broadcasted_iota(jnp.int32, (SUBLANES, bw), 0)

    def gates(off, d, dst_a, dst_b):
        for blk in range(nb):
            cols = slice(blk * bw, (blk + 1) * bw)
            x = xc_ref[pl.ds(off, c), cols]
            g = _dot(x.astype(BF16), w_ref[blk, :, d * 2 * bw:(d + 1) * 2 * bw]) \
                + bias_ref[blk, :, d * 2 * bw:(d + 1) * 2 * bw]
            log_a = LRU_C * _sigmoid(g[:, :bw]) * log_sig[d:d + 1, cols]
            a = jnp.exp(log_a)
            dst_a[:, cols] = a
            dst_b[:, cols] = jnp.sqrt(-jnp.tanh(log_a) * (a * a + 1.0)) * (_sigmoid(g[:, bw:]) * x)

    def step(s, carry):
        off_f = pl.multiple_of(s * c, c)
        off_b = pl.multiple_of(_bwd_chunk(s, n_chunks, n_ctx_chunks) * c, c)
        gates(off_f, 0, a_f, b_f)
        gates(off_b, 1, a_b, b_b)

        def tile(t, carry):
            h_f, h_b = carry
            r_f = pl.multiple_of(t * SUBLANES, SUBLANES)
            r_b = pl.multiple_of((n_tiles - 1 - t) * SUBLANES, SUBLANES)
            new_f, new_b = [], []
            for blk in range(nb):
                cols = slice(blk * bw, (blk + 1) * bw)
                a, b = _tile_scan(a_f[pl.ds(r_f, SUBLANES), cols], b_f[pl.ds(r_f, SUBLANES), cols], row, False)
                h = b + a * h_f[blk]
                acc[pl.ds(off_f + r_f, SUBLANES), cols] += h
                new_f.append(h[SUBLANES - 1:SUBLANES, :])
                a, b = _tile_scan(a_b[pl.ds(r_b, SUBLANES), cols], b_b[pl.ds(r_b, SUBLANES), cols], row, True)
                h = b + a * h_b[blk]
                acc[pl.ds(off_b + r_b, SUBLANES), cols] += h
                new_b.append(h[0:1, :])
            return tuple(new_f), tuple(new_b)

        return lax.fori_loop(0, n_tiles, tile, carry)

    zero = tuple(jnp.zeros((1, bw), F32) for _ in range(nb))
    lax.fori_loop(0, n_chunks, step, (zero, zero))

    def finish(s, _):
        off = pl.multiple_of(s * c, c)
        o_ref[pl.ds(off, c), :] = (acc[pl.ds(off, c), :] * _gelu_tanh(y_ref[pl.ds(off, c), :])).astype(o_ref.dtype)
        return 0

    lax.fori_loop(0, n_chunks, finish, 0)


def _rglru(xc, u_y, y_col0, wa, ba, wx, bx, lam, mix_w, *, n_ctx, blocks_per_step=4):
    bsz, t, _ = xc.shape
    bw = mix_w // LRU_BLOCKS
    gw = blocks_per_step * bw
    w = jnp.concatenate([wa[0], wx[0], wa[1], wx[1]], axis=-1).astype(BF16)
    blk = lambda v: v.reshape(2, LRU_BLOCKS, 1, bw)
    bias = jnp.concatenate([blk(ba)[0], blk(bx)[0], blk(ba)[1], blk(bx)[1]], axis=-1)
    yc = y_col0 // gw
    return pl.pallas_call(
        functools.partial(_lru_kernel, t_total=t, n_ctx=n_ctx),
        grid=(bsz, LRU_BLOCKS // blocks_per_step),
        in_specs=[pl.BlockSpec((None, t, gw), lambda i, j: (i, 0, j)),
                  pl.BlockSpec((None, t, gw), lambda i, j: (i, 0, yc + j)),
                  pl.BlockSpec((blocks_per_step, bw, 4 * bw), lambda i, j: (j, 0, 0)),
                  pl.BlockSpec((blocks_per_step, 1, 4 * bw), lambda i, j: (j, 0, 0)),
                  pl.BlockSpec((2, gw), lambda i, j: (0, j))],
        out_specs=pl.BlockSpec((None, t, gw), lambda i, j: (i, 0, j)),
        out_shape=jax.ShapeDtypeStruct((bsz, t, mix_w), BF16),
        scratch_shapes=[pltpu.VMEM((t, gw), F32)] + [pltpu.VMEM((SCAN_CHUNK, gw), F32)] * 4,
        compiler_params=_params("parallel", "arbitrary"), name="rglru",
    )(xc, u_y, w, bias, lam)


def _hg_tables():
    c = HG_CHUNK
    idx = np.arange(c)
    sel = np.zeros((2, HG_LEVELS, c, c), np.float32)
    rq = np.zeros((2, HG_LEVELS, c, HG_DK), np.float32)
    msk = np.zeros((2, HG_LEVELS, c, c), np.float32)
    for lvl in range(HG_LEVELS):
        half = 1 << lvl
        mid = (idx // (2 * half)) * 2 * half + half
        late = idx >= mid
        same = (idx[:, None] // (2 * half)) == (idx[None, :] // (2 * half))
        sel[0, lvl, idx, mid - 1] = 1.0
        sel[1, lvl, idx, mid] = 1.0
        rq[0, lvl] = late[:, None]
        rq[1, lvl] = ~late[:, None]
        msk[0, lvl] = same & late[:, None] & ~late[None, :]
        msk[1, lvl] = same & ~late[:, None] & late[None, :]
    return sel, rq, msk


def _hg_kernel(q_ref, f0_ref, f1_ref, v_ref, g_ref, lb_ref, nw_ref, sel_ref, rq_ref, msk_ref, o_ref, acc,
               *, t_total, n_ctx):
    c = HG_CHUNK
    n_chunks, n_ctx_chunks = t_total // c, n_ctx // c
    ii = lax.broadcasted_iota(jnp.int32, (c, c), 0)
    jj = lax.broadcasted_iota(jnp.int32, (c, c), 1)

    def chunk(off, st, d):
        fwd = d == 0
        lb = lb_ref[d:d + 1, :]
        f = (f0_ref if fwd else f1_ref)[pl.ds(off, c), :]
        log_f = jnp.log(lb + (1.0 - lb) * _sigmoid(f))
        k = (1.0 - lb) * _sigmoid(-f)
        q = _silu(q_ref[pl.ds(off, c), :])
        v = v_ref[pl.ds(off, c), :].astype(BF16)
        order = (ii >= jj) if fwd else (jj >= ii)
        cum = _dot(order.astype(F32), log_f, HI)
        scores = jnp.where(ii == jj, _dot_nt(q.astype(BF16), k.astype(BF16)), 0.0)
        for lvl in range(HG_LEVELS):
            e = cum - _dot(sel_ref[d, lvl], cum, HI)
            rq = rq_ref[d, lvl]
            qt = (q * jnp.exp(e * rq)).astype(BF16)
            kt = (k * jnp.exp(e * (rq - 1.0))).astype(BF16)
            scores = scores + _dot_nt(qt, kt) * msk_ref[d, lvl]
        last = cum[c - 1:c, :] if fwd else cum[0:1, :]
        o = _dot(scores.astype(BF16), v) + _dot_nt((q * jnp.exp(cum)).astype(BF16), st.astype(BF16))
        st = st * jnp.exp(last) + _dot_tn(v, (k * jnp.exp(last - cum)).astype(BF16))
        return o, st

    def fwd_body(s, st):
        off = pl.multiple_of(s * c, c)
        o, st = chunk(off, st, 0)
        acc[pl.ds(off, c), :] = o
        return st

    def bwd_body(s, st):
        off = pl.multiple_of(_bwd_chunk(s, n_chunks, n_ctx_chunks) * c, c)
        o, st = chunk(off, st, 1)
        o = o + acc[pl.ds(off, c), :]
        y = o * lax.rsqrt(jnp.mean(o * o, axis=-1, keepdims=True) + EPS) * nw_ref[...]
        o_ref[pl.ds(off, c), :] = (y * _silu(g_ref[pl.ds(off, c), :])).astype(o_ref.dtype)
        return st

    zero = jnp.zeros((HG_DK, HG_DK), F32)
    lax.fori_loop(0, n_chunks, fwd_body, zero)
    lax.fori_loop(0, n_chunks, bwd_body, zero)


def _hgrn2(u, col0, mix_w, lower, norm_w, *, n_ctx):
    bsz, t, _ = u.shape
    heads = mix_w // HG_DK
    c0 = col0 // HG_DK
    sel, rq, msk = (jnp.asarray(v) for v in _hg_tables())

    def spec(k):
        return pl.BlockSpec((None, t, HG_DK), lambda i, h, k=k: (i, 0, c0 + k * heads + h))

    def table(v):
        return pl.BlockSpec(v.shape, lambda i, h: (0, 0, 0, 0))

    return pl.pallas_call(
        functools.partial(_hg_kernel, t_total=t, n_ctx=n_ctx),
        grid=(bsz, heads),
        in_specs=[spec(0), spec(1), spec(2), spec(3), spec(4),
                  pl.BlockSpec((2, HG_DK), lambda i, h: (0, h)),
                  pl.BlockSpec((1, HG_DK), lambda i, h: (0, h)),
                  table(sel), table(rq), table(msk)],
        out_specs=pl.BlockSpec((None, t, HG_DK), lambda i, h: (i, 0, h)),
        out_shape=jax.ShapeDtypeStruct((bsz, t, mix_w), BF16),
        scratch_shapes=[pltpu.VMEM((t, HG_DK), F32)],
        compiler_params=_params("parallel", "arbitrary"), name="hgrn2",
    )(u, u, u, u, u, lower, norm_w.reshape(1, mix_w), sel, rq, msk)


def _ada_kernel(c_ref, w_ref, b_ref, o_ref):
    o_ref[...] = _dot(_silu(c_ref[...]).astype(BF16), w_ref[...].astype(BF16)) + b_ref[...]


def _ada(cond, ada_w, ada_b, tn=1024):
    r, d = cond.shape
    depth, _, n = ada_w.shape
    return pl.pallas_call(
        _ada_kernel,
        grid=(depth, n // tn),
        in_specs=[pl.BlockSpec((r, d), lambda l, j: (0, 0)),
                  pl.BlockSpec((None, d, tn), lambda l, j: (l, 0, j)),
                  pl.BlockSpec((None, 1, tn), lambda l, j: (l, 0, j))],
        out_specs=pl.BlockSpec((None, r, tn), lambda l, j: (l, 0, j)),
        out_shape=jax.ShapeDtypeStruct((depth, r, n), F32),
        compiler_params=_params("parallel", "parallel"), name="ada",
    )(cond, ada_w, ada_b.reshape(depth, 1, n))


def _rope_tables(n_ctx, n_lat):
    r, col = jnp.meshgrid(jnp.arange(n_lat // GRID_W), jnp.arange(GRID_W), indexing='ij')
    n_freq = RET_DK // 4
    freqs = ROPE_BASE ** (-jnp.arange(n_freq, dtype=F32) / n_freq)
    ang = jnp.concatenate([r.reshape(-1, 1) * freqs, col.reshape(-1, 1) * freqs], axis=-1)
    cos, sin = jnp.cos(ang), jnp.sin(ang)
    cos2 = jnp.concatenate([jnp.ones((n_ctx, RET_DK), F32), jnp.concatenate([cos, cos], axis=-1)], axis=0)
    sin2 = jnp.concatenate([jnp.zeros((n_ctx, RET_DK), F32), jnp.concatenate([-sin, sin], axis=-1)], axis=0)
    return cos2, sin2


def kernel(x, c, ctx, c_ctx, ada_w, ada_b, norm1_w, norm2_w, w_in, ssd_conv_w, ssd_conv_b, ssd_a_log, ssd_dt_bias, ssd_d, ssd_norm_w, lru_conv_w, lru_conv_b, lru_wa, lru_ba, lru_wx, lru_bx, lru_lambda, hg_lb_logits, hg_norm_w, w_branch, w_out, ffn_wgu, ffn_w2, router_w, moe_wgu, moe_w2, final_norm_w):
    bsz, n_lat, d = x.shape
    n_ctx = ctx.shape[1]
    t = n_ctx + n_lat
    depth = w_in.shape[0]
    mix_w = d // 2
    ssd_heads = mix_w // SSD_HD
    xbc_w = mix_w + 2 * SSD_GROUPS * SSD_STATE
    widths = (4 * mix_w, mix_w + xbc_w, 2 * ssd_heads, 2 * mix_w, 5 * mix_w, N_BRANCH * d)
    starts = np.concatenate([[0], np.cumsum(widths)])

    cos2, sin2 = _rope_tables(n_ctx, n_lat)
    p_lb = jax.nn.softmax(hg_lb_logits.astype(F32), axis=1)
    lower_bounds = jnp.cumsum(p_lb, axis=1) - p_lb[:, :1]

    pad = (-(1 + bsz)) % SUBLANES
    cond = jnp.concatenate([c_ctx[None], c, jnp.zeros((pad, d), F32)], axis=0)
    mods = _ada(cond, ada_w, ada_b).reshape(depth, -1, N_MOD, d)

    h = jnp.concatenate([ctx.astype(x.dtype), x], axis=1)
    for layer in range(depth):
        m_ctx = jnp.broadcast_to(mods[layer, 0], (bsz, N_MOD, d))
        m_lat = mods[layer, 1:1 + bsz]
        mod_rows = jnp.stack([m_ctx, m_lat], axis=1)
        mod_cols = jnp.stack([m_ctx, m_lat], axis=2)

        xn = _norm_call(h, norm1_w[layer], mod_rows, shift=0, scale=1, n_ctx=n_ctx).reshape(bsz * t, d)
        u_ret, u_ssd, u_dt, u_lru, u_hg, u_gates = (
            _matmul(xn, w_in[layer, :, starts[i]:starts[i + 1]].astype(BF16)).reshape(bsz, t, widths[i])
            for i in range(len(widths)))

        b_ret = _retention(u_ret, 0, mix_w, cos2, sin2, n_ctx=n_ctx)
        xact = _dwconv(u_ssd, mix_w, xbc_w, ssd_conv_w[layer], ssd_conv_b[layer], n_ctx=n_ctx, act=True)
        b_ssd = _ssd(u_ssd, 0, xact, u_dt, ssd_dt_bias[layer], ssd_a_log[layer], ssd_d[layer],
                     ssd_norm_w[layer], mix_w, n_ctx=n_ctx)
        xc = _dwconv(u_lru, 0, mix_w, lru_conv_w[layer], lru_conv_b[layer], n_ctx=n_ctx, act=False)
        b_lru = _rglru(xc, u_lru, mix_w, lru_wa[layer], lru_ba[layer], lru_wx[layer], lru_bx[layer],
                       lru_lambda[layer], mix_w, n_ctx=n_ctx)
        b_hg = _hgrn2(u_hg, 0, mix_w, lower_bounds[:, layer], hg_norm_w[layer], n_ctx=n_ctx)

        branches = [v.reshape(bsz * t, mix_w) for v in (b_ret, b_ssd, b_lru, b_hg)]
        mixed = _merge(branches, u_gates.reshape(bsz * t, N_BRANCH * d), w_branch[layer].astype(BF16), tm=768)
        h2 = _matmul_residual(mixed, w_out[layer].astype(BF16), h.reshape(bsz * t, d), mod_cols, 2,
                              t_total=t, n_ctx=n_ctx)

        if layer % 2 == 0:
            xn = _norm_call(h2.reshape(bsz, t, d), norm2_w[layer], mod_rows, shift=3, scale=4, n_ctx=n_ctx)
            act = _swiglu_up(xn.reshape(bsz * t, d), ffn_wgu[layer // 2].astype(BF16))
            w2 = ffn_w2[layer // 2].astype(BF16)
        else:
            xn, gates = _norm_call(h2.reshape(bsz, t, d), norm2_w[layer], mod_rows, shift=3, scale=4, n_ctx=n_ctx,
                                   router_w=router_w[layer // 2])
            act = _moe_up(xn.reshape(bsz * t, d), moe_wgu[layer // 2].astype(BF16), gates.reshape(bsz * t, LANES))
            w2 = moe_w2[layer // 2].astype(BF16).reshape(-1, d)
        h = _matmul_residual(act, w2, h2, mod_cols, 5, t_total=t, n_ctx=n_ctx).reshape(bsz, t, d)
    return _final_norm_call(h, final_norm_w, n_ctx)
```

```python
import functools
import math

import numpy as np
import jax
import jax.numpy as jnp
from jax import lax
from jax.experimental import pallas as pl
from jax.experimental.pallas import tpu as pltpu

F32 = jnp.float32
BF16 = jnp.bfloat16
HI = lax.Precision.HIGHEST

EPS = 1e-6
GRID_W = 64
ROPE_BASE = 10000.0
RET_DK = 128
SSD_HD = 64
SSD_GROUPS = 4
SSD_STATE = 128
CONV_W = 4
LRU_BLOCKS = 8
LRU_C = 8.0
HG_DK = 128
N_BRANCH = 4
N_EXPERTS = 8
N_MOD = 6

LANES = 128
SUBLANES = 8
VMEM_LIMIT = 52 * 1024 * 1024

SCAN_CHUNK = 256
HG_CHUNK = 128
HG_LEVELS = 7


def _params(*sem):
    return pltpu.CompilerParams(dimension_semantics=sem, vmem_limit_bytes=VMEM_LIMIT)


def _dot(a, b, prec=None):
    return jnp.dot(a, b, preferred_element_type=F32, precision=prec)


def _dot_nt(a, b, prec=None):
    return lax.dot_general(a, b, (((1,), (1,)), ((), ())), preferred_element_type=F32, precision=prec)


def _dot_tn(a, b, prec=None):
    return lax.dot_general(a, b, (((0,), (0,)), ((), ())), preferred_element_type=F32, precision=prec)


def _sigmoid(x):
    return 1.0 / (1.0 + jnp.exp(-x))


def _silu(x):
    return x * _sigmoid(x)


def _softplus(x):
    return jnp.maximum(x, 0.0) + jnp.log1p(jnp.exp(-jnp.abs(x)))


def _gelu_tanh(x):
    return 0.5 * x * (1.0 + jnp.tanh(math.sqrt(2.0 / math.pi) * (x + 0.044715 * (x * x * x))))


def _bwd_chunk(s, n_chunks, n_ctx_chunks):
    return jnp.where(s < n_ctx_chunks, n_ctx_chunks - 1 - s, n_chunks - 1 - (s - n_ctx_chunks))


def _scan_both_ways(n_chunks, n_ctx_chunks, c, chunk_fn, acc_f, acc_b, zero_state, post_fn):
    def body(s, carry):
        st_f, st_b = carry
        off_f = pl.multiple_of(s * c, c)
        off_b = pl.multiple_of(_bwd_chunk(s, n_chunks, n_ctx_chunks) * c, c)
        o_f, st_f = chunk_fn(off_f, st_f, 0)
        o_b, st_b = chunk_fn(off_b, st_b, 1)
        acc_f[pl.ds(off_f, c), :] = o_f
        acc_b[pl.ds(off_b, c), :] = o_b
        return st_f, st_b

    lax.fori_loop(0, n_chunks, body, (zero_state, zero_state))

    def finish(s, _):
        post_fn(pl.multiple_of(s * c, c))
        return 0

    lax.fori_loop(0, n_chunks, finish, 0)


def _scan_order_tables(c):
    idx = np.arange(c)
    lower = (idx[None, :] <= idx[:, None]).astype(np.float32)
    return jnp.asarray(np.stack([lower, lower.T]), BF16)


def _split_bf16(x, parts):
    out = []
    for _ in range(parts):
        p = x.astype(BF16)
        out.append(p)
        x = x - p.astype(F32)
    return out


def _norm_kernel(h_ref, w_ref, mod_ref, o_ref, *, shift, scale):
    x = h_ref[...]
    y = x * lax.rsqrt(jnp.mean(x * x, axis=-1, keepdims=True) + EPS) * w_ref[...]
    o_ref[...] = (y * (1.0 + mod_ref[scale:scale + 1, :]) + mod_ref[shift:shift + 1, :]).astype(o_ref.dtype)


def _norm_router_kernel(h_ref, w_ref, mod_ref, rw_ref, o_ref, gate_ref, *, shift, scale):
    x = h_ref[...]
    y = x * lax.rsqrt(jnp.mean(x * x, axis=-1, keepdims=True) + EPS) * w_ref[...]
    xn = y * (1.0 + mod_ref[scale:scale + 1, :]) + mod_ref[shift:shift + 1, :]
    o_ref[...] = xn.astype(o_ref.dtype)
    logits = _dot(xn.astype(BF16), rw_ref[...])
    lane = lax.broadcasted_iota(jnp.int32, logits.shape, 1)
    neg = jnp.float32(-jnp.inf)
    logits = jnp.where(lane < N_EXPERTS, logits, neg)
    v1 = jnp.max(logits, axis=-1, keepdims=True)
    i1 = jnp.min(jnp.where(logits == v1, lane, LANES), axis=-1, keepdims=True)
    rest = jnp.where(lane == i1, neg, logits)
    v2 = jnp.max(rest, axis=-1, keepdims=True)
    i2 = jnp.min(jnp.where(rest == v2, lane, LANES), axis=-1, keepdims=True)
    e2 = jnp.exp(v2 - v1)
    w1 = 1.0 / (1.0 + e2)
    w2 = e2 / (1.0 + e2)
    gate_ref[...] = jnp.where(lane == i1, w1, 0.0) + jnp.where(lane == i2, w2, 0.0)


def _norm_call(h, w, mod, *, shift, scale, n_ctx, router_w=None, out_dtype=BF16):
    b, t, d = h.shape
    tr = n_ctx
    grid = (b, t // tr)
    h_spec = pl.BlockSpec((None, tr, d), lambda i, j: (i, j, 0))
    w_spec = pl.BlockSpec((1, d), lambda i, j: (0, 0))
    mod_spec = pl.BlockSpec((None, None, N_MOD, d), lambda i, j: (i, jnp.minimum(j, 1), 0, 0))
    if router_w is None:
        return pl.pallas_call(
            functools.partial(_norm_kernel, shift=shift, scale=scale),
            grid=grid, in_specs=[h_spec, w_spec, mod_spec], out_specs=h_spec,
            out_shape=jax.ShapeDtypeStruct(h.shape, out_dtype),
            compiler_params=_params("parallel", "parallel"), name="norm_mod",
        )(h, w.reshape(1, d), mod)
    rw = jnp.zeros((d, LANES), BF16).at[:, :N_EXPERTS].set(router_w.astype(BF16))
    return pl.pallas_call(
        functools.partial(_norm_router_kernel, shift=shift, scale=scale),
        grid=grid,
        in_specs=[h_spec, w_spec, mod_spec, pl.BlockSpec((d, LANES), lambda i, j: (0, 0))],
        out_specs=[h_spec, pl.BlockSpec((None, tr, LANES), lambda i, j: (i, j, 0))],
        out_shape=[jax.ShapeDtypeStruct(h.shape, out_dtype), jax.ShapeDtypeStruct((b, t, LANES), F32)],
        compiler_params=_params("parallel", "parallel"), name="norm_mod_router",
    )(h, w.reshape(1, d), mod, rw)


def _final_norm_kernel(h_ref, w_ref, o_ref):
    x = h_ref[...]
    o_ref[...] = x * lax.rsqrt(jnp.mean(x * x, axis=-1, keepdims=True) + EPS) * w_ref[...]


def _final_norm_call(h, w, n_ctx):
    b, t, d = h.shape
    tr = n_ctx
    return pl.pallas_call(
        _final_norm_kernel,
        grid=(b, (t - n_ctx) // tr),
        in_specs=[pl.BlockSpec((None, tr, d), lambda i, j: (i, j + 1, 0)), pl.BlockSpec((1, d), lambda i, j: (0, 0))],
        out_specs=pl.BlockSpec((None, tr, d), lambda i, j: (i, j, 0)),
        out_shape=jax.ShapeDtypeStruct((b, t - n_ctx, d), F32),
        compiler_params=_params("parallel", "parallel"), name="final_norm",
    )(h, w.reshape(1, d))


def _pick(n, prefs):
    for p in prefs:
        if n % p == 0:
            return p
    return n


def _mm_kernel(a_ref, w_ref, o_ref, acc_ref, *, nk):
    k = pl.program_id(2)
    part = _dot(a_ref[...], w_ref[...])
    if nk == 1:
        o_ref[...] = part.astype(o_ref.dtype)
        return

    @pl.when(k == 0)
    def _():
        acc_ref[...] = part

    @pl.when(k > 0)
    def _():
        acc_ref[...] += part

    @pl.when(k == nk - 1)
    def _():
        o_ref[...] = acc_ref[...].astype(o_ref.dtype)


def _matmul(a, w, out_dtype=F32, tm=None, tn=None, tk=None):
    m, kd = a.shape
    n = w.shape[1]
    tm = tm or _pick(m, (1152, 1024, 768, 512, 256))
    tn = tn or _pick(n, (1024, 512, 256, 128))
    tk = tk or (kd if kd <= 2048 else _pick(kd, (512, 256, 128)))
    nk = kd // tk
    return pl.pallas_call(
        functools.partial(_mm_kernel, nk=nk),
        grid=(n // tn, m // tm, nk),
        in_specs=[pl.BlockSpec((tm, tk), lambda j, i, k: (i, k)), pl.BlockSpec((tk, tn), lambda j, i, k: (k, j))],
        out_specs=pl.BlockSpec((tm, tn), lambda j, i, k: (i, j)),
        out_shape=jax.ShapeDtypeStruct((m, n), out_dtype),
        scratch_shapes=[pltpu.VMEM((tm, tn), F32)],
        compiler_params=_params("parallel", "parallel", "arbitrary"), name="matmul",
    )(a, w)


def _row_is_ctx(tm, t_total, n_ctx):
    row0 = pl.program_id(1) * tm
    pos = (row0 + lax.broadcasted_iota(jnp.int32, (tm, 1), 0)) % t_total
    return pos < n_ctx


def _mm_res_kernel(a_ref, w_ref, res_ref, mod_ref, o_ref, acc_ref, *, nk, tm, t_total, n_ctx):
    k = pl.program_id(2)
    part = _dot(a_ref[...], w_ref[...])

    @pl.when(k == 0)
    def _():
        acc_ref[...] = part

    @pl.when(k > 0)
    def _():
        acc_ref[...] += part

    @pl.when(k == nk - 1)
    def _():
        mod = jnp.where(_row_is_ctx(tm, t_total, n_ctx), mod_ref[0:1, :], mod_ref[1:2, :])
        o_ref[...] = res_ref[...] + mod * acc_ref[...]


def _matmul_residual(a, w, res, mod, mod_idx, *, t_total, n_ctx, tm=None, tn=None, tk=None):
    m, kd = a.shape
    n = w.shape[1]
    tm = tm or _pick(t_total, (1152, 768, 256))
    tn = tn or _pick(n, (1024, 512, 256, 128))
    tk = tk or (kd if kd <= 2048 else _pick(kd, (512, 256, 128)))
    nk = kd // tk
    per_b = t_total // tm
    return pl.pallas_call(
        functools.partial(_mm_res_kernel, nk=nk, tm=tm, t_total=t_total, n_ctx=n_ctx),
        grid=(n // tn, m // tm, nk),
        in_specs=[pl.BlockSpec((tm, tk), lambda j, i, k: (i, k)),
                  pl.BlockSpec((tk, tn), lambda j, i, k: (k, j)),
                  pl.BlockSpec((tm, tn), lambda j, i, k: (i, j)),
                  pl.BlockSpec((None, None, 2, tn), lambda j, i, k: (i // per_b, mod_idx, 0, j))],
        out_specs=pl.BlockSpec((tm, tn), lambda j, i, k: (i, j)),
        out_shape=jax.ShapeDtypeStruct((m, n), F32),
        scratch_shapes=[pltpu.VMEM((tm, tn), F32)],
        compiler_params=_params("parallel", "parallel", "arbitrary"), name="matmul_residual",
    )(a, w, res, mod)


def _swiglu_kernel(a_ref, wg_ref, wu_ref, o_ref):
    a = a_ref[...]
    g = _dot(a, wg_ref[...])
    u = _dot(a, wu_ref[...])
    o_ref[...] = (_silu(g) * u).astype(o_ref.dtype)


def _swiglu_up(a, wgu, tm=None, tn=None):
    m, kd = a.shape
    f = wgu.shape[1] // 2
    tm = tm or _pick(m, (1152, 1024, 768, 512, 256))
    tn = tn or _pick(f, (512, 256, 128))
    nf = f // tn
    return pl.pallas_call(
        _swiglu_kernel,
        grid=(nf, m // tm),
        in_specs=[pl.BlockSpec((tm, kd), lambda j, i: (i, 0)),
                  pl.BlockSpec((kd, tn), lambda j, i: (0, j)),
                  pl.BlockSpec((kd, tn), lambda j, i: (0, j + nf))],
        out_specs=pl.BlockSpec((tm, tn), lambda j, i: (i, j)),
        out_shape=jax.ShapeDtypeStruct((m, f), BF16),
        compiler_params=_params("parallel", "parallel"), name="swiglu_up",
    )(a, wgu, wgu)


def _moe_up_kernel(a_ref, wg_ref, wu_ref, gate_ref, o_ref, *, n_sub):
    e = pl.program_id(0) // n_sub
    a = a_ref[...]
    g = _dot(a, wg_ref[...])
    u = _dot(a, wu_ref[...])
    gates = gate_ref[...]
    lane = lax.broadcasted_iota(jnp.int32, gates.shape, 1)
    ge = jnp.sum(jnp.where(lane == e, gates, 0.0), axis=-1, keepdims=True)
    o_ref[...] = (_silu(g) * u * ge).astype(o_ref.dtype)


def _moe_up(a, wgu, gates, tm=None, tn=None):
    m, kd = a.shape
    ne, _, f2 = wgu.shape
    f = f2 // 2
    tm = tm or _pick(m, (1152, 1024, 768, 512, 256))
    tn = tn or _pick(f, (512, 256, 128))
    nf = f // tn
    return pl.pallas_call(
        functools.partial(_moe_up_kernel, n_sub=nf),
        grid=(ne * nf, m // tm),
        in_specs=[pl.BlockSpec((tm, kd), lambda j, i: (i, 0)),
                  pl.BlockSpec((None, kd, tn), lambda j, i: (j // nf, 0, j % nf)),
                  pl.BlockSpec((None, kd, tn), lambda j, i: (j // nf, 0, j % nf + nf)),
                  pl.BlockSpec((tm, LANES), lambda j, i: (i, 0))],
        out_specs=pl.BlockSpec((tm, tn), lambda j, i: (i, j)),
        out_shape=jax.ShapeDtypeStruct((m, ne * f), BF16),
        compiler_params=_params("parallel", "parallel"), name="moe_up",
    )(a, wgu, wgu, gates)


def _merge_kernel(b0, b1, b2, b3, w_ref, g0, g1, g2, g3, o_ref):
    acc = None
    for n, (br, g) in enumerate(((b0, g0), (b1, g1), (b2, g2), (b3, g3))):
        term = _sigmoid(g[...]) * _dot(br[...], w_ref[n])
        acc = term if acc is None else acc + term
    o_ref[...] = acc.astype(o_ref.dtype)


def _merge(branches, gates, wb, tm=None, tn=None):
    m, kd = branches[0].shape
    d = wb.shape[-1]
    tm = tm or _pick(m, (1152, 1024, 768, 512, 256))
    tn = tn or _pick(d, (512, 256, 128))
    nd = d // tn
    br_spec = pl.BlockSpec((tm, kd), lambda j, i: (i, 0))
    gate_specs = [pl.BlockSpec((tm, tn), functools.partial(lambda j, i, n: (i, j + n * nd), n=n))
                  for n in range(N_BRANCH)]
    return pl.pallas_call(
        _merge_kernel,
        grid=(nd, m // tm),
        in_specs=[br_spec] * N_BRANCH + [pl.BlockSpec((N_BRANCH, kd, tn), lambda j, i: (0, 0, j))] + gate_specs,
        out_specs=pl.BlockSpec((tm, tn), lambda j, i: (i, j)),
        out_shape=jax.ShapeDtypeStruct((m, d), BF16),
        compiler_params=_params("parallel", "parallel"), name="merge",
    )(*branches, wb, gates, gates, gates, gates)


def _conv_kernel(u_ref, w_ref, b_ref, o_ref, *, t_total, n_ctx, act):
    x = u_ref[...]
    pos = lax.broadcasted_iota(jnp.int32, (t_total, 1), 0)
    seg_lo = jnp.where(pos < n_ctx, 0, n_ctx)
    seg_hi = jnp.where(pos < n_ctx, n_ctx, t_total)
    left = (CONV_W - 1) // 2
    acc = None
    for j in range(CONV_W):
        d = j - left
        xs = x if d == 0 else pltpu.roll(x, (-d) % t_total, 0)
        ok = (pos + d >= seg_lo) & (pos + d < seg_hi)
        term = jnp.where(ok, xs, 0.0) * w_ref[j:j + 1, :]
        acc = term if acc is None else acc + term
    acc = acc + b_ref[...]
    o_ref[...] = _silu(acc) if act else acc


def _dwconv(u, col0, width, w, b, *, n_ctx, act, wt=256):
    bsz, t, _ = u.shape
    c0 = col0 // wt
    return pl.pallas_call(
        functools.partial(_conv_kernel, t_total=t, n_ctx=n_ctx, act=act),
        grid=(bsz, width // wt),
        in_specs=[pl.BlockSpec((None, t, wt), lambda i, j: (i, 0, j + c0)),
                  pl.BlockSpec((CONV_W, wt), lambda i, j: (0, j)),
                  pl.BlockSpec((1, wt), lambda i, j: (0, j))],
        out_specs=pl.BlockSpec((None, t, wt), lambda i, j: (i, 0, j)),
        out_shape=jax.ShapeDtypeStruct((bsz, t, width), F32),
        compiler_params=_params("parallel", "parallel"), name="dwconv",
    )(u, w, b.reshape(1, width))


def _ret_kernel(lg_ref, q_ref, k_ref, v_ref, g_ref, cos_ref, sin_ref, o_ref, qs, ks, acc_f, acc_b,
                *, t_total, n_ctx):
    c = SCAN_CHUNK
    n_chunks, n_ctx_chunks = t_total // c, n_ctx // c
    lg = lg_ref[pl.program_id(1)]
    cos, sin = cos_ref[...], sin_ref[...]
    q, k = q_ref[...], k_ref[...]
    qs[...] = q * cos + pltpu.roll(q, RET_DK // 2, 1) * sin
    ks[...] = (k * cos + pltpu.roll(k, RET_DK // 2, 1) * sin) * (RET_DK ** -0.5)

    ii = lax.broadcasted_iota(jnp.int32, (c, c), 0)
    jj = lax.broadcasted_iota(jnp.int32, (c, c), 1)
    dist = (ii - jj).astype(F32)
    r = lax.broadcasted_iota(jnp.int32, (c, 1), 0).astype(F32)
    decay_all = jnp.exp(c * lg)

    def chunk(off, s_in, fwd):
        qc = qs[pl.ds(off, c), :]
        kc = ks[pl.ds(off, c), :]
        vc = v_ref[pl.ds(off, c), :].astype(BF16)
        if fwd:
            mask, steps, q_pow, k_pow = ii >= jj, dist, r + 1.0, c - 1.0 - r
        else:
            mask, steps, q_pow, k_pow = jj >= ii, -dist, c - r, r
        decay = jnp.where(mask, jnp.exp(jnp.where(mask, steps, 0.0) * lg), 0.0)
        scores = _dot_nt(qc.astype(BF16), kc.astype(BF16)) * decay
        o = _dot(scores.astype(BF16), vc) + jnp.exp(q_pow * lg) * _dot(qc.astype(BF16), s_in.astype(BF16))
        s_out = decay_all * s_in + _dot_tn((kc * jnp.exp(k_pow * lg)).astype(BF16), vc)
        return o, s_out

    def post(off):
        o = acc_f[pl.ds(off, c), :] + acc_b[pl.ds(off, c), :]
        y = o * lax.rsqrt(jnp.mean(o * o, axis=-1, keepdims=True) + EPS)
        o_ref[pl.ds(off, c), :] = (y * _silu(g_ref[pl.ds(off, c), :])).astype(o_ref.dtype)

    _scan_both_ways(n_chunks, n_ctx_chunks, c, lambda off, st, d: chunk(off, st, d == 0), acc_f, acc_b,
                    jnp.zeros((RET_DK, RET_DK), F32), post)


def _retention(u, col0, mix_w, cos2, sin2, *, n_ctx):
    bsz, t, _ = u.shape
    heads = mix_w // RET_DK
    c0 = col0 // RET_DK
    log_gamma = jnp.log1p(-jnp.exp2(-5.0 - jnp.arange(heads, dtype=F32)))

    def spec(k):
        return pl.BlockSpec((None, t, RET_DK), lambda i, h, lg, k=k: (i, 0, c0 + k * heads + h))

    table = pl.BlockSpec((t, RET_DK), lambda i, h, lg: (0, 0))
    return pl.pallas_call(
        functools.partial(_ret_kernel, t_total=t, n_ctx=n_ctx),
        grid_spec=pltpu.PrefetchScalarGridSpec(
            num_scalar_prefetch=1, grid=(bsz, heads),
            in_specs=[spec(0), spec(1), spec(2), spec(3), table, table],
            out_specs=pl.BlockSpec((None, t, RET_DK), lambda i, h, lg: (i, 0, h)),
            scratch_shapes=[pltpu.VMEM((t, RET_DK), F32)] * 4),
        out_shape=jax.ShapeDtypeStruct((bsz, t, mix_w), BF16),
        compiler_params=_params("parallel", "arbitrary"), name="retention",
    )(log_gamma, u, u, u, u, cos2, sin2)


def _ssd_kernel(xs_ref, bm_ref, cm_ref, z_ref, dt_ref, dtb_ref, alog_ref, dsk_ref, nw_ref, sel_ref, ord_ref, o_ref,
                acc_f, acc_b, dts, las, *, t_total, n_ctx):
    c = SCAN_CHUNK
    rep = sel_ref.shape[1] // 2
    gw = rep * SSD_HD
    n_chunks, n_ctx_chunks = t_total // c, n_ctx // c
    dt_all = _softplus(dt_ref[...] + dtb_ref[...])
    dts[...] = dt_all
    las[...] = dt_all * (-jnp.exp(alog_ref[...]))

    ii = lax.broadcasted_iota(jnp.int32, (c, c), 0)
    jj = lax.broadcasted_iota(jnp.int32, (c, c), 1)
    lane_head = lax.broadcasted_iota(jnp.int32, (1, gw), 1) // SSD_HD

    def chunk(off, s_in, d):
        fwd = d == 0
        mask = (ii >= jj) if fwd else (jj >= ii)
        la_parts = _split_bf16(las[pl.ds(off, c), :], 3)
        cum = sum(_dot(ord_ref[d], p) for p in la_parts)
        cum_rows = sum(_dot_tn(p, ord_ref[1 - d]) for p in la_parts)
        sel = sel_ref[d]
        cum_e = sum(_dot(p, sel) for p in _split_bf16(cum, 3))
        dt_e = sum(_dot(p, sel) for p in _split_bf16(dts[pl.ds(off, c), :], 3))
        last_e = cum_e[c - 1:c, :] if fwd else cum_e[0:1, :]
        xs = xs_ref[pl.ds(off, c), :]
        vdt = xs * dt_e
        cm = cm_ref[pl.ds(off, c), :].astype(BF16)
        bm = bm_ref[pl.ds(off, c), :].astype(BF16)
        qk = _dot_nt(cm, bm)
        o = jnp.exp(cum_e) * _dot(cm, s_in.astype(BF16))
        for i in range(rep):
            col = d * rep + i
            diff = cum[:, col:col + 1] - cum_rows[col:col + 1, :]
            decay = jnp.where(mask, jnp.exp(jnp.where(mask, diff, 0.0)), 0.0)
            o = o + _dot((qk * decay).astype(BF16), jnp.where(lane_head == i, vdt, 0.0).astype(BF16))
        s_out = jnp.exp(last_e) * s_in + _dot_tn(bm, (vdt * jnp.exp(last_e - cum_e)).astype(BF16))
        return o, s_out

    def post(off):
        o = acc_f[pl.ds(off, c), :] + acc_b[pl.ds(off, c), :]
        y = (o + dsk_ref[...] * xs_ref[pl.ds(off, c), :]) * _silu(z_ref[pl.ds(off, c), :])
        y = y * lax.rsqrt(jnp.mean(y * y, axis=-1, keepdims=True) + EPS)
        o_ref[pl.ds(off, c), :] = (y * nw_ref[...]).astype(o_ref.dtype)

    _scan_both_ways(n_chunks, n_ctx_chunks, c, chunk, acc_f, acc_b, jnp.zeros((SSD_STATE, gw), F32), post)


def _ssd(u_z, z_col0, xact, u_dt, dt_bias, a_log, d_skip, norm_w, mix_w, *, n_ctx):
    bsz, t, _ = xact.shape
    heads = mix_w // SSD_HD
    rep = heads // SSD_GROUPS
    gw = rep * SSD_HD
    def per_group(v):
        lead = v.shape[:-1]
        v = v.reshape(*lead, 2, SSD_GROUPS, rep)
        return jnp.moveaxis(v, -2, 0).reshape(SSD_GROUPS, *lead, 2 * rep)
    dt_g = jnp.moveaxis(per_group(u_dt), 0, 1)
    dtb_g = per_group(dt_bias.reshape(1, 2 * heads))
    alog_g = per_group(a_log.reshape(1, 2 * heads))
    sel = np.zeros((2, 2 * rep, gw), np.float32)
    for d in range(2):
        for i in range(rep):
            sel[d, d * rep + i, i * SSD_HD:(i + 1) * SSD_HD] = 1.0
    dsk = jnp.repeat(d_skip, SSD_HD).reshape(1, mix_w)
    zc, bc, cc = z_col0 // gw, mix_w // SSD_STATE, mix_w // SSD_STATE + SSD_GROUPS
    small = lambda: pl.BlockSpec((None, 1, 2 * rep), lambda i, g: (g, 0, 0))
    return pl.pallas_call(
        functools.partial(_ssd_kernel, t_total=t, n_ctx=n_ctx),
        grid=(bsz, SSD_GROUPS),
        in_specs=[pl.BlockSpec((None, t, gw), lambda i, g: (i, 0, g)),
                  pl.BlockSpec((None, t, SSD_STATE), lambda i, g: (i, 0, bc + g)),
                  pl.BlockSpec((None, t, SSD_STATE), lambda i, g: (i, 0, cc + g)),
                  pl.BlockSpec((None, t, gw), lambda i, g: (i, 0, zc + g)),
                  pl.BlockSpec((None, None, t, 2 * rep), lambda i, g: (i, g, 0, 0)),
                  small(), small(),
                  pl.BlockSpec((1, gw), lambda i, g: (0, g)),
                  pl.BlockSpec((1, gw), lambda i, g: (0, g)),
                  pl.BlockSpec((2, 2 * rep, gw), lambda i, g: (0, 0, 0)),
                  pl.BlockSpec((2, SCAN_CHUNK, SCAN_CHUNK), lambda i, g: (0, 0, 0))],
        out_specs=pl.BlockSpec((None, t, gw), lambda i, g: (i, 0, g)),
        out_shape=jax.ShapeDtypeStruct((bsz, t, mix_w), BF16),
        scratch_shapes=[pltpu.VMEM((t, gw), F32), pltpu.VMEM((t, gw), F32),
                        pltpu.VMEM((t, 2 * rep), F32), pltpu.VMEM((t, 2 * rep), F32)],
        compiler_params=_params("parallel", "arbitrary"), name="ssd",
    )(xact, xact, xact, u_z, dt_g, dtb_g, alog_g, dsk, norm_w.reshape(1, mix_w), jnp.asarray(sel, BF16),
      _scan_order_tables(SCAN_CHUNK))


def _tile_scan(a, b, row, reverse):
    for s in (1, 2, 4):
        if reverse:
            ok, shift = row < SUBLANES - s, SUBLANES - s
        else:
            ok, shift = row >= s, s
        b = b + a * jnp.where(ok, pltpu.roll(b, shift, 0), 0.0)
        a = a * jnp.where(ok, pltpu.roll(a, shift, 0), 1.0)
    return a, b


def _lru_kernel(xc_ref, y_ref, w_ref, bias_ref, lam_ref, o_ref, acc, a_f, b_f, a_b, b_b, *, t_total, n_ctx):
    c = SCAN_CHUNK
    nb, bw = w_ref.shape[0], w_ref.shape[1]
    n_chunks, n_ctx_chunks = t_total // c, n_ctx // c
    n_tiles = c // SUBLANES
    log_sig = -_softplus(-lam_ref[...])
    acc[...] = jnp.zeros_like(acc)
    row = lax.broadcasted_iota(jnp.int32, (SUBLANES, bw), 0)

    def gates(off, d, dst_a, dst_b):
        for blk in range(nb):
            cols = slice(blk * bw, (blk + 1) * bw)
            x = xc_ref[pl.ds(off, c), cols]
            g = _dot(x.astype(BF16), w_ref[blk, :, d * 2 * bw:(d + 1) * 2 * bw]) \
                + bias_ref[blk, :, d * 2 * bw:(d + 1) * 2 * bw]
            log_a = LRU_C * _sigmoid(g[:, :bw]) * log_sig[d:d + 1, cols]
            a = jnp.exp(log_a)
            dst_a[:, cols] = a
            dst_b[:, cols] = jnp.sqrt(-jnp.tanh(log_a) * (a * a + 1.0)) * (_sigmoid(g[:, bw:]) * x)

    def step(s, carry):
        off_f = pl.multiple_of(s * c, c)
        off_b = pl.multiple_of(_bwd_chunk(s, n_chunks, n_ctx_chunks) * c, c)
        gates(off_f, 0, a_f, b_f)
        gates(off_b, 1, a_b, b_b)

        def tile(t, carry):
            h_f, h_b = carry
            r_f = pl.multiple_of(t * SUBLANES, SUBLANES)
            r_b = pl.multiple_of((n_tiles - 1 - t) * SUBLANES, SUBLANES)
            new_f, new_b = [], []
            for blk in range(nb):
                cols = slice(blk * bw, (blk + 1) * bw)
                a, b = _tile_scan(a_f[pl.ds(r_f, SUBLANES), cols], b_f[pl.ds(r_f, SUBLANES), cols], row, False)
                h = b + a * h_f[blk]
                acc[pl.ds(off_f + r_f, SUBLANES), cols] += h
                new_f.append(h[SUBLANES - 1:SUBLANES, :])
                a, b = _tile_scan(a_b[pl.ds(r_b, SUBLANES), cols], b_b[pl.ds(r_b, SUBLANES), cols], row, True)
                h = b + a * h_b[blk]
                acc[pl.ds(off_b + r_b, SUBLANES), cols] += h
                new_b.append(h[0:1, :])
            return tuple(new_f), tuple(new_b)

        return lax.fori_loop(0, n_tiles, tile, carry)

    zero = tuple(jnp.zeros((1, bw), F32) for _ in range(nb))
    lax.fori_loop(0, n_chunks, step, (zero, zero))

    def finish(s, _):
        off = pl.multiple_of(s * c, c)
        o_ref[pl.ds(off, c), :] = (acc[pl.ds(off, c), :] * _gelu_tanh(y_ref[pl.ds(off, c), :])).astype(o_ref.dtype)
        return 0

    lax.fori_loop(0, n_chunks, finish, 0)


def _rglru(xc, u_y, y_col0, wa, ba, wx, bx, lam, mix_w, *, n_ctx, blocks_per_step=4):
    bsz, t, _ = xc.shape
    bw = mix_w // LRU_BLOCKS
    gw = blocks_per_step * bw
    w = jnp.concatenate([wa[0], wx[0], wa[1], wx[1]], axis=-1).astype(BF16)
    blk = lambda v: v.reshape(2, LRU_BLOCKS, 1, bw)
    bias = jnp.concatenate([blk(ba)[0], blk(bx)[0], blk(ba)[1], blk(bx)[1]], axis=-1)
    yc = y_col0 // gw
    return pl.pallas_call(
        functools.partial(_lru_kernel, t_total=t, n_ctx=n_ctx),
        grid=(bsz, LRU_BLOCKS // blocks_per_step),
        in_specs=[pl.BlockSpec((None, t, gw), lambda i, j: (i, 0, j)),
                  pl.BlockSpec((None, t, gw), lambda i, j: (i, 0, yc + j)),
                  pl.BlockSpec((blocks_per_step, bw, 4 * bw), lambda i, j: (j, 0, 0)),
                  pl.BlockSpec((blocks_per_step, 1, 4 * bw), lambda i, j: (j, 0, 0)),
                  pl.BlockSpec((2, gw), lambda i, j: (0, j))],
        out_specs=pl.BlockSpec((None, t, gw), lambda i, j: (i, 0, j)),
        out_shape=jax.ShapeDtypeStruct((bsz, t, mix_w), BF16),
        scratch_shapes=[pltpu.VMEM((t, gw), F32)] + [pltpu.VMEM((SCAN_CHUNK, gw), F32)] * 4,
        compiler_params=_params("parallel", "arbitrary"), name="rglru",
    )(xc, u_y, w, bias, lam)


def _hg_tables():
    c = HG_CHUNK
    idx = np.arange(c)
    sel = np.zeros((2, HG_LEVELS, c, c), np.float32)
    isq = np.zeros((2, HG_LEVELS, c, HG_DK), np.float32)
    msk = np.zeros((2, HG_LEVELS, c, c), np.float32)
    for lvl in range(HG_LEVELS):
        half = 1 << lvl
        mid = (idx // (2 * half)) * 2 * half + half
        late = idx >= mid
        same = (idx[:, None] // (2 * half)) == (idx[None, :] // (2 * half))
        sel[0, lvl, idx, mid - 1] = 1.0
        sel[1, lvl, idx, mid] = 1.0
        isq[0, lvl] = late[:, None]
        isq[1, lvl] = ~late[:, None]
        msk[0, lvl] = same & late[:, None] & ~late[None, :]
        msk[1, lvl] = same & ~late[:, None] & late[None, :]
    return (jnp.asarray(sel.reshape(2, HG_LEVELS * c, c), BF16), jnp.asarray(2.0 * isq - 1.0), jnp.asarray(isq),
            jnp.asarray(msk))


def _hg_kernel(q_ref, f0_ref, f1_ref, v_ref, g_ref, lb_ref, nw_ref, ord_ref, sel_ref, sgn_ref, isq_ref, msk_ref,
               o_ref, acc_f, acc_b, *, t_total, n_ctx):
    c = HG_CHUNK
    n_chunks, n_ctx_chunks = t_total // c, n_ctx // c
    eye = lax.broadcasted_iota(jnp.int32, (c, c), 0) == lax.broadcasted_iota(jnp.int32, (c, c), 1)

    def chunk(off, st, d):
        fwd = d == 0
        lb = lb_ref[d:d + 1, :]
        f = (f0_ref if fwd else f1_ref)[pl.ds(off, c), :]
        log_f = jnp.log(lb + (1.0 - lb) * _sigmoid(f))
        k = (1.0 - lb) * _sigmoid(-f)
        q = _silu(q_ref[pl.ds(off, c), :])
        v = v_ref[pl.ds(off, c), :].astype(BF16)
        cum3 = _dot(ord_ref[d], jnp.concatenate(_split_bf16(log_f, 3), axis=1))
        cum = cum3[:, :HG_DK] + cum3[:, HG_DK:2 * HG_DK] + cum3[:, 2 * HG_DK:]
        bnd = _dot(sel_ref[d], jnp.concatenate(_split_bf16(cum, 2), axis=1))
        q_minus_k = q - k
        scores = jnp.where(eye, _dot_nt(q.astype(BF16), k.astype(BF16)), 0.0)
        for lvl in range(HG_LEVELS):
            rows = slice(lvl * c, (lvl + 1) * c)
            e = cum - (bnd[rows, :HG_DK] + bnd[rows, HG_DK:])
            y = ((k + isq_ref[d, lvl] * q_minus_k) * jnp.exp(e * sgn_ref[d, lvl])).astype(BF16)
            scores = scores + _dot_nt(y, y) * msk_ref[d, lvl]
        last = cum[c - 1:c, :] if fwd else cum[0:1, :]
        o = _dot(scores.astype(BF16), v) + _dot_nt((q * jnp.exp(cum)).astype(BF16), st.astype(BF16))
        st = st * jnp.exp(last) + _dot_tn(v, (k * jnp.exp(last - cum)).astype(BF16))
        return o, st

    def post(off):
        o = acc_f[pl.ds(off, c), :] + acc_b[pl.ds(off, c), :]
        y = o * lax.rsqrt(jnp.mean(o * o, axis=-1, keepdims=True) + EPS) * nw_ref[...]
        o_ref[pl.ds(off, c), :] = (y * _silu(g_ref[pl.ds(off, c), :])).astype(o_ref.dtype)

    _scan_both_ways(n_chunks, n_ctx_chunks, c, chunk, acc_f, acc_b, jnp.zeros((HG_DK, HG_DK), F32), post)


def _hgrn2(u, col0, mix_w, lower, norm_w, *, n_ctx):
    bsz, t, _ = u.shape
    heads = mix_w // HG_DK
    c0 = col0 // HG_DK
    tables = (_scan_order_tables(HG_CHUNK),) + _hg_tables()

    def spec(k):
        return pl.BlockSpec((None, t, HG_DK), lambda i, h, k=k: (i, 0, c0 + k * heads + h))

    def table(v):
        return pl.BlockSpec(v.shape, lambda i, h, nd=v.ndim: (0,) * nd)

    return pl.pallas_call(
        functools.partial(_hg_kernel, t_total=t, n_ctx=n_ctx),
        grid=(bsz, heads),
        in_specs=[spec(0), spec(1), spec(2), spec(3), spec(4),
                  pl.BlockSpec((2, HG_DK), lambda i, h: (0, h)),
                  pl.BlockSpec((1, HG_DK), lambda i, h: (0, h))] + [table(v) for v in tables],
        out_specs=pl.BlockSpec((None, t, HG_DK), lambda i, h: (i, 0, h)),
        out_shape=jax.ShapeDtypeStruct((bsz, t, mix_w), BF16),
        scratch_shapes=[pltpu.VMEM((t, HG_DK), F32)] * 2,
        compiler_params=_params("parallel", "arbitrary"), name="hgrn2",
    )(u, u, u, u, u, lower, norm_w.reshape(1, mix_w), *tables)


def _ada_kernel(c_ref, w_ref, b_ref, o_ref):
    o_ref[...] = _dot(_silu(c_ref[...]).astype(BF16), w_ref[...].astype(BF16)) + b_ref[...]


def _ada(cond, ada_w, ada_b, tn=1024):
    r, d = cond.shape
    depth, _, n = ada_w.shape
    return pl.pallas_call(
        _ada_kernel,
        grid=(depth, n // tn),
        in_specs=[pl.BlockSpec((r, d), lambda l, j: (0, 0)),
                  pl.BlockSpec((None, d, tn), lambda l, j: (l, 0, j)),
                  pl.BlockSpec((None, 1, tn), lambda l, j: (l, 0, j))],
        out_specs=pl.BlockSpec((None, r, tn), lambda l, j: (l, 0, j)),
        out_shape=jax.ShapeDtypeStruct((depth, r, n), F32),
        compiler_params=_params("parallel", "parallel"), name="ada",
    )(cond, ada_w, ada_b.reshape(depth, 1, n))


def _rope_tables(n_ctx, n_lat):
    r, col = jnp.meshgrid(jnp.arange(n_lat // GRID_W), jnp.arange(GRID_W), indexing='ij')
    n_freq = RET_DK // 4
    freqs = ROPE_BASE ** (-jnp.arange(n_freq, dtype=F32) / n_freq)
    ang = jnp.concatenate([r.reshape(-1, 1) * freqs, col.reshape(-1, 1) * freqs], axis=-1)
    cos, sin = jnp.cos(ang), jnp.sin(ang)
    cos2 = jnp.concatenate([jnp.ones((n_ctx, RET_DK), F32), jnp.concatenate([cos, cos], axis=-1)], axis=0)
    sin2 = jnp.concatenate([jnp.zeros((n_ctx, RET_DK), F32), jnp.concatenate([-sin, sin], axis=-1)], axis=0)
    return cos2, sin2


def kernel(x, c, ctx, c_ctx, ada_w, ada_b, norm1_w, norm2_w, w_in, ssd_conv_w, ssd_conv_b, ssd_a_log, ssd_dt_bias, ssd_d, ssd_norm_w, lru_conv_w, lru_conv_b, lru_wa, lru_ba, lru_wx, lru_bx, lru_lambda, hg_lb_logits, hg_norm_w, w_branch, w_out, ffn_wgu, ffn_w2, router_w, moe_wgu, moe_w2, final_norm_w):
    bsz, n_lat, d = x.shape
    n_ctx = ctx.shape[1]
    t = n_ctx + n_lat
    depth = w_in.shape[0]
    mix_w = d // 2
    ssd_heads = mix_w // SSD_HD
    xbc_w = mix_w + 2 * SSD_GROUPS * SSD_STATE
    widths = (4 * mix_w, mix_w + xbc_w, 2 * ssd_heads, 2 * mix_w, 5 * mix_w, N_BRANCH * d)
    starts = np.concatenate([[0], np.cumsum(widths)])

    cos2, sin2 = _rope_tables(n_ctx, n_lat)
    p_lb = jax.nn.softmax(hg_lb_logits.astype(F32), axis=1)
    lower_bounds = jnp.cumsum(p_lb, axis=1) - p_lb[:, :1]

    pad = (-(1 + bsz)) % SUBLANES
    cond = jnp.concatenate([c_ctx[None], c, jnp.zeros((pad, d), F32)], axis=0)
    mods = _ada(cond, ada_w, ada_b).reshape(depth, -1, N_MOD, d)

    h = jnp.concatenate([ctx.astype(x.dtype), x], axis=1)
    for layer in range(depth):
        m_ctx = jnp.broadcast_to(mods[layer, 0], (bsz, N_MOD, d))
        m_lat = mods[layer, 1:1 + bsz]
        mod_rows = jnp.stack([m_ctx, m_lat], axis=1)
        mod_cols = jnp.stack([m_ctx, m_lat], axis=2)

        xn = _norm_call(h, norm1_w[layer], mod_rows, shift=0, scale=1, n_ctx=n_ctx).reshape(bsz * t, d)
        u_ret, u_ssd, u_dt, u_lru, u_hg, u_gates = (
            _matmul(xn, w_in[layer, :, starts[i]:starts[i + 1]].astype(BF16)).reshape(bsz, t, widths[i])
            for i in range(len(widths)))

        b_ret = _retention(u_ret, 0, mix_w, cos2, sin2, n_ctx=n_ctx)
        xact = _dwconv(u_ssd, mix_w, xbc_w, ssd_conv_w[layer], ssd_conv_b[layer], n_ctx=n_ctx, act=True)
        b_ssd = _ssd(u_ssd, 0, xact, u_dt, ssd_dt_bias[layer], ssd_a_log[layer], ssd_d[layer],
                     ssd_norm_w[layer], mix_w, n_ctx=n_ctx)
        xc = _dwconv(u_lru, 0, mix_w, lru_conv_w[layer], lru_conv_b[layer], n_ctx=n_ctx, act=False)
        b_lru = _rglru(xc, u_lru, mix_w, lru_wa[layer], lru_ba[layer], lru_wx[layer], lru_bx[layer],
                       lru_lambda[layer], mix_w, n_ctx=n_ctx)
        b_hg = _hgrn2(u_hg, 0, mix_w, lower_bounds[:, layer], hg_norm_w[layer], n_ctx=n_ctx)

        branches = [v.reshape(bsz * t, mix_w) for v in (b_ret, b_ssd, b_lru, b_hg)]
        mixed = _merge(branches, u_gates.reshape(bsz * t, N_BRANCH * d), w_branch[layer].astype(BF16), tm=768)
        h2 = _matmul_residual(mixed, w_out[layer].astype(BF16), h.reshape(bsz * t, d), mod_cols, 2,
                              t_total=t, n_ctx=n_ctx)

        if layer % 2 == 0:
            xn = _norm_call(h2.reshape(bsz, t, d), norm2_w[layer], mod_rows, shift=3, scale=4, n_ctx=n_ctx)
            act = _swiglu_up(xn.reshape(bsz * t, d), ffn_wgu[layer // 2].astype(BF16))
            w2 = ffn_w2[layer // 2].astype(BF16)
        else:
            xn, gates = _norm_call(h2.reshape(bsz, t, d), norm2_w[layer], mod_rows, shift=3, scale=4, n_ctx=n_ctx,
                                   router_w=router_w[layer // 2])
            act = _moe_up(xn.reshape(bsz * t, d), moe_wgu[layer // 2].astype(BF16), gates.reshape(bsz * t, LANES))
            w2 = moe_w2[layer // 2].astype(BF16).reshape(-1, d)
        h = _matmul_residual(act, w2, h2, mod_cols, 5, t_total=t, n_ctx=n_ctx).reshape(bsz, t, d)
    return _final_norm_call(h, final_norm_w, n_ctx)
```

```python
import functools
import math

import numpy as np
import jax
import jax.numpy as jnp
from jax import lax
from jax.experimental import pallas as pl
from jax.experimental.pallas import tpu as pltpu

F32 = jnp.float32
BF16 = jnp.bfloat16
HI = lax.Precision.HIGHEST

EPS = 1e-6
GRID_W = 64
ROPE_BASE = 10000.0
RET_DK = 128
SSD_HD = 64
SSD_GROUPS = 4
SSD_STATE = 128
CONV_W = 4
LRU_BLOCKS = 8
LRU_C = 8.0
HG_DK = 128
N_BRANCH = 4
N_EXPERTS = 8
N_MOD = 6

LANES = 128
SUBLANES = 8
VMEM_LIMIT = 52 * 1024 * 1024

SCAN_CHUNK = 256
HG_CHUNK = 128
HG_LEVELS = 7


def _params(*sem):
    return pltpu.CompilerParams(dimension_semantics=sem, vmem_limit_bytes=VMEM_LIMIT)


def _dot(a, b, prec=None):
    return jnp.dot(a, b, preferred_element_type=F32, precision=prec)


def _dot_nt(a, b, prec=None):
    return lax.dot_general(a, b, (((1,), (1,)), ((), ())), preferred_element_type=F32, precision=prec)


def _dot_tn(a, b, prec=None):
    return lax.dot_general(a, b, (((0,), (0,)), ((), ())), preferred_element_type=F32, precision=prec)


def _sigmoid(x):
    return 1.0 / (1.0 + jnp.exp(-x))


def _silu(x):
    return x * _sigmoid(x)


def _softplus(x):
    return jnp.maximum(x, 0.0) + jnp.log1p(jnp.exp(-jnp.abs(x)))


def _gelu_tanh(x):
    return 0.5 * x * (1.0 + jnp.tanh(math.sqrt(2.0 / math.pi) * (x + 0.044715 * (x * x * x))))


def _bwd_chunk(s, n_chunks, n_ctx_chunks):
    return jnp.where(s < n_ctx_chunks, n_ctx_chunks - 1 - s, n_chunks - 1 - (s - n_ctx_chunks))


def _scan_both_ways(n_chunks, n_ctx_chunks, c, chunk_fn, acc_f, acc_b, zero_state, post_fn):
    def body(s, carry):
        st_f, st_b = carry
        off_f = pl.multiple_of(s * c, c)
        off_b = pl.multiple_of(_bwd_chunk(s, n_chunks, n_ctx_chunks) * c, c)
        o_f, st_f = chunk_fn(off_f, st_f, 0)
        o_b, st_b = chunk_fn(off_b, st_b, 1)
        acc_f[pl.ds(off_f, c), :] = o_f
        acc_b[pl.ds(off_b, c), :] = o_b
        return st_f, st_b

    lax.fori_loop(0, n_chunks, body, (zero_state, zero_state))

    def finish(s, _):
        post_fn(pl.multiple_of(s * c, c))
        return 0

    lax.fori_loop(0, n_chunks, finish, 0)


def _scan_order_tables(c):
    idx = np.arange(c)
    lower = (idx[None, :] <= idx[:, None]).astype(np.float32)
    return jnp.asarray(np.stack([lower, lower.T]), BF16)


def _split_bf16(x, parts):
    out = []
    for _ in range(parts):
        p = x.astype(BF16)
        out.append(p)
        x = x - p.astype(F32)
    return out


def _norm_kernel(h_ref, w_ref, mod_ref, o_ref, *, shift, scale):
    x = h_ref[...]
    y = x * lax.rsqrt(jnp.mean(x * x, axis=-1, keepdims=True) + EPS) * w_ref[...]
    o_ref[...] = (y * (1.0 + mod_ref[scale:scale + 1, :]) + mod_ref[shift:shift + 1, :]).astype(o_ref.dtype)


def _norm_router_kernel(h_ref, w_ref, mod_ref, rw_ref, o_ref, gate_ref, *, shift, scale):
    x = h_ref[...]
    y = x * lax.rsqrt(jnp.mean(x * x, axis=-1, keepdims=True) + EPS) * w_ref[...]
    xn = y * (1.0 + mod_ref[scale:scale + 1, :]) + mod_ref[shift:shift + 1, :]
    o_ref[...] = xn.astype(o_ref.dtype)
    logits = _dot(xn.astype(BF16), rw_ref[...])
    lane = lax.broadcasted_iota(jnp.int32, logits.shape, 1)
    neg = jnp.float32(-jnp.inf)
    logits = jnp.where(lane < N_EXPERTS, logits, neg)
    v1 = jnp.max(logits, axis=-1, keepdims=True)
    i1 = jnp.min(jnp.where(logits == v1, lane, LANES), axis=-1, keepdims=True)
    rest = jnp.where(lane == i1, neg, logits)
    v2 = jnp.max(rest, axis=-1, keepdims=True)
    i2 = jnp.min(jnp.where(rest == v2, lane, LANES), axis=-1, keepdims=True)
    e2 = jnp.exp(v2 - v1)
    w1 = 1.0 / (1.0 + e2)
    w2 = e2 / (1.0 + e2)
    gate_ref[...] = (jnp.where(lane == i1, w1, 0.0) + jnp.where(lane == i2, w2, 0.0)
                     + jnp.where(lane == N_EXPERTS, i1.astype(F32), 0.0)
                     + jnp.where(lane == N_EXPERTS + 1, i2.astype(F32), 0.0))


def _norm_call(h, w, mod, *, shift, scale, n_ctx, router_w=None, out_dtype=BF16):
    b, t, d = h.shape
    tr = n_ctx
    grid = (b, t // tr)
    h_spec = pl.BlockSpec((None, tr, d), lambda i, j: (i, j, 0))
    w_spec = pl.BlockSpec((1, d), lambda i, j: (0, 0))
    mod_spec = pl.BlockSpec((None, None, N_MOD, d), lambda i, j: (i, jnp.minimum(j, 1), 0, 0))
    if router_w is None:
        return pl.pallas_call(
            functools.partial(_norm_kernel, shift=shift, scale=scale),
            grid=grid, in_specs=[h_spec, w_spec, mod_spec], out_specs=h_spec,
            out_shape=jax.ShapeDtypeStruct(h.shape, out_dtype),
            compiler_params=_params("parallel", "parallel"), name="norm_mod",
        )(h, w.reshape(1, d), mod)
    rw = jnp.zeros((d, LANES), BF16).at[:, :N_EXPERTS].set(router_w.astype(BF16))
    return pl.pallas_call(
        functools.partial(_norm_router_kernel, shift=shift, scale=scale),
        grid=grid,
        in_specs=[h_spec, w_spec, mod_spec, pl.BlockSpec((d, LANES), lambda i, j: (0, 0))],
        out_specs=[h_spec, pl.BlockSpec((None, tr, LANES), lambda i, j: (i, j, 0))],
        out_shape=[jax.ShapeDtypeStruct(h.shape, out_dtype), jax.ShapeDtypeStruct((b, t, LANES), F32)],
        compiler_params=_params("parallel", "parallel"), name="norm_mod_router",
    )(h, w.reshape(1, d), mod, rw)


def _final_norm_kernel(h_ref, w_ref, o_ref):
    x = h_ref[...]
    o_ref[...] = x * lax.rsqrt(jnp.mean(x * x, axis=-1, keepdims=True) + EPS) * w_ref[...]


def _final_norm_call(h, w, n_ctx):
    b, t, d = h.shape
    tr = n_ctx
    return pl.pallas_call(
        _final_norm_kernel,
        grid=(b, (t - n_ctx) // tr),
        in_specs=[pl.BlockSpec((None, tr, d), lambda i, j: (i, j + 1, 0)), pl.BlockSpec((1, d), lambda i, j: (0, 0))],
        out_specs=pl.BlockSpec((None, tr, d), lambda i, j: (i, j, 0)),
        out_shape=jax.ShapeDtypeStruct((b, t - n_ctx, d), F32),
        compiler_params=_params("parallel", "parallel"), name="final_norm",
    )(h, w.reshape(1, d))


def _pick(n, prefs):
    for p in prefs:
        if n % p == 0:
            return p
    return n


def _mm_kernel(a_ref, w_ref, o_ref, acc_ref, *, nk):
    k = pl.program_id(2)
    part = _dot(a_ref[...], w_ref[...])
    if nk == 1:
        o_ref[...] = part.astype(o_ref.dtype)
        return

    @pl.when(k == 0)
    def _():
        acc_ref[...] = part

    @pl.when(k > 0)
    def _():
        acc_ref[...] += part

    @pl.when(k == nk - 1)
    def _():
        o_ref[...] = acc_ref[...].astype(o_ref.dtype)


def _matmul(a, w, out_dtype=F32, tm=None, tn=None, tk=None):
    m, kd = a.shape
    n = w.shape[1]
    tm = tm or _pick(m, (1152, 1024, 768, 512, 256))
    tn = tn or _pick(n, (1024, 512, 256, 128))
    tk = tk or (kd if kd <= 2048 else _pick(kd, (512, 256, 128)))
    nk = kd // tk
    return pl.pallas_call(
        functools.partial(_mm_kernel, nk=nk),
        grid=(n // tn, m // tm, nk),
        in_specs=[pl.BlockSpec((tm, tk), lambda j, i, k: (i, k)), pl.BlockSpec((tk, tn), lambda j, i, k: (k, j))],
        out_specs=pl.BlockSpec((tm, tn), lambda j, i, k: (i, j)),
        out_shape=jax.ShapeDtypeStruct((m, n), out_dtype),
        scratch_shapes=[pltpu.VMEM((tm, tn), F32)],
        compiler_params=_params("parallel", "parallel", "arbitrary"), name="matmul",
    )(a, w)


def _row_is_ctx(tm, t_total, n_ctx):
    row0 = pl.program_id(1) * tm
    pos = (row0 + lax.broadcasted_iota(jnp.int32, (tm, 1), 0)) % t_total
    return pos < n_ctx


def _mm_res_kernel(a_ref, w_ref, res_ref, mod_ref, o_ref, acc_ref, *, nk, tm, t_total, n_ctx):
    k = pl.program_id(2)
    part = _dot(a_ref[...], w_ref[...])

    @pl.when(k == 0)
    def _():
        acc_ref[...] = part

    @pl.when(k > 0)
    def _():
        acc_ref[...] += part

    @pl.when(k == nk - 1)
    def _():
        mod = jnp.where(_row_is_ctx(tm, t_total, n_ctx), mod_ref[0:1, :], mod_ref[1:2, :])
        o_ref[...] = res_ref[...] + mod * acc_ref[...]


def _matmul_residual(a, w, res, mod, mod_idx, *, t_total, n_ctx, tm=None, tn=None, tk=None):
    m, kd = a.shape
    n = w.shape[1]
    tm = tm or _pick(t_total, (1152, 768, 256))
    tn = tn or _pick(n, (1024, 512, 256, 128))
    tk = tk or (kd if kd <= 2048 else _pick(kd, (512, 256, 128)))
    nk = kd // tk
    per_b = t_total // tm
    return pl.pallas_call(
        functools.partial(_mm_res_kernel, nk=nk, tm=tm, t_total=t_total, n_ctx=n_ctx),
        grid=(n // tn, m // tm, nk),
        in_specs=[pl.BlockSpec((tm, tk), lambda j, i, k: (i, k)),
                  pl.BlockSpec((tk, tn), lambda j, i, k: (k, j)),
                  pl.BlockSpec((tm, tn), lambda j, i, k: (i, j)),
                  pl.BlockSpec((None, None, 2, tn), lambda j, i, k: (i // per_b, mod_idx, 0, j))],
        out_specs=pl.BlockSpec((tm, tn), lambda j, i, k: (i, j)),
        out_shape=jax.ShapeDtypeStruct((m, n), F32),
        scratch_shapes=[pltpu.VMEM((tm, tn), F32)],
        compiler_params=_params("parallel", "parallel", "arbitrary"), name="matmul_residual",
    )(a, w, res, mod)


def _swiglu_kernel(a_ref, wg_ref, wu_ref, o_ref):
    a = a_ref[...]
    g = _dot(a, wg_ref[...])
    u = _dot(a, wu_ref[...])
    o_ref[...] = (_silu(g) * u).astype(o_ref.dtype)


def _swiglu_up(a, wgu, tm=None, tn=None):
    m, kd = a.shape
    f = wgu.shape[1] // 2
    tm = tm or _pick(m, (1152, 1024, 768, 512, 256))
    tn = tn or _pick(f, (512, 256, 128))
    nf = f // tn
    return pl.pallas_call(
        _swiglu_kernel,
        grid=(nf, m // tm),
        in_specs=[pl.BlockSpec((tm, kd), lambda j, i: (i, 0)),
                  pl.BlockSpec((kd, tn), lambda j, i: (0, j)),
                  pl.BlockSpec((kd, tn), lambda j, i: (0, j + nf))],
        out_specs=pl.BlockSpec((tm, tn), lambda j, i: (i, j)),
        out_shape=jax.ShapeDtypeStruct((m, f), BF16),
        compiler_params=_params("parallel", "parallel"), name="swiglu_up",
    )(a, wgu, wgu)


MOE_TM = 512
GATHER_ROWS = 256


def _route(gates, n_tok):
    tm = MOE_TM
    n_slot = 2 * n_tok
    n_tiles = n_slot // tm + N_EXPERTS
    eid = gates[:, N_EXPERTS:N_EXPERTS + 2].astype(jnp.int32).reshape(n_slot)
    order = jnp.argsort(eid, stable=True)
    eid_sorted = eid[order]
    counts = jnp.zeros((N_EXPERTS,), jnp.int32).at[eid].add(1)
    padded = (counts + tm - 1) // tm * tm
    ends = jnp.cumsum(padded)
    starts = ends - padded
    first_sorted = jnp.cumsum(counts) - counts
    dest_sorted = starts[eid_sorted] + jnp.arange(n_slot, dtype=jnp.int32) - first_sorted[eid_sorted]
    row_src = jnp.zeros((n_tiles * tm,), jnp.int32).at[dest_sorted].set(order // 2)
    dest = jnp.zeros((n_slot,), jnp.int32).at[order].set(dest_sorted).reshape(n_tok, 2)
    tile_start = jnp.arange(n_tiles, dtype=jnp.int32) * tm
    tile_expert = jnp.minimum(jnp.searchsorted(ends, tile_start, side='right'), N_EXPERTS - 1).astype(jnp.int32)
    tile_first = jnp.concatenate([jnp.ones((1,), jnp.int32),
                                  (tile_expert[1:] != tile_expert[:-1]).astype(jnp.int32)])
    return row_src, dest, tile_expert, tile_first, (ends[-1:] // tm).astype(jnp.int32)


def _gather_kernel(idx_ref, src_ref, o_ref, sem, *, rows):
    base = pl.program_id(0) * rows

    def row_copy(r, src_row):
        return pltpu.make_async_copy(src_ref.at[pl.ds(src_row, 1), :], o_ref.at[pl.ds(r, 1), :], sem)

    def issue(r, carry):
        row_copy(r, idx_ref[base + r]).start()
        return carry

    def drain(r, carry):
        row_copy(r, 0).wait()
        return carry

    lax.fori_loop(0, rows, issue, 0)
    lax.fori_loop(0, rows, drain, 0)


def _gather_rows(src, idx):
    p, w = idx.shape[0], src.shape[1]
    rows = GATHER_ROWS
    return pl.pallas_call(
        functools.partial(_gather_kernel, rows=rows),
        grid_spec=pltpu.PrefetchScalarGridSpec(
            num_scalar_prefetch=1, grid=(p // rows,),
            in_specs=[pl.BlockSpec(memory_space=pl.ANY)],
            out_specs=pl.BlockSpec((rows, w), lambda i, idx: (i, 0)),
            scratch_shapes=[pltpu.SemaphoreType.DMA(())]),
        out_shape=jax.ShapeDtypeStruct((p, w), src.dtype),
        compiler_params=_params("arbitrary"), name="gather_rows",
    )(idx, src)


def _moe_up_kernel(te_ref, tf_ref, nu_ref, a_ref, wg_ref, wu_ref, o_ref, wg_bf, wu_bf):
    i = pl.program_id(1)

    @pl.when(tf_ref[i] == 1)
    def _():
        wg_bf[...] = wg_ref[...].astype(BF16)
        wu_bf[...] = wu_ref[...].astype(BF16)

    @pl.when(i < nu_ref[0])
    def _():
        a = a_ref[...]
        o_ref[...] = (_silu(_dot(a, wg_bf[...])) * _dot(a, wu_bf[...])).astype(o_ref.dtype)

    @pl.when(i >= nu_ref[0])
    def _():
        o_ref[...] = jnp.zeros_like(o_ref)


def _moe_up(xg, wgu, tile_expert, tile_first, n_used, tn=512):
    p, kd = xg.shape
    f = wgu.shape[2] // 2
    nf = f // tn
    tm = MOE_TM
    return pl.pallas_call(
        _moe_up_kernel,
        grid_spec=pltpu.PrefetchScalarGridSpec(
            num_scalar_prefetch=3, grid=(nf, p // tm),
            in_specs=[pl.BlockSpec((tm, kd), lambda j, i, te, tf, nu: (i, 0)),
                      pl.BlockSpec((None, kd, tn), lambda j, i, te, tf, nu: (te[i], 0, j)),
                      pl.BlockSpec((None, kd, tn), lambda j, i, te, tf, nu: (te[i], 0, j + nf))],
            out_specs=pl.BlockSpec((tm, tn), lambda j, i, te, tf, nu: (i, j)),
            scratch_shapes=[pltpu.VMEM((kd, tn), BF16)] * 2),
        out_shape=jax.ShapeDtypeStruct((p, f), BF16),
        compiler_params=_params("arbitrary", "arbitrary"), name="moe_up",
    )(tile_expert, tile_first, n_used, xg, wgu, wgu)


def _moe_down_kernel(te_ref, tf_ref, nu_ref, a_ref, w_ref, o_ref, w_bf):
    i = pl.program_id(1)

    @pl.when(tf_ref[i] == 1)
    def _():
        w_bf[...] = w_ref[...].astype(BF16)

    @pl.when(i < nu_ref[0])
    def _():
        o_ref[...] = _dot(a_ref[...], w_bf[...])

    @pl.when(i >= nu_ref[0])
    def _():
        o_ref[...] = jnp.zeros_like(o_ref)


def _moe_down(act, w2, tile_expert, tile_first, n_used, tn=512):
    p, f = act.shape
    d = w2.shape[2]
    tm = MOE_TM
    return pl.pallas_call(
        _moe_down_kernel,
        grid_spec=pltpu.PrefetchScalarGridSpec(
            num_scalar_prefetch=3, grid=(d // tn, p // tm),
            in_specs=[pl.BlockSpec((tm, f), lambda j, i, te, tf, nu: (i, 0)),
                      pl.BlockSpec((None, f, tn), lambda j, i, te, tf, nu: (te[i], 0, j))],
            out_specs=pl.BlockSpec((tm, tn), lambda j, i, te, tf, nu: (i, j)),
            scratch_shapes=[pltpu.VMEM((f, tn), BF16)]),
        out_shape=jax.ShapeDtypeStruct((p, d), F32),
        compiler_params=_params("arbitrary", "arbitrary"), name="moe_down",
    )(tile_expert, tile_first, n_used, act, w2)


def _moe_combine_kernel(y0_ref, y1_ref, gate_ref, res_ref, mod_ref, o_ref, *, mod_idx):
    gates = gate_ref[...]
    lane = lax.broadcasted_iota(jnp.int32, gates.shape, 1)
    i0 = gates[:, N_EXPERTS:N_EXPERTS + 1].astype(jnp.int32)
    i1 = gates[:, N_EXPERTS + 1:N_EXPERTS + 2].astype(jnp.int32)
    w0 = jnp.sum(jnp.where(lane == i0, gates, 0.0), axis=-1, keepdims=True)
    w1 = jnp.sum(jnp.where(lane == i1, gates, 0.0), axis=-1, keepdims=True)
    o_ref[...] = res_ref[...] + mod_ref[mod_idx:mod_idx + 1, :] * (w0 * y0_ref[...] + w1 * y1_ref[...])


def _moe_combine(y0, y1, gates, res, mod, mod_idx, *, n_ctx):
    b, t, d = res.shape
    tr = n_ctx
    spec = pl.BlockSpec((None, tr, d), lambda i, j: (i, j, 0))
    return pl.pallas_call(
        functools.partial(_moe_combine_kernel, mod_idx=mod_idx),
        grid=(b, t // tr),
        in_specs=[spec, spec, pl.BlockSpec((None, tr, LANES), lambda i, j: (i, j, 0)), spec,
                  pl.BlockSpec((None, None, N_MOD, d), lambda i, j: (i, jnp.minimum(j, 1), 0, 0))],
        out_specs=spec,
        out_shape=jax.ShapeDtypeStruct((b, t, d), F32),
        compiler_params=_params("parallel", "parallel"), name="moe_combine",
    )(y0, y1, gates, res, mod)


def _moe_block(xn, gates, res, wgu, w2, mod, mod_idx, *, n_ctx):
    b, t, d = xn.shape
    m = b * t
    row_src, dest, tile_expert, tile_first, n_used = _route(gates.reshape(m, LANES), m)
    xn_words = lax.bitcast_convert_type(xn.reshape(m, d // 2, 2), jnp.uint32)
    xg = lax.bitcast_convert_type(_gather_rows(xn_words, row_src), BF16).reshape(-1, d)
    act = _moe_up(xg, wgu, tile_expert, tile_first, n_used)
    yg = _moe_down(act, w2, tile_expert, tile_first, n_used)
    y0 = _gather_rows(yg, dest[:, 0]).reshape(b, t, d)
    y1 = _gather_rows(yg, dest[:, 1]).reshape(b, t, d)
    return _moe_combine(y0, y1, gates, res, mod, mod_idx, n_ctx=n_ctx)


def _merge_kernel(b0, b1, b2, b3, w_ref, g0, g1, g2, g3, o_ref):
    acc = None
    for n, (br, g) in enumerate(((b0, g0), (b1, g1), (b2, g2), (b3, g3))):
        term = _sigmoid(g[...]) * _dot(br[...], w_ref[n])
        acc = term if acc is None else acc + term
    o_ref[...] = acc.astype(o_ref.dtype)


def _merge(branches, gates, wb, tm=None, tn=None):
    m, kd = branches[0].shape
    d = wb.shape[-1]
    tm = tm or _pick(m, (1152, 1024, 768, 512, 256))
    tn = tn or _pick(d, (512, 256, 128))
    nd = d // tn
    br_spec = pl.BlockSpec((tm, kd), lambda j, i: (i, 0))
    gate_specs = [pl.BlockSpec((tm, tn), functools.partial(lambda j, i, n: (i, j + n * nd), n=n))
                  for n in range(N_BRANCH)]
    return pl.pallas_call(
        _merge_kernel,
        grid=(nd, m // tm),
        in_specs=[br_spec] * N_BRANCH + [pl.BlockSpec((N_BRANCH, kd, tn), lambda j, i: (0, 0, j))] + gate_specs,
        out_specs=pl.BlockSpec((tm, tn), lambda j, i: (i, j)),
        out_shape=jax.ShapeDtypeStruct((m, d), BF16),
        compiler_params=_params("parallel", "parallel"), name="merge",
    )(*branches, wb, gates, gates, gates, gates)


def _conv_kernel(u_ref, w_ref, b_ref, o_ref, *, t_total, n_ctx, act):
    x = u_ref[...]
    pos = lax.broadcasted_iota(jnp.int32, (t_total, 1), 0)
    seg_lo = jnp.where(pos < n_ctx, 0, n_ctx)
    seg_hi = jnp.where(pos < n_ctx, n_ctx, t_total)
    left = (CONV_W - 1) // 2
    acc = None
    for j in range(CONV_W):
        d = j - left
        xs = x if d == 0 else pltpu.roll(x, (-d) % t_total, 0)
        ok = (pos + d >= seg_lo) & (pos + d < seg_hi)
        term = jnp.where(ok, xs, 0.0) * w_ref[j:j + 1, :]
        acc = term if acc is None else acc + term
    acc = acc + b_ref[...]
    o_ref[...] = _silu(acc) if act else acc


def _dwconv(u, col0, width, w, b, *, n_ctx, act, wt=256):
    bsz, t, _ = u.shape
    c0 = col0 // wt
    return pl.pallas_call(
        functools.partial(_conv_kernel, t_total=t, n_ctx=n_ctx, act=act),
        grid=(bsz, width // wt),
        in_specs=[pl.BlockSpec((None, t, wt), lambda i, j: (i, 0, j + c0)),
                  pl.BlockSpec((CONV_W, wt), lambda i, j: (0, j)),
                  pl.BlockSpec((1, wt), lambda i, j: (0, j))],
        out_specs=pl.BlockSpec((None, t, wt), lambda i, j: (i, 0, j)),
        out_shape=jax.ShapeDtypeStruct((bsz, t, width), F32),
        compiler_params=_params("parallel", "parallel"), name="dwconv",
    )(u, w, b.reshape(1, width))


def _ret_kernel(lg_ref, q_ref, k_ref, v_ref, g_ref, cos_ref, sin_ref, o_ref, qs, ks, acc_f, acc_b,
                *, t_total, n_ctx):
    c = SCAN_CHUNK
    n_chunks, n_ctx_chunks = t_total // c, n_ctx // c
    lg = lg_ref[pl.program_id(1)]
    cos, sin = cos_ref[...], sin_ref[...]
    q, k = q_ref[...], k_ref[...]
    qs[...] = q * cos + pltpu.roll(q, RET_DK // 2, 1) * sin
    ks[...] = (k * cos + pltpu.roll(k, RET_DK // 2, 1) * sin) * (RET_DK ** -0.5)

    ii = lax.broadcasted_iota(jnp.int32, (c, c), 0)
    jj = lax.broadcasted_iota(jnp.int32, (c, c), 1)
    dist = (ii - jj).astype(F32)
    r = lax.broadcasted_iota(jnp.int32, (c, 1), 0).astype(F32)
    decay_all = jnp.exp(c * lg)

    def chunk(off, s_in, fwd):
        qc = qs[pl.ds(off, c), :]
        kc = ks[pl.ds(off, c), :]
        vc = v_ref[pl.ds(off, c), :].astype(BF16)
        if fwd:
            mask, steps, q_pow, k_pow = ii >= jj, dist, r + 1.0, c - 1.0 - r
        else:
            mask, steps, q_pow, k_pow = jj >= ii, -dist, c - r, r
        decay = jnp.where(mask, jnp.exp(jnp.where(mask, steps, 0.0) * lg), 0.0)
        scores = _dot_nt(qc.astype(BF16), kc.astype(BF16)) * decay
        o = _dot(scores.astype(BF16), vc) + jnp.exp(q_pow * lg) * _dot(qc.astype(BF16), s_in.astype(BF16))
        s_out = decay_all * s_in + _dot_tn((kc * jnp.exp(k_pow * lg)).astype(BF16), vc)
        return o, s_out

    def post(off):
        o = acc_f[pl.ds(off, c), :] + acc_b[pl.ds(off, c), :]
        y = o * lax.rsqrt(jnp.mean(o * o, axis=-1, keepdims=True) + EPS)
        o_ref[pl.ds(off, c), :] = (y * _silu(g_ref[pl.ds(off, c), :])).astype(o_ref.dtype)

    _scan_both_ways(n_chunks, n_ctx_chunks, c, lambda off, st, d: chunk(off, st, d == 0), acc_f, acc_b,
                    jnp.zeros((RET_DK, RET_DK), F32), post)


def _retention(u, col0, mix_w, cos2, sin2, *, n_ctx):
    bsz, t, _ = u.shape
    heads = mix_w // RET_DK
    c0 = col0 // RET_DK
    log_gamma = jnp.log1p(-jnp.exp2(-5.0 - jnp.arange(heads, dtype=F32)))

    def spec(k):
        return pl.BlockSpec((None, t, RET_DK), lambda i, h, lg, k=k: (i, 0, c0 + k * heads + h))

    table = pl.BlockSpec((t, RET_DK), lambda i, h, lg: (0, 0))
    return pl.pallas_call(
        functools.partial(_ret_kernel, t_total=t, n_ctx=n_ctx),
        grid_spec=pltpu.PrefetchScalarGridSpec(
            num_scalar_prefetch=1, grid=(bsz, heads),
            in_specs=[spec(0), spec(1), spec(2), spec(3), table, table],
            out_specs=pl.BlockSpec((None, t, RET_DK), lambda i, h, lg: (i, 0, h)),
            scratch_shapes=[pltpu.VMEM((t, RET_DK), F32)] * 4),
        out_shape=jax.ShapeDtypeStruct((bsz, t, mix_w), BF16),
        compiler_params=_params("parallel", "arbitrary"), name="retention",
    )(log_gamma, u, u, u, u, cos2, sin2)


def _ssd_kernel(xs_ref, bm_ref, cm_ref, z_ref, dt_ref, dtb_ref, alog_ref, dsk_ref, nw_ref, sel_ref, ord_ref, o_ref,
                acc_f, acc_b, dts, las, *, t_total, n_ctx):
    c = SCAN_CHUNK
    rep = sel_ref.shape[1] // 2
    gw = rep * SSD_HD
    n_chunks, n_ctx_chunks = t_total // c, n_ctx // c
    dt_all = _softplus(dt_ref[...] + dtb_ref[...])
    dts[...] = dt_all
    las[...] = dt_all * (-jnp.exp(alog_ref[...]))

    ii = lax.broadcasted_iota(jnp.int32, (c, c), 0)
    jj = lax.broadcasted_iota(jnp.int32, (c, c), 1)
    lane_head = lax.broadcasted_iota(jnp.int32, (1, gw), 1) // SSD_HD

    def chunk(off, s_in, d):
        fwd = d == 0
        mask = (ii >= jj) if fwd else (jj >= ii)
        la_parts = _split_bf16(las[pl.ds(off, c), :], 3)
        cum = sum(_dot(ord_ref[d], p) for p in la_parts)
        cum_rows = sum(_dot_tn(p, ord_ref[1 - d]) for p in la_parts)
        sel = sel_ref[d]
        cum_e = sum(_dot(p, sel) for p in _split_bf16(cum, 3))
        dt_e = sum(_dot(p, sel) for p in _split_bf16(dts[pl.ds(off, c), :], 3))
        last_e = cum_e[c - 1:c, :] if fwd else cum_e[0:1, :]
        xs = xs_ref[pl.ds(off, c), :]
        vdt = xs * dt_e
        cm = cm_ref[pl.ds(off, c), :].astype(BF16)
        bm = bm_ref[pl.ds(off, c), :].astype(BF16)
        qk = _dot_nt(cm, bm)
        o = jnp.exp(cum_e) * _dot(cm, s_in.astype(BF16))
        for i in range(rep):
            col = d * rep + i
            diff = cum[:, col:col + 1] - cum_rows[col:col + 1, :]
            decay = jnp.where(mask, jnp.exp(jnp.where(mask, diff, 0.0)), 0.0)
            o = o + _dot((qk * decay).astype(BF16), jnp.where(lane_head == i, vdt, 0.0).astype(BF16))
        s_out = jnp.exp(last_e) * s_in + _dot_tn(bm, (vdt * jnp.exp(last_e - cum_e)).astype(BF16))
        return o, s_out

    def post(off):
        o = acc_f[pl.ds(off, c), :] + acc_b[pl.ds(off, c), :]
        y = (o + dsk_ref[...] * xs_ref[pl.ds(off, c), :]) * _silu(z_ref[pl.ds(off, c), :])
        y = y * lax.rsqrt(jnp.mean(y * y, axis=-1, keepdims=True) + EPS)
        o_ref[pl.ds(off, c), :] = (y * nw_ref[...]).astype(o_ref.dtype)

    _scan_both_ways(n_chunks, n_ctx_chunks, c, chunk, acc_f, acc_b, jnp.zeros((SSD_STATE, gw), F32), post)


def _ssd(u_z, z_col0, xact, u_dt, dt_bias, a_log, d_skip, norm_w, mix_w, *, n_ctx):
    bsz, t, _ = xact.shape
    heads = mix_w // SSD_HD
    rep = heads // SSD_GROUPS
    gw = rep * SSD_HD
    def per_group(v):
        lead = v.shape[:-1]
        v = v.reshape(*lead, 2, SSD_GROUPS, rep)
        return jnp.moveaxis(v, -2, 0).reshape(SSD_GROUPS, *lead, 2 * rep)
    dt_g = jnp.moveaxis(per_group(u_dt), 0, 1)
    dtb_g = per_group(dt_bias.reshape(1, 2 * heads))
    alog_g = per_group(a_log.reshape(1, 2 * heads))
    sel = np.zeros((2, 2 * rep, gw), np.float32)
    for d in range(2):
        for i in range(rep):
            sel[d, d * rep + i, i * SSD_HD:(i + 1) * SSD_HD] = 1.0
    dsk = jnp.repeat(d_skip, SSD_HD).reshape(1, mix_w)
    zc, bc, cc = z_col0 // gw, mix_w // SSD_STATE, mix_w // SSD_STATE + SSD_GROUPS
    small = lambda: pl.BlockSpec((None, 1, 2 * rep), lambda i, g: (g, 0, 0))
    return pl.pallas_call(
        functools.partial(_ssd_kernel, t_total=t, n_ctx=n_ctx),
        grid=(bsz, SSD_GROUPS),
        in_specs=[pl.BlockSpec((None, t, gw), lambda i, g: (i, 0, g)),
                  pl.BlockSpec((None, t, SSD_STATE), lambda i, g: (i, 0, bc + g)),
                  pl.BlockSpec((None, t, SSD_STATE), lambda i, g: (i, 0, cc + g)),
                  pl.BlockSpec((None, t, gw), lambda i, g: (i, 0, zc + g)),
                  pl.BlockSpec((None, None, t, 2 * rep), lambda i, g: (i, g, 0, 0)),
                  small(), small(),
                  pl.BlockSpec((1, gw), lambda i, g: (0, g)),
                  pl.BlockSpec((1, gw), lambda i, g: (0, g)),
                  pl.BlockSpec((2, 2 * rep, gw), lambda i, g: (0, 0, 0)),
                  pl.BlockSpec((2, SCAN_CHUNK, SCAN_CHUNK), lambda i, g: (0, 0, 0))],
        out_specs=pl.BlockSpec((None, t, gw), lambda i, g: (i, 0, g)),
        out_shape=jax.ShapeDtypeStruct((bsz, t, mix_w), BF16),
        scratch_shapes=[pltpu.VMEM((t, gw), F32), pltpu.VMEM((t, gw), F32),
                        pltpu.VMEM((t, 2 * rep), F32), pltpu.VMEM((t, 2 * rep), F32)],
        compiler_params=_params("parallel", "arbitrary"), name="ssd",
    )(xact, xact, xact, u_z, dt_g, dtb_g, alog_g, dsk, norm_w.reshape(1, mix_w), jnp.asarray(sel, BF16),
      _scan_order_tables(SCAN_CHUNK))


def _tile_scan(a, b, row, reverse):
    for s in (1, 2, 4):
        if reverse:
            ok, shift = row < SUBLANES - s, SUBLANES - s
        else:
            ok, shift = row >= s, s
        b = b + a * jnp.where(ok, pltpu.roll(b, shift, 0), 0.0)
        a = a * jnp.where(ok, pltpu.roll(a, shift, 0), 1.0)
    return a, b


def _lru_kernel(xc_ref, y_ref, w_ref, bias_ref, lam_ref, o_ref, acc, a_f, b_f, a_b, b_b, *, t_total, n_ctx):
    c = SCAN_CHUNK
    nb, bw = w_ref.shape[0], w_ref.shape[1]
    n_chunks, n_ctx_chunks = t_total // c, n_ctx // c
    n_tiles = c // SUBLANES
    log_sig = -_softplus(-lam_ref[...])
    acc[...] = jnp.zeros_like(acc)
    row = lax.broadcasted_iota(jnp.int32, (SUBLANES, bw), 0)

    def gates(off, d, dst_a, dst_b):
        for blk in range(nb):
            cols = slice(blk * bw, (blk + 1) * bw)
            x = xc_ref[pl.ds(off, c), cols]
            g = _dot(x.astype(BF16), w_ref[blk, :, d * 2 * bw:(d + 1) * 2 * bw]) \
                + bias_ref[blk, :, d * 2 * bw:(d + 1) * 2 * bw]
            log_a = LRU_C * _sigmoid(g[:, :bw]) * log_sig[d:d + 1, cols]
            a = jnp.exp(log_a)
            dst_a[:, cols] = a
            dst_b[:, cols] = jnp.sqrt(-jnp.tanh(log_a) * (a * a + 1.0)) * (_sigmoid(g[:, bw:]) * x)

    def step(s, carry):
        off_f = pl.multiple_of(s * c, c)
        off_b = pl.multiple_of(_bwd_chunk(s, n_chunks, n_ctx_chunks) * c, c)
        gates(off_f, 0, a_f, b_f)
        gates(off_b, 1, a_b, b_b)

        def tile(t, carry):
            h_f, h_b = carry
            r_f = pl.multiple_of(t * SUBLANES, SUBLANES)
            r_b = pl.multiple_of((n_tiles - 1 - t) * SUBLANES, SUBLANES)
            new_f, new_b = [], []
            for blk in range(nb):
                cols = slice(blk * bw, (blk + 1) * bw)
                a, b = _tile_scan(a_f[pl.ds(r_f, SUBLANES), cols], b_f[pl.ds(r_f, SUBLANES), cols], row, False)
                h = b + a * h_f[blk]
                acc[pl.ds(off_f + r_f, SUBLANES), cols] += h
                new_f.append(h[SUBLANES - 1:SUBLANES, :])
                a, b = _tile_scan(a_b[pl.ds(r_b, SUBLANES), cols], b_b[pl.ds(r_b, SUBLANES), cols], row, True)
                h = b + a * h_b[blk]
                acc[pl.ds(off_b + r_b, SUBLANES), cols] += h
                new_b.append(h[0:1, :])
            return tuple(new_f), tuple(new_b)

        return lax.fori_loop(0, n_tiles, tile, carry)

    zero = tuple(jnp.zeros((1, bw), F32) for _ in range(nb))
    lax.fori_loop(0, n_chunks, step, (zero, zero))

    def finish(s, _):
        off = pl.multiple_of(s * c, c)
        o_ref[pl.ds(off, c), :] = (acc[pl.ds(off, c), :] * _gelu_tanh(y_ref[pl.ds(off, c), :])).astype(o_ref.dtype)
        return 0

    lax.fori_loop(0, n_chunks, finish, 0)


def _rglru(xc, u_y, y_col0, wa, ba, wx, bx, lam, mix_w, *, n_ctx, blocks_per_step=4):
    bsz, t, _ = xc.shape
    bw = mix_w // LRU_BLOCKS
    gw = blocks_per_step * bw
    w = jnp.concatenate([wa[0], wx[0], wa[1], wx[1]], axis=-1).astype(BF16)
    blk = lambda v: v.reshape(2, LRU_BLOCKS, 1, bw)
    bias = jnp.concatenate([blk(ba)[0], blk(bx)[0], blk(ba)[1], blk(bx)[1]], axis=-1)
    yc = y_col0 // gw
    return pl.pallas_call(
        functools.partial(_lru_kernel, t_total=t, n_ctx=n_ctx),
        grid=(bsz, LRU_BLOCKS // blocks_per_step),
        in_specs=[pl.BlockSpec((None, t, gw), lambda i, j: (i, 0, j)),
                  pl.BlockSpec((None, t, gw), lambda i, j: (i, 0, yc + j)),
                  pl.BlockSpec((blocks_per_step, bw, 4 * bw), lambda i, j: (j, 0, 0)),
                  pl.BlockSpec((blocks_per_step, 1, 4 * bw), lambda i, j: (j, 0, 0)),
                  pl.BlockSpec((2, gw), lambda i, j: (0, j))],
        out_specs=pl.BlockSpec((None, t, gw), lambda i, j: (i, 0, j)),
        out_shape=jax.ShapeDtypeStruct((bsz, t, mix_w), BF16),
        scratch_shapes=[pltpu.VMEM((t, gw), F32)] + [pltpu.VMEM((SCAN_CHUNK, gw), F32)] * 4,
        compiler_params=_params("parallel", "arbitrary"), name="rglru",
    )(xc, u_y, w, bias, lam)


def _hg_tables():
    c = HG_CHUNK
    idx = np.arange(c)
    sel = np.zeros((2, HG_LEVELS, c, c), np.float32)
    isq = np.zeros((2, HG_LEVELS, c, HG_DK), np.float32)
    msk = np.zeros((2, HG_LEVELS, c, c), np.float32)
    for lvl in range(HG_LEVELS):
        half = 1 << lvl
        mid = (idx // (2 * half)) * 2 * half + half
        late = idx >= mid
        same = (idx[:, None] // (2 * half)) == (idx[None, :] // (2 * half))
        sel[0, lvl, idx, mid - 1] = 1.0
        sel[1, lvl, idx, mid] = 1.0
        isq[0, lvl] = late[:, None]
        isq[1, lvl] = ~late[:, None]
        msk[0, lvl] = same & late[:, None] & ~late[None, :]
        msk[1, lvl] = same & ~late[:, None] & late[None, :]
    return (jnp.asarray(sel.reshape(2, HG_LEVELS * c, c), BF16), jnp.asarray(2.0 * isq - 1.0), jnp.asarray(isq),
            jnp.asarray(msk))


def _hg_kernel(q_ref, f0_ref, f1_ref, v_ref, g_ref, lb_ref, nw_ref, ord_ref, sel_ref, sgn_ref, isq_ref, msk_ref,
               o_ref, acc_f, acc_b, *, t_total, n_ctx):
    c = HG_CHUNK
    n_chunks, n_ctx_chunks = t_total // c, n_ctx // c
    eye = lax.broadcasted_iota(jnp.int32, (c, c), 0) == lax.broadcasted_iota(jnp.int32, (c, c), 1)

    def chunk(off, st, d):
        fwd = d == 0
        lb = lb_ref[d:d + 1, :]
        f = (f0_ref if fwd else f1_ref)[pl.ds(off, c), :]
        log_f = jnp.log(lb + (1.0 - lb) * _sigmoid(f))
        k = (1.0 - lb) * _sigmoid(-f)
        q = _silu(q_ref[pl.ds(off, c), :])
        v = v_ref[pl.ds(off, c), :].astype(BF16)
        cum3 = _dot(ord_ref[d], jnp.concatenate(_split_bf16(log_f, 3), axis=1))
        cum = cum3[:, :HG_DK] + cum3[:, HG_DK:2 * HG_DK] + cum3[:, 2 * HG_DK:]
        bnd = _dot(sel_ref[d], jnp.concatenate(_split_bf16(cum, 2), axis=1))
        q_minus_k = q - k
        scores = jnp.where(eye, _dot_nt(q.astype(BF16), k.astype(BF16)), 0.0)
        for lvl in range(HG_LEVELS):
            rows = slice(lvl * c, (lvl + 1) * c)
            e = cum - (bnd[rows, :HG_DK] + bnd[rows, HG_DK:])
            y = ((k + isq_ref[d, lvl] * q_minus_k) * jnp.exp(e * sgn_ref[d, lvl])).astype(BF16)
            scores = scores + _dot_nt(y, y) * msk_ref[d, lvl]
        last = cum[c - 1:c, :] if fwd else cum[0:1, :]
        o = _dot(scores.astype(BF16), v) + _dot_nt((q * jnp.exp(cum)).astype(BF16), st.astype(BF16))
        st = st * jnp.exp(last) + _dot_tn(v, (k * jnp.exp(last - cum)).astype(BF16))
        return o, st

    def post(off):
        o = acc_f[pl.ds(off, c), :] + acc_b[pl.ds(off, c), :]
        y = o * lax.rsqrt(jnp.mean(o * o, axis=-1, keepdims=True) + EPS) * nw_ref[...]
        o_ref[pl.ds(off, c), :] = (y * _silu(g_ref[pl.ds(off, c), :])).astype(o_ref.dtype)

    _scan_both_ways(n_chunks, n_ctx_chunks, c, chunk, acc_f, acc_b, jnp.zeros((HG_DK, HG_DK), F32), post)


def _hgrn2(u, col0, mix_w, lower, norm_w, *, n_ctx):
    bsz, t, _ = u.shape
    heads = mix_w // HG_DK
    c0 = col0 // HG_DK
    tables = (_scan_order_tables(HG_CHUNK),) + _hg_tables()

    def spec(k):
        return pl.BlockSpec((None, t, HG_DK), lambda i, h, k=k: (i, 0, c0 + k * heads + h))

    def table(v):
        return pl.BlockSpec(v.shape, lambda i, h, nd=v.ndim: (0,) * nd)

    return pl.pallas_call(
        functools.partial(_hg_kernel, t_total=t, n_ctx=n_ctx),
        grid=(bsz, heads),
        in_specs=[spec(0), spec(1), spec(2), spec(3), spec(4),
                  pl.BlockSpec((2, HG_DK), lambda i, h: (0, h)),
                  pl.BlockSpec((1, HG_DK), lambda i, h: (0, h))] + [table(v) for v in tables],
        out_specs=pl.BlockSpec((None, t, HG_DK), lambda i, h: (i, 0, h)),
        out_shape=jax.ShapeDtypeStruct((bsz, t, mix_w), BF16),
        scratch_shapes=[pltpu.VMEM((t, HG_DK), F32)] * 2,
        compiler_params=_params("parallel", "arbitrary"), name="hgrn2",
    )(u, u, u, u, u, lower, norm_w.reshape(1, mix_w), *tables)


def _ada_kernel(c_ref, w_ref, b_ref, o_ref):
    o_ref[...] = _dot(_silu(c_ref[...]).astype(BF16), w_ref[...].astype(BF16)) + b_ref[...]


def _ada(cond, ada_w, ada_b, tn=1024):
    r, d = cond.shape
    depth, _, n = ada_w.shape
    return pl.pallas_call(
        _ada_kernel,
        grid=(depth, n // tn),
        in_specs=[pl.BlockSpec((r, d), lambda l, j: (0, 0)),
                  pl.BlockSpec((None, d, tn), lambda l, j: (l, 0, j)),
                  pl.BlockSpec((None, 1, tn), lambda l, j: (l, 0, j))],
        out_specs=pl.BlockSpec((None, r, tn), lambda l, j: (l, 0, j)),
        out_shape=jax.ShapeDtypeStruct((depth, r, n), F32),
        compiler_params=_params("parallel", "parallel"), name="ada",
    )(cond, ada_w, ada_b.reshape(depth, 1, n))


def _rope_tables(n_ctx, n_lat):
    r, col = jnp.meshgrid(jnp.arange(n_lat // GRID_W), jnp.arange(GRID_W), indexing='ij')
    n_freq = RET_DK // 4
    freqs = ROPE_BASE ** (-jnp.arange(n_freq, dtype=F32) / n_freq)
    ang = jnp.concatenate([r.reshape(-1, 1) * freqs, col.reshape(-1, 1) * freqs], axis=-1)
    cos, sin = jnp.cos(ang), jnp.sin(ang)
    cos2 = jnp.concatenate([jnp.ones((n_ctx, RET_DK), F32), jnp.concatenate([cos, cos], axis=-1)], axis=0)
    sin2 = jnp.concatenate([jnp.zeros((n_ctx, RET_DK), F32), jnp.concatenate([-sin, sin], axis=-1)], axis=0)
    return cos2, sin2


def kernel(x, c, ctx, c_ctx, ada_w, ada_b, norm1_w, norm2_w, w_in, ssd_conv_w, ssd_conv_b, ssd_a_log, ssd_dt_bias, ssd_d, ssd_norm_w, lru_conv_w, lru_conv_b, lru_wa, lru_ba, lru_wx, lru_bx, lru_lambda, hg_lb_logits, hg_norm_w, w_branch, w_out, ffn_wgu, ffn_w2, router_w, moe_wgu, moe_w2, final_norm_w):
    bsz, n_lat, d = x.shape
    n_ctx = ctx.shape[1]
    t = n_ctx + n_lat
    depth = w_in.shape[0]
    mix_w = d // 2
    ssd_heads = mix_w // SSD_HD
    xbc_w = mix_w + 2 * SSD_GROUPS * SSD_STATE
    widths = (4 * mix_w, mix_w + xbc_w, 2 * ssd_heads, 2 * mix_w, 5 * mix_w, N_BRANCH * d)
    starts = np.concatenate([[0], np.cumsum(widths)])

    cos2, sin2 = _rope_tables(n_ctx, n_lat)
    p_lb = jax.nn.softmax(hg_lb_logits.astype(F32), axis=1)
    lower_bounds = jnp.cumsum(p_lb, axis=1) - p_lb[:, :1]

    pad = (-(1 + bsz)) % SUBLANES
    cond = jnp.concatenate([c_ctx[None], c, jnp.zeros((pad, d), F32)], axis=0)
    mods = _ada(cond, ada_w, ada_b).reshape(depth, -1, N_MOD, d)

    h = jnp.concatenate([ctx.astype(x.dtype), x], axis=1)
    for layer in range(depth):
        m_ctx = jnp.broadcast_to(mods[layer, 0], (bsz, N_MOD, d))
        m_lat = mods[layer, 1:1 + bsz]
        mod_rows = jnp.stack([m_ctx, m_lat], axis=1)
        mod_cols = jnp.stack([m_ctx, m_lat], axis=2)

        xn = _norm_call(h, norm1_w[layer], mod_rows, shift=0, scale=1, n_ctx=n_ctx).reshape(bsz * t, d)
        u_ret, u_ssd, u_dt, u_lru, u_hg, u_gates = (
            _matmul(xn, w_in[layer, :, starts[i]:starts[i + 1]].astype(BF16)).reshape(bsz, t, widths[i])
            for i in range(len(widths)))

        b_ret = _retention(u_ret, 0, mix_w, cos2, sin2, n_ctx=n_ctx)
        xact = _dwconv(u_ssd, mix_w, xbc_w, ssd_conv_w[layer], ssd_conv_b[layer], n_ctx=n_ctx, act=True)
        b_ssd = _ssd(u_ssd, 0, xact, u_dt, ssd_dt_bias[layer], ssd_a_log[layer], ssd_d[layer],
                     ssd_norm_w[layer], mix_w, n_ctx=n_ctx)
        xc = _dwconv(u_lru, 0, mix_w, lru_conv_w[layer], lru_conv_b[layer], n_ctx=n_ctx, act=False)
        b_lru = _rglru(xc, u_lru, mix_w, lru_wa[layer], lru_ba[layer], lru_wx[layer], lru_bx[layer],
                       lru_lambda[layer], mix_w, n_ctx=n_ctx)
        b_hg = _hgrn2(u_hg, 0, mix_w, lower_bounds[:, layer], hg_norm_w[layer], n_ctx=n_ctx)

        branches = [v.reshape(bsz * t, mix_w) for v in (b_ret, b_ssd, b_lru, b_hg)]
        mixed = _merge(branches, u_gates.reshape(bsz * t, N_BRANCH * d), w_branch[layer].astype(BF16), tm=768)
        h2 = _matmul_residual(mixed, w_out[layer].astype(BF16), h.reshape(bsz * t, d), mod_cols, 2,
                              t_total=t, n_ctx=n_ctx)

        if layer % 2 == 0:
            xn = _norm_call(h2.reshape(bsz, t, d), norm2_w[layer], mod_rows, shift=3, scale=4, n_ctx=n_ctx)
            act = _swiglu_up(xn.reshape(bsz * t, d), ffn_wgu[layer // 2].astype(BF16))
            h = _matmul_residual(act, ffn_w2[layer // 2].astype(BF16), h2, mod_cols, 5,
                                 t_total=t, n_ctx=n_ctx).reshape(bsz, t, d)
        else:
            xn, gates = _norm_call(h2.reshape(bsz, t, d), norm2_w[layer], mod_rows, shift=3, scale=4, n_ctx=n_ctx,
                                   router_w=router_w[layer // 2])
            h = _moe_block(xn, gates, h2.reshape(bsz, t, d), moe_wgu[layer // 2], moe_w2[layer // 2],
                           mod_rows, 5, n_ctx=n_ctx)
    return _final_norm_call(h, final_norm_w, n_ctx)
```

```python
import functools
import math

import numpy as np
import jax
import jax.numpy as jnp
from jax import lax
from jax.experimental import pallas as pl
from jax.experimental.pallas import tpu as pltpu

F32 = jnp.float32
BF16 = jnp.bfloat16
HI = lax.Precision.HIGHEST

EPS = 1e-6
GRID_W = 64
ROPE_BASE = 10000.0
RET_DK = 128
SSD_HD = 64
SSD_GROUPS = 4
SSD_STATE = 128
CONV_W = 4
LRU_BLOCKS = 8
LRU_C = 8.0
HG_DK = 128
N_BRANCH = 4
N_EXPERTS = 8
N_MOD = 6

LANES = 128
SUBLANES = 8
VMEM_LIMIT = 52 * 1024 * 1024

SCAN_CHUNK = 256
HG_CHUNK = 128
HG_LEVELS = 7


def _params(*sem):
    return pltpu.CompilerParams(dimension_semantics=sem, vmem_limit_bytes=VMEM_LIMIT)


def _dot(a, b, prec=None):
    return jnp.dot(a, b, preferred_element_type=F32, precision=prec)


def _dot_nt(a, b, prec=None):
    return lax.dot_general(a, b, (((1,), (1,)), ((), ())), preferred_element_type=F32, precision=prec)


def _dot_tn(a, b, prec=None):
    return lax.dot_general(a, b, (((0,), (0,)), ((), ())), preferred_element_type=F32, precision=prec)


def _sigmoid(x):
    return 1.0 / (1.0 + jnp.exp(-x))


def _silu(x):
    return x * _sigmoid(x)


def _softplus(x):
    return jnp.maximum(x, 0.0) + jnp.log1p(jnp.exp(-jnp.abs(x)))


def _gelu_tanh(x):
    return 0.5 * x * (1.0 + jnp.tanh(math.sqrt(2.0 / math.pi) * (x + 0.044715 * (x * x * x))))


def _bwd_chunk(s, n_chunks, n_ctx_chunks):
    return jnp.where(s < n_ctx_chunks, n_ctx_chunks - 1 - s, n_chunks - 1 - (s - n_ctx_chunks))


def _scan_both_ways(n_chunks, n_ctx_chunks, c, chunk_fn, acc_f, acc_b, zero_state, post_fn):
    def body(s, carry):
        st_f, st_b = carry
        off_f = pl.multiple_of(s * c, c)
        off_b = pl.multiple_of(_bwd_chunk(s, n_chunks, n_ctx_chunks) * c, c)
        o_f, st_f = chunk_fn(off_f, st_f, 0)
        o_b, st_b = chunk_fn(off_b, st_b, 1)
        acc_f[pl.ds(off_f, c), :] = o_f
        acc_b[pl.ds(off_b, c), :] = o_b
        return st_f, st_b

    lax.fori_loop(0, n_chunks, body, (zero_state, zero_state), unroll=2)

    def finish(s, _):
        post_fn(pl.multiple_of(s * c, c))
        return 0

    lax.fori_loop(0, n_chunks, finish, 0)


def _scan_order_tables(c):
    idx = np.arange(c)
    lower = (idx[None, :] <= idx[:, None]).astype(np.float32)
    return jnp.asarray(np.stack([lower, lower.T]), BF16)


def _split_bf16(x, parts):
    out = []
    for _ in range(parts):
        p = x.astype(BF16)
        out.append(p)
        x = x - p.astype(F32)
    return out


def _norm_kernel(h_ref, w_ref, mod_ref, o_ref, *, shift, scale):
    x = h_ref[...]
    y = x * lax.rsqrt(jnp.mean(x * x, axis=-1, keepdims=True) + EPS) * w_ref[...]
    o_ref[...] = (y * (1.0 + mod_ref[scale:scale + 1, :]) + mod_ref[shift:shift + 1, :]).astype(o_ref.dtype)


def _norm_router_kernel(h_ref, w_ref, mod_ref, rw_ref, o_ref, gate_ref, *, shift, scale):
    x = h_ref[...]
    y = x * lax.rsqrt(jnp.mean(x * x, axis=-1, keepdims=True) + EPS) * w_ref[...]
    xn = y * (1.0 + mod_ref[scale:scale + 1, :]) + mod_ref[shift:shift + 1, :]
    o_ref[...] = xn.astype(o_ref.dtype)
    logits = _dot(xn.astype(BF16), rw_ref[...])
    lane = lax.broadcasted_iota(jnp.int32, logits.shape, 1)
    neg = jnp.float32(-jnp.inf)
    logits = jnp.where(lane < N_EXPERTS, logits, neg)
    v1 = jnp.max(logits, axis=-1, keepdims=True)
    i1 = jnp.min(jnp.where(logits == v1, lane, LANES), axis=-1, keepdims=True)
    rest = jnp.where(lane == i1, neg, logits)
    v2 = jnp.max(rest, axis=-1, keepdims=True)
    i2 = jnp.min(jnp.where(rest == v2, lane, LANES), axis=-1, keepdims=True)
    e2 = jnp.exp(v2 - v1)
    w1 = 1.0 / (1.0 + e2)
    w2 = e2 / (1.0 + e2)
    gate_ref[...] = (jnp.where(lane == i1, w1, 0.0) + jnp.where(lane == i2, w2, 0.0)
                     + jnp.where(lane == N_EXPERTS, i1.astype(F32), 0.0)
                     + jnp.where(lane == N_EXPERTS + 1, i2.astype(F32), 0.0))


def _norm_call(h, w, mod, *, shift, scale, n_ctx, router_w=None, out_dtype=BF16):
    b, t, d = h.shape
    tr = n_ctx
    grid = (b, t // tr)
    h_spec = pl.BlockSpec((None, tr, d), lambda i, j: (i, j, 0))
    w_spec = pl.BlockSpec((1, d), lambda i, j: (0, 0))
    mod_spec = pl.BlockSpec((None, None, N_MOD, d), lambda i, j: (i, jnp.minimum(j, 1), 0, 0))
    if router_w is None:
        return pl.pallas_call(
            functools.partial(_norm_kernel, shift=shift, scale=scale),
            grid=grid, in_specs=[h_spec, w_spec, mod_spec], out_specs=h_spec,
            out_shape=jax.ShapeDtypeStruct(h.shape, out_dtype),
            compiler_params=_params("parallel", "parallel"), name="norm_mod",
        )(h, w.reshape(1, d), mod)
    rw = jnp.zeros((d, LANES), BF16).at[:, :N_EXPERTS].set(router_w.astype(BF16))
    return pl.pallas_call(
        functools.partial(_norm_router_kernel, shift=shift, scale=scale),
        grid=grid,
        in_specs=[h_spec, w_spec, mod_spec, pl.BlockSpec((d, LANES), lambda i, j: (0, 0))],
        out_specs=[h_spec, pl.BlockSpec((None, tr, LANES), lambda i, j: (i, j, 0))],
        out_shape=[jax.ShapeDtypeStruct(h.shape, out_dtype), jax.ShapeDtypeStruct((b, t, LANES), F32)],
        compiler_params=_params("parallel", "parallel"), name="norm_mod_router",
    )(h, w.reshape(1, d), mod, rw)


def _final_norm_kernel(h_ref, w_ref, o_ref):
    x = h_ref[...]
    o_ref[...] = x * lax.rsqrt(jnp.mean(x * x, axis=-1, keepdims=True) + EPS) * w_ref[...]


def _final_norm_call(h, w, n_ctx):
    b, t, d = h.shape
    tr = n_ctx
    return pl.pallas_call(
        _final_norm_kernel,
        grid=(b, (t - n_ctx) // tr),
        in_specs=[pl.BlockSpec((None, tr, d), lambda i, j: (i, j + 1, 0)), pl.BlockSpec((1, d), lambda i, j: (0, 0))],
        out_specs=pl.BlockSpec((None, tr, d), lambda i, j: (i, j, 0)),
        out_shape=jax.ShapeDtypeStruct((b, t - n_ctx, d), F32),
        compiler_params=_params("parallel", "parallel"), name="final_norm",
    )(h, w.reshape(1, d))


def _pick(n, prefs):
    for p in prefs:
        if n % p == 0:
            return p
    return n


def _mm_kernel(a_ref, w_ref, o_ref, acc_ref, *, nk):
    k = pl.program_id(2)
    part = _dot(a_ref[...], w_ref[...])
    if nk == 1:
        o_ref[...] = part.astype(o_ref.dtype)
        return

    @pl.when(k == 0)
    def _():
        acc_ref[...] = part

    @pl.when(k > 0)
    def _():
        acc_ref[...] += part

    @pl.when(k == nk - 1)
    def _():
        o_ref[...] = acc_ref[...].astype(o_ref.dtype)


def _matmul(a, w, layer, col0, n, out_dtype=F32, tm=None, tn=None, tk=None):
    m, kd = a.shape
    tm = tm or _pick(m, (1152, 1024, 768, 512, 256))
    tn = tn or _pick(math.gcd(n, col0) if col0 else n, (1024, 512, 256, 128))
    tk = tk or (kd if kd <= 2048 else _pick(kd, (512, 256, 128)))
    nk = kd // tk
    c0 = col0 // tn
    return pl.pallas_call(
        functools.partial(_mm_kernel, nk=nk),
        grid=(n // tn, m // tm, nk),
        in_specs=[pl.BlockSpec((tm, tk), lambda j, i, k: (i, k)),
                  pl.BlockSpec((None, tk, tn), lambda j, i, k: (layer, k, c0 + j))],
        out_specs=pl.BlockSpec((tm, tn), lambda j, i, k: (i, j)),
        out_shape=jax.ShapeDtypeStruct((m, n), out_dtype),
        scratch_shapes=[pltpu.VMEM((tm, tn), F32)],
        compiler_params=_params("parallel", "parallel", "arbitrary"), name="matmul",
    )(a, w)


def _row_is_ctx(tm, t_total, n_ctx):
    row0 = pl.program_id(1) * tm
    pos = (row0 + lax.broadcasted_iota(jnp.int32, (tm, 1), 0)) % t_total
    return pos < n_ctx


def _mm_res_kernel(a_ref, w_ref, res_ref, mod_ref, o_ref, acc_ref, *, nk, tm, t_total, n_ctx):
    k = pl.program_id(2)
    part = _dot(a_ref[...], w_ref[...])

    @pl.when(k == 0)
    def _():
        acc_ref[...] = part

    @pl.when(k > 0)
    def _():
        acc_ref[...] += part

    @pl.when(k == nk - 1)
    def _():
        mod = jnp.where(_row_is_ctx(tm, t_total, n_ctx), mod_ref[0:1, :], mod_ref[1:2, :])
        o_ref[...] = res_ref[...] + mod * acc_ref[...]


def _matmul_residual(a, w, layer, res, mod, mod_idx, *, t_total, n_ctx, tm=None, tn=None, tk=None):
    m, kd = a.shape
    n = w.shape[2]
    tm = tm or _pick(t_total, (1152, 768, 256))
    tn = tn or _pick(n, (1024, 512, 256, 128))
    tk = tk or (kd if kd <= 2048 else _pick(kd, (512, 256, 128)))
    nk = kd // tk
    per_b = t_total // tm
    return pl.pallas_call(
        functools.partial(_mm_res_kernel, nk=nk, tm=tm, t_total=t_total, n_ctx=n_ctx),
        grid=(n // tn, m // tm, nk),
        in_specs=[pl.BlockSpec((tm, tk), lambda j, i, k: (i, k)),
                  pl.BlockSpec((None, tk, tn), lambda j, i, k: (layer, k, j)),
                  pl.BlockSpec((tm, tn), lambda j, i, k: (i, j)),
                  pl.BlockSpec((None, None, 2, tn), lambda j, i, k: (i // per_b, mod_idx, 0, j))],
        out_specs=pl.BlockSpec((tm, tn), lambda j, i, k: (i, j)),
        out_shape=jax.ShapeDtypeStruct((m, n), F32),
        scratch_shapes=[pltpu.VMEM((tm, tn), F32)],
        compiler_params=_params("parallel", "parallel", "arbitrary"), name="matmul_residual",
    )(a, w, res, mod)


def _swiglu_kernel(a_ref, wg_ref, wu_ref, o_ref, wg_bf, wu_bf):
    @pl.when(pl.program_id(1) == 0)
    def _():
        wg_bf[...] = wg_ref[...].astype(BF16)
        wu_bf[...] = wu_ref[...].astype(BF16)

    a = a_ref[...]
    o_ref[...] = (_silu(_dot(a, wg_bf[...])) * _dot(a, wu_bf[...])).astype(o_ref.dtype)


def _swiglu_up(a, wgu, layer, tm=None, tn=None):
    m, kd = a.shape
    f = wgu.shape[2] // 2
    tm = tm or _pick(m, (1152, 1024, 768, 512, 256))
    tn = tn or _pick(f, (512, 256, 128))
    nf = f // tn
    return pl.pallas_call(
        _swiglu_kernel,
        grid=(nf, m // tm),
        in_specs=[pl.BlockSpec((tm, kd), lambda j, i: (i, 0)),
                  pl.BlockSpec((None, kd, tn), lambda j, i: (layer, 0, j)),
                  pl.BlockSpec((None, kd, tn), lambda j, i: (layer, 0, j + nf))],
        out_specs=pl.BlockSpec((tm, tn), lambda j, i: (i, j)),
        out_shape=jax.ShapeDtypeStruct((m, f), BF16),
        scratch_shapes=[pltpu.VMEM((kd, tn), BF16)] * 2,
        compiler_params=_params("arbitrary", "arbitrary"), name="swiglu_up",
    )(a, wgu, wgu)


MOE_TM = 512
GATHER_ROWS = 256


def _route(gates, n_tok):
    tm = MOE_TM
    n_slot = 2 * n_tok
    n_tiles = n_slot // tm + N_EXPERTS
    i32 = jnp.int32
    eid = gates[:, N_EXPERTS:N_EXPERTS + 2].astype(i32).reshape(n_slot)
    order = jnp.argsort(eid, stable=True).astype(i32)
    rank = jnp.argsort(order).astype(i32)
    onehot = eid[:, None] == jnp.arange(N_EXPERTS, dtype=i32)[None, :]
    counts = jnp.sum(onehot, axis=0, dtype=i32)
    padded = (counts + tm - 1) // tm * tm
    ends = jnp.cumsum(padded)
    group_end = jnp.cumsum(counts)
    shift = (ends - padded) - (group_end - counts)
    dest = (rank + jnp.sum(jnp.where(onehot, shift[None, :], 0), axis=1, dtype=i32)).reshape(n_tok, 2)
    tile_start = jnp.arange(n_tiles, dtype=i32) * tm
    tile_expert = jnp.minimum(jnp.sum(tile_start[:, None] >= ends[None, :], axis=1, dtype=i32), N_EXPERTS - 1)
    row_expert = jnp.repeat(tile_expert, tm)
    pos = jnp.arange(n_tiles * tm, dtype=i32) - shift[row_expert]
    row_src = jnp.where(pos < group_end[row_expert], order[jnp.clip(pos, 0, n_slot - 1)] // 2, 0)
    tile_first = jnp.concatenate([jnp.ones((1,), i32), (tile_expert[1:] != tile_expert[:-1]).astype(i32)])
    return row_src, dest, tile_expert, tile_first, (ends[-1:] // tm).astype(i32)


def _gather_kernel(idx_ref, src_ref, o_ref, sem, *, rows):
    base = pl.program_id(0) * rows

    def row_copy(r, src_row):
        return pltpu.make_async_copy(src_ref.at[pl.ds(src_row, 1), :], o_ref.at[pl.ds(r, 1), :], sem)

    def issue(r, carry):
        row_copy(r, idx_ref[base + r]).start()
        return carry

    def drain(r, carry):
        row_copy(r, 0).wait()
        return carry

    lax.fori_loop(0, rows, issue, 0)
    lax.fori_loop(0, rows, drain, 0)


def _gather_rows(src, idx):
    p, w = idx.shape[0], src.shape[1]
    rows = GATHER_ROWS
    return pl.pallas_call(
        functools.partial(_gather_kernel, rows=rows),
        grid_spec=pltpu.PrefetchScalarGridSpec(
            num_scalar_prefetch=1, grid=(p // rows,),
            in_specs=[pl.BlockSpec(memory_space=pl.ANY)],
            out_specs=pl.BlockSpec((rows, w), lambda i, idx: (i, 0)),
            scratch_shapes=[pltpu.SemaphoreType.DMA(())]),
        out_shape=jax.ShapeDtypeStruct((p, w), src.dtype),
        compiler_params=_params("arbitrary"), name="gather_rows",
    )(idx, src)


def _moe_up_kernel(te_ref, tf_ref, nu_ref, a_ref, wg_ref, wu_ref, o_ref, wg_bf, wu_bf):
    i = pl.program_id(1)

    @pl.when(tf_ref[i] == 1)
    def _():
        wg_bf[...] = wg_ref[...].astype(BF16)
        wu_bf[...] = wu_ref[...].astype(BF16)

    @pl.when(i < nu_ref[0])
    def _():
        a = a_ref[...].astype(BF16)
        o_ref[...] = (_silu(_dot(a, wg_bf[...])) * _dot(a, wu_bf[...])).astype(o_ref.dtype)

    @pl.when(i >= nu_ref[0])
    def _():
        o_ref[...] = jnp.zeros_like(o_ref)


def _moe_up(xg, wgu, layer, tile_expert, tile_first, n_used, tn=512):
    p, kd = xg.shape
    f = wgu.shape[3] // 2
    nf = f // tn
    tm = MOE_TM
    return pl.pallas_call(
        _moe_up_kernel,
        grid_spec=pltpu.PrefetchScalarGridSpec(
            num_scalar_prefetch=3, grid=(nf, p // tm),
            in_specs=[pl.BlockSpec((tm, kd), lambda j, i, te, tf, nu: (i, 0)),
                      pl.BlockSpec((None, None, kd, tn), lambda j, i, te, tf, nu: (layer, te[i], 0, j)),
                      pl.BlockSpec((None, None, kd, tn), lambda j, i, te, tf, nu: (layer, te[i], 0, j + nf))],
            out_specs=pl.BlockSpec((tm, tn), lambda j, i, te, tf, nu: (i, j)),
            scratch_shapes=[pltpu.VMEM((kd, tn), BF16)] * 2),
        out_shape=jax.ShapeDtypeStruct((p, f), BF16),
        compiler_params=_params("arbitrary", "arbitrary"), name="moe_up",
    )(tile_expert, tile_first, n_used, xg, wgu, wgu)


def _moe_down_kernel(te_ref, tf_ref, nu_ref, a_ref, w_ref, o_ref, w_bf):
    i = pl.program_id(1)

    @pl.when(tf_ref[i] == 1)
    def _():
        w_bf[...] = w_ref[...].astype(BF16)

    @pl.when(i < nu_ref[0])
    def _():
        o_ref[...] = _dot(a_ref[...], w_bf[...])

    @pl.when(i >= nu_ref[0])
    def _():
        o_ref[...] = jnp.zeros_like(o_ref)


def _moe_down(act, w2, layer, tile_expert, tile_first, n_used, tn=512):
    p, f = act.shape
    d = w2.shape[3]
    tm = MOE_TM
    return pl.pallas_call(
        _moe_down_kernel,
        grid_spec=pltpu.PrefetchScalarGridSpec(
            num_scalar_prefetch=3, grid=(d // tn, p // tm),
            in_specs=[pl.BlockSpec((tm, f), lambda j, i, te, tf, nu: (i, 0)),
                      pl.BlockSpec((None, None, f, tn), lambda j, i, te, tf, nu: (layer, te[i], 0, j))],
            out_specs=pl.BlockSpec((tm, tn), lambda j, i, te, tf, nu: (i, j)),
            scratch_shapes=[pltpu.VMEM((f, tn), BF16)]),
        out_shape=jax.ShapeDtypeStruct((p, d), F32),
        compiler_params=_params("arbitrary", "arbitrary"), name="moe_down",
    )(tile_expert, tile_first, n_used, act, w2)


def _moe_combine_kernel(y0_ref, y1_ref, gate_ref, res_ref, mod_ref, o_ref, *, mod_idx):
    gates = gate_ref[...]
    lane = lax.broadcasted_iota(jnp.int32, gates.shape, 1)
    i0 = gates[:, N_EXPERTS:N_EXPERTS + 1].astype(jnp.int32)
    i1 = gates[:, N_EXPERTS + 1:N_EXPERTS + 2].astype(jnp.int32)
    w0 = jnp.sum(jnp.where(lane == i0, gates, 0.0), axis=-1, keepdims=True)
    w1 = jnp.sum(jnp.where(lane == i1, gates, 0.0), axis=-1, keepdims=True)
    o_ref[...] = res_ref[...] + mod_ref[mod_idx:mod_idx + 1, :] * (w0 * y0_ref[...] + w1 * y1_ref[...])


def _moe_combine(y0, y1, gates, res, mod, mod_idx, *, n_ctx):
    b, t, d = res.shape
    tr = n_ctx
    spec = pl.BlockSpec((None, tr, d), lambda i, j: (i, j, 0))
    return pl.pallas_call(
        functools.partial(_moe_combine_kernel, mod_idx=mod_idx),
        grid=(b, t // tr),
        in_specs=[spec, spec, pl.BlockSpec((None, tr, LANES), lambda i, j: (i, j, 0)), spec,
                  pl.BlockSpec((None, None, N_MOD, d), lambda i, j: (i, jnp.minimum(j, 1), 0, 0))],
        out_specs=spec,
        out_shape=jax.ShapeDtypeStruct((b, t, d), F32),
        compiler_params=_params("parallel", "parallel"), name="moe_combine",
    )(y0, y1, gates, res, mod)


def _moe_block(xn, gates, res, wgu, w2, layer, mod, mod_idx, *, n_ctx):
    b, t, d = xn.shape
    m = b * t
    row_src, dest, tile_expert, tile_first, n_used = _route(gates.reshape(m, LANES), m)
    xg = _gather_rows(xn.reshape(m, d), row_src)
    act = _moe_up(xg, wgu, layer, tile_expert, tile_first, n_used)
    yg = _moe_down(act, w2, layer, tile_expert, tile_first, n_used)
    y0 = _gather_rows(yg, dest[:, 0]).reshape(b, t, d)
    y1 = _gather_rows(yg, dest[:, 1]).reshape(b, t, d)
    return _moe_combine(y0, y1, gates, res, mod, mod_idx, n_ctx=n_ctx)


def _merge_kernel(b0, b1, b2, b3, w_ref, g0, g1, g2, g3, o_ref):
    acc = None
    for n, (br, g) in enumerate(((b0, g0), (b1, g1), (b2, g2), (b3, g3))):
        term = _sigmoid(g[...]) * _dot(br[...], w_ref[n])
        acc = term if acc is None else acc + term
    o_ref[...] = acc.astype(o_ref.dtype)


def _merge(branches, gates, wb, layer, tm=None, tn=None):
    m, kd = branches[0].shape
    d = wb.shape[-1]
    tm = tm or _pick(m, (1152, 1024, 768, 512, 256))
    tn = tn or _pick(d, (512, 256, 128))
    nd = d // tn
    br_spec = pl.BlockSpec((tm, kd), lambda j, i: (i, 0))
    gate_specs = [pl.BlockSpec((tm, tn), functools.partial(lambda j, i, n: (i, j + n * nd), n=n))
                  for n in range(N_BRANCH)]
    return pl.pallas_call(
        _merge_kernel,
        grid=(nd, m // tm),
        in_specs=[br_spec] * N_BRANCH + [pl.BlockSpec((None, N_BRANCH, kd, tn), lambda j, i: (layer, 0, 0, j))]
        + gate_specs,
        out_specs=pl.BlockSpec((tm, tn), lambda j, i: (i, j)),
        out_shape=jax.ShapeDtypeStruct((m, d), BF16),
        compiler_params=_params("parallel", "parallel"), name="merge",
    )(*branches, wb, gates, gates, gates, gates)


def _conv_kernel(u_ref, w_ref, b_ref, o_ref, *, t_total, n_ctx, act):
    x = u_ref[...]
    pos = lax.broadcasted_iota(jnp.int32, (t_total, 1), 0)
    seg_lo = jnp.where(pos < n_ctx, 0, n_ctx)
    seg_hi = jnp.where(pos < n_ctx, n_ctx, t_total)
    left = (CONV_W - 1) // 2
    acc = None
    for j in range(CONV_W):
        d = j - left
        xs = x if d == 0 else pltpu.roll(x, (-d) % t_total, 0)
        ok = (pos + d >= seg_lo) & (pos + d < seg_hi)
        term = jnp.where(ok, xs, 0.0) * w_ref[j:j + 1, :]
        acc = term if acc is None else acc + term
    acc = acc + b_ref[...]
    o_ref[...] = _silu(acc) if act else acc


def _dwconv(u, col0, width, w, b, *, n_ctx, act, wt=256):
    bsz, t, _ = u.shape
    c0 = col0 // wt
    return pl.pallas_call(
        functools.partial(_conv_kernel, t_total=t, n_ctx=n_ctx, act=act),
        grid=(bsz, width // wt),
        in_specs=[pl.BlockSpec((None, t, wt), lambda i, j: (i, 0, j + c0)),
                  pl.BlockSpec((CONV_W, wt), lambda i, j: (0, j)),
                  pl.BlockSpec((1, wt), lambda i, j: (0, j))],
        out_specs=pl.BlockSpec((None, t, wt), lambda i, j: (i, 0, j)),
        out_shape=jax.ShapeDtypeStruct((bsz, t, width), F32),
        compiler_params=_params("parallel", "parallel"), name="dwconv",
    )(u, w, b.reshape(1, width))


def _ret_kernel(lg_ref, q_ref, k_ref, v_ref, g_ref, cos_ref, sin_ref, o_ref, qs, ks, acc_f, acc_b,
                *, t_total, n_ctx):
    c = SCAN_CHUNK
    n_chunks, n_ctx_chunks = t_total // c, n_ctx // c
    lg = lg_ref[pl.program_id(1)]
    cos, sin = cos_ref[...], sin_ref[...]
    q, k = q_ref[...], k_ref[...]
    qs[...] = q * cos + pltpu.roll(q, RET_DK // 2, 1) * sin
    ks[...] = (k * cos + pltpu.roll(k, RET_DK // 2, 1) * sin) * (RET_DK ** -0.5)

    ii = lax.broadcasted_iota(jnp.int32, (c, c), 0)
    jj = lax.broadcasted_iota(jnp.int32, (c, c), 1)
    dist = (ii - jj).astype(F32)
    r = lax.broadcasted_iota(jnp.int32, (c, 1), 0).astype(F32)
    decay_all = jnp.exp(c * lg)

    def chunk(off, s_in, fwd):
        qc = qs[pl.ds(off, c), :]
        kc = ks[pl.ds(off, c), :]
        vc = v_ref[pl.ds(off, c), :].astype(BF16)
        if fwd:
            mask, steps, q_pow, k_pow = ii >= jj, dist, r + 1.0, c - 1.0 - r
        else:
            mask, steps, q_pow, k_pow = jj >= ii, -dist, c - r, r
        decay = jnp.where(mask, jnp.exp(jnp.where(mask, steps, 0.0) * lg), 0.0)
        scores = _dot_nt(qc.astype(BF16), kc.astype(BF16)) * decay
        o = _dot(scores.astype(BF16), vc) + jnp.exp(q_pow * lg) * _dot(qc.astype(BF16), s_in.astype(BF16))
        s_out = decay_all * s_in + _dot_tn((kc * jnp.exp(k_pow * lg)).astype(BF16), vc)
        return o, s_out

    def post(off):
        o = acc_f[pl.ds(off, c), :] + acc_b[pl.ds(off, c), :]
        y = o * lax.rsqrt(jnp.mean(o * o, axis=-1, keepdims=True) + EPS)
        o_ref[pl.ds(off, c), :] = (y * _silu(g_ref[pl.ds(off, c), :])).astype(o_ref.dtype)

    _scan_both_ways(n_chunks, n_ctx_chunks, c, lambda off, st, d: chunk(off, st, d == 0), acc_f, acc_b,
                    jnp.zeros((RET_DK, RET_DK), F32), post)


def _retention(u, col0, mix_w, cos2, sin2, *, n_ctx):
    bsz, t, _ = u.shape
    heads = mix_w // RET_DK
    c0 = col0 // RET_DK
    log_gamma = jnp.log1p(-jnp.exp2(-5.0 - jnp.arange(heads, dtype=F32)))

    def spec(k):
        return pl.BlockSpec((None, t, RET_DK), lambda i, h, lg, k=k: (i, 0, c0 + k * heads + h))

    table = pl.BlockSpec((t, RET_DK), lambda i, h, lg: (0, 0))
    return pl.pallas_call(
        functools.partial(_ret_kernel, t_total=t, n_ctx=n_ctx),
        grid_spec=pltpu.PrefetchScalarGridSpec(
            num_scalar_prefetch=1, grid=(bsz, heads),
            in_specs=[spec(0), spec(1), spec(2), spec(3), table, table],
            out_specs=pl.BlockSpec((None, t, RET_DK), lambda i, h, lg: (i, 0, h)),
            scratch_shapes=[pltpu.VMEM((t, RET_DK), F32)] * 4),
        out_shape=jax.ShapeDtypeStruct((bsz, t, mix_w), BF16),
        compiler_params=_params("parallel", "arbitrary"), name="retention",
    )(log_gamma, u, u, u, u, cos2, sin2)


def _ssd_kernel(xs_ref, bm_ref, cm_ref, z_ref, dt_ref, dtb_ref, alog_ref, dsk_ref, nw_ref, sel_ref, ord_ref, o_ref,
                acc_f, acc_b, dts, las, *, t_total, n_ctx):
    c = SCAN_CHUNK
    rep = sel_ref.shape[1] // 2
    gw = rep * SSD_HD
    n_chunks, n_ctx_chunks = t_total // c, n_ctx // c
    dt_all = _softplus(dt_ref[...] + dtb_ref[...])
    dts[...] = dt_all
    las[...] = dt_all * (-jnp.exp(alog_ref[...]))

    ii = lax.broadcasted_iota(jnp.int32, (c, c), 0)
    jj = lax.broadcasted_iota(jnp.int32, (c, c), 1)
    lane_head = lax.broadcasted_iota(jnp.int32, (1, gw), 1) // SSD_HD

    def chunk(off, s_in, d):
        fwd = d == 0
        mask = (ii >= jj) if fwd else (jj >= ii)
        la_parts = _split_bf16(las[pl.ds(off, c), :], 3)
        cum = sum(_dot(ord_ref[d], p) for p in la_parts)
        cum_rows = sum(_dot_tn(p, ord_ref[1 - d]) for p in la_parts)
        sel = sel_ref[d]
        cum_e = sum(_dot(p, sel) for p in _split_bf16(cum, 3))
        dt_e = sum(_dot(p, sel) for p in _split_bf16(dts[pl.ds(off, c), :], 3))
        last_e = cum_e[c - 1:c, :] if fwd else cum_e[0:1, :]
        xs = xs_ref[pl.ds(off, c), :]
        vdt = xs * dt_e
        cm = cm_ref[pl.ds(off, c), :].astype(BF16)
        bm = bm_ref[pl.ds(off, c), :].astype(BF16)
        qk = _dot_nt(cm, bm)
        o = jnp.exp(cum_e) * _dot(cm, s_in.astype(BF16))
        for i in range(rep):
            col = d * rep + i
            diff = cum[:, col:col + 1] - cum_rows[col:col + 1, :]
            decay = jnp.where(mask, jnp.exp(jnp.where(mask, diff, 0.0)), 0.0)
            o = o + _dot((qk * decay).astype(BF16), jnp.where(lane_head == i, vdt, 0.0).astype(BF16))
        s_out = jnp.exp(last_e) * s_in + _dot_tn(bm, (vdt * jnp.exp(last_e - cum_e)).astype(BF16))
        return o, s_out

    def post(off):
        o = acc_f[pl.ds(off, c), :] + acc_b[pl.ds(off, c), :]
        y = (o + dsk_ref[...] * xs_ref[pl.ds(off, c), :]) * _silu(z_ref[pl.ds(off, c), :])
        y = y * lax.rsqrt(jnp.mean(y * y, axis=-1, keepdims=True) + EPS)
        o_ref[pl.ds(off, c), :] = (y * nw_ref[...]).astype(o_ref.dtype)

    _scan_both_ways(n_chunks, n_ctx_chunks, c, chunk, acc_f, acc_b, jnp.zeros((SSD_STATE, gw), F32), post)


def _ssd(u_z, z_col0, xact, u_dt, dt_bias, a_log, d_skip, norm_w, mix_w, *, n_ctx):
    bsz, t, _ = xact.shape
    heads = mix_w // SSD_HD
    rep = heads // SSD_GROUPS
    gw = rep * SSD_HD
    def per_group(v):
        lead = v.shape[:-1]
        v = v.reshape(*lead, 2, SSD_GROUPS, rep)
        return jnp.moveaxis(v, -2, 0).reshape(SSD_GROUPS, *lead, 2 * rep)
    dt_g = jnp.moveaxis(per_group(u_dt), 0, 1)
    dtb_g = per_group(dt_bias.reshape(1, 2 * heads))
    alog_g = per_group(a_log.reshape(1, 2 * heads))
    sel = np.zeros((2, 2 * rep, gw), np.float32)
    for d in range(2):
        for i in range(rep):
            sel[d, d * rep + i, i * SSD_HD:(i + 1) * SSD_HD] = 1.0
    dsk = jnp.repeat(d_skip, SSD_HD).reshape(1, mix_w)
    zc, bc, cc = z_col0 // gw, mix_w // SSD_STATE, mix_w // SSD_STATE + SSD_GROUPS
    small = lambda: pl.BlockSpec((None, 1, 2 * rep), lambda i, g: (g, 0, 0))
    return pl.pallas_call(
        functools.partial(_ssd_kernel, t_total=t, n_ctx=n_ctx),
        grid=(bsz, SSD_GROUPS),
        in_specs=[pl.BlockSpec((None, t, gw), lambda i, g: (i, 0, g)),
                  pl.BlockSpec((None, t, SSD_STATE), lambda i, g: (i, 0, bc + g)),
                  pl.BlockSpec((None, t, SSD_STATE), lambda i, g: (i, 0, cc + g)),
                  pl.BlockSpec((None, t, gw), lambda i, g: (i, 0, zc + g)),
                  pl.BlockSpec((None, None, t, 2 * rep), lambda i, g: (i, g, 0, 0)),
                  small(), small(),
                  pl.BlockSpec((1, gw), lambda i, g: (0, g)),
                  pl.BlockSpec((1, gw), lambda i, g: (0, g)),
                  pl.BlockSpec((2, 2 * rep, gw), lambda i, g: (0, 0, 0)),
                  pl.BlockSpec((2, SCAN_CHUNK, SCAN_CHUNK), lambda i, g: (0, 0, 0))],
        out_specs=pl.BlockSpec((None, t, gw), lambda i, g: (i, 0, g)),
        out_shape=jax.ShapeDtypeStruct((bsz, t, mix_w), BF16),
        scratch_shapes=[pltpu.VMEM((t, gw), F32), pltpu.VMEM((t, gw), F32),
                        pltpu.VMEM((t, 2 * rep), F32), pltpu.VMEM((t, 2 * rep), F32)],
        compiler_params=_params("parallel", "arbitrary"), name="ssd",
    )(xact, xact, xact, u_z, dt_g, dtb_g, alog_g, dsk, norm_w.reshape(1, mix_w), jnp.asarray(sel, BF16),
      _scan_order_tables(SCAN_CHUNK))


def _tile_scan(a, b, row, reverse):
    for s in (1, 2, 4):
        if reverse:
            ok, shift = row < SUBLANES - s, SUBLANES - s
        else:
            ok, shift = row >= s, s
        b = b + a * jnp.where(ok, pltpu.roll(b, shift, 0), 0.0)
        a = a * jnp.where(ok, pltpu.roll(a, shift, 0), 1.0)
    return a, b


def _lru_kernel(xc_ref, y_ref, w_ref, bias_ref, lam_ref, o_ref, acc, a_f, b_f, a_b, b_b, *, t_total, n_ctx):
    c = SCAN_CHUNK
    nb, bw = w_ref.shape[0], w_ref.shape[1]
    n_chunks, n_ctx_chunks = t_total // c, n_ctx // c
    n_tiles = c // SUBLANES
    log_sig = -_softplus(-lam_ref[...])
    acc[...] = jnp.zeros_like(acc)
    row = lax.broadcasted_iota(jnp.int32, (SUBLANES, bw), 0)

    def gates(off, d, dst_a, dst_b):
        for blk in range(nb):
            cols = slice(blk * bw, (blk + 1) * bw)
            x = xc_ref[pl.ds(off, c), cols]
            g = _dot(x.astype(BF16), w_ref[blk, :, d * 2 * bw:(d + 1) * 2 * bw]) \
                + bias_ref[blk, :, d * 2 * bw:(d + 1) * 2 * bw]
            log_a = LRU_C * _sigmoid(g[:, :bw]) * log_sig[d:d + 1, cols]
            a = jnp.exp(log_a)
            dst_a[:, cols] = a
            dst_b[:, cols] = jnp.sqrt(-jnp.tanh(log_a) * (a * a + 1.0)) * (_sigmoid(g[:, bw:]) * x)

    def step(s, carry):
        off_f = pl.multiple_of(s * c, c)
        off_b = pl.multiple_of(_bwd_chunk(s, n_chunks, n_ctx_chunks) * c, c)
        gates(off_f, 0, a_f, b_f)
        gates(off_b, 1, a_b, b_b)

        def tile(t, carry):
            h_f, h_b = carry
            r_f = pl.multiple_of(t * SUBLANES, SUBLANES)
            r_b = pl.multiple_of((n_tiles - 1 - t) * SUBLANES, SUBLANES)
            new_f, new_b = [], []
            for blk in range(nb):
                cols = slice(blk * bw, (blk + 1) * bw)
                a, b = _tile_scan(a_f[pl.ds(r_f, SUBLANES), cols], b_f[pl.ds(r_f, SUBLANES), cols], row, False)
                h = b + a * h_f[blk]
                acc[pl.ds(off_f + r_f, SUBLANES), cols] += h
                new_f.append(h[SUBLANES - 1:SUBLANES, :])
                a, b = _tile_scan(a_b[pl.ds(r_b, SUBLANES), cols], b_b[pl.ds(r_b, SUBLANES), cols], row, True)
                h = b + a * h_b[blk]
                acc[pl.ds(off_b + r_b, SUBLANES), cols] += h
                new_b.append(h[0:1, :])
            return tuple(new_f), tuple(new_b)

        return lax.fori_loop(0, n_tiles, tile, carry)

    zero = tuple(jnp.zeros((1, bw), F32) for _ in range(nb))
    lax.fori_loop(0, n_chunks, step, (zero, zero))

    def finish(s, _):
        off = pl.multiple_of(s * c, c)
        o_ref[pl.ds(off, c), :] = (acc[pl.ds(off, c), :] * _gelu_tanh(y_ref[pl.ds(off, c), :])).astype(o_ref.dtype)
        return 0

    lax.fori_loop(0, n_chunks, finish, 0)


def _rglru(xc, u_y, y_col0, wa, ba, wx, bx, lam, mix_w, *, n_ctx, blocks_per_step=4):
    bsz, t, _ = xc.shape
    bw = mix_w // LRU_BLOCKS
    gw = blocks_per_step * bw
    w = jnp.concatenate([wa[0], wx[0], wa[1], wx[1]], axis=-1).astype(BF16)
    blk = lambda v: v.reshape(2, LRU_BLOCKS, 1, bw)
    bias = jnp.concatenate([blk(ba)[0], blk(bx)[0], blk(ba)[1], blk(bx)[1]], axis=-1)
    yc = y_col0 // gw
    return pl.pallas_call(
        functools.partial(_lru_kernel, t_total=t, n_ctx=n_ctx),
        grid=(bsz, LRU_BLOCKS // blocks_per_step),
        in_specs=[pl.BlockSpec((None, t, gw), lambda i, j: (i, 0, j)),
                  pl.BlockSpec((None, t, gw), lambda i, j: (i, 0, yc + j)),
                  pl.BlockSpec((blocks_per_step, bw, 4 * bw), lambda i, j: (j, 0, 0)),
                  pl.BlockSpec((blocks_per_step, 1, 4 * bw), lambda i, j: (j, 0, 0)),
                  pl.BlockSpec((2, gw), lambda i, j: (0, j))],
        out_specs=pl.BlockSpec((None, t, gw), lambda i, j: (i, 0, j)),
        out_shape=jax.ShapeDtypeStruct((bsz, t, mix_w), BF16),
        scratch_shapes=[pltpu.VMEM((t, gw), F32)] + [pltpu.VMEM((SCAN_CHUNK, gw), F32)] * 4,
        compiler_params=_params("parallel", "arbitrary"), name="rglru",
    )(xc, u_y, w, bias, lam)


def _hg_tables():
    c = HG_CHUNK
    idx = np.arange(c)
    dif = np.zeros((2, HG_LEVELS, c, c), np.float32)
    isq = np.zeros((2, HG_LEVELS, c, HG_DK), np.float32)
    msk = np.zeros((2, HG_LEVELS, c, c), np.float32)
    for lvl in range(HG_LEVELS):
        half = 1 << lvl
        mid = (idx // (2 * half)) * 2 * half + half
        late = idx >= mid
        same = (idx[:, None] // (2 * half)) == (idx[None, :] // (2 * half))
        for d, boundary, is_query in ((0, mid - 1, late), (1, mid, ~late)):
            sign = np.where(is_query, 1.0, -1.0)
            dif[d, lvl, idx, idx] += sign
            dif[d, lvl, idx, boundary] -= sign
            isq[d, lvl] = is_query[:, None]
        msk[0, lvl] = same & late[:, None] & ~late[None, :]
        msk[1, lvl] = same & ~late[:, None] & late[None, :]
    lower = (idx[None, :] <= idx[:, None]).astype(np.float32)
    order = np.stack([lower, lower.T])
    rep3 = lambda v: jnp.asarray(np.concatenate([v, v, v], axis=-1), BF16)
    return rep3(order), rep3(dif.reshape(2, HG_LEVELS * c, c)), jnp.asarray(isq), jnp.asarray(msk)


def _hg_kernel(q_ref, f0_ref, f1_ref, v_ref, g_ref, lb_ref, nw_ref, ord_ref, dif_ref, isq_ref, msk_ref,
               o_ref, acc_f, acc_b, *, t_total, n_ctx):
    c = HG_CHUNK
    n_chunks, n_ctx_chunks = t_total // c, n_ctx // c
    eye = lax.broadcasted_iota(jnp.int32, (c, c), 0) == lax.broadcasted_iota(jnp.int32, (c, c), 1)

    def chunk(off, st, d):
        fwd = d == 0
        lb = lb_ref[d:d + 1, :]
        f = (f0_ref if fwd else f1_ref)[pl.ds(off, c), :]
        log_f = jnp.log(lb + (1.0 - lb) * _sigmoid(f))
        k = (1.0 - lb) * _sigmoid(-f)
        q = _silu(q_ref[pl.ds(off, c), :])
        v = v_ref[pl.ds(off, c), :].astype(BF16)
        cum = _dot(ord_ref[d], jnp.concatenate(_split_bf16(log_f, 3), axis=0))
        expo = _dot(dif_ref[d], jnp.concatenate(_split_bf16(cum, 3), axis=0))
        q_minus_k = q - k
        scores = jnp.where(eye, _dot_nt(q.astype(BF16), k.astype(BF16)), 0.0)
        for lvl in range(HG_LEVELS):
            rows = slice(lvl * c, (lvl + 1) * c)
            y = ((k + isq_ref[d, lvl] * q_minus_k) * jnp.exp(expo[rows, :])).astype(BF16)
            scores = scores + _dot_nt(y, y) * msk_ref[d, lvl]
        last = cum[c - 1:c, :] if fwd else cum[0:1, :]
        o = _dot(scores.astype(BF16), v) + _dot_nt((q * jnp.exp(cum)).astype(BF16), st.astype(BF16))
        st = st * jnp.exp(last) + _dot_tn(v, (k * jnp.exp(last - cum)).astype(BF16))
        return o, st

    def post(off):
        o = acc_f[pl.ds(off, c), :] + acc_b[pl.ds(off, c), :]
        y = o * lax.rsqrt(jnp.mean(o * o, axis=-1, keepdims=True) + EPS) * nw_ref[...]
        o_ref[pl.ds(off, c), :] = (y * _silu(g_ref[pl.ds(off, c), :])).astype(o_ref.dtype)

    _scan_both_ways(n_chunks, n_ctx_chunks, c, chunk, acc_f, acc_b, jnp.zeros((HG_DK, HG_DK), F32), post)


def _hgrn2(u, col0, mix_w, lower, norm_w, *, n_ctx):
    bsz, t, _ = u.shape
    heads = mix_w // HG_DK
    c0 = col0 // HG_DK
    tables = _hg_tables()

    def spec(k):
        return pl.BlockSpec((None, t, HG_DK), lambda i, h, k=k: (i, 0, c0 + k * heads + h))

    def table(v):
        return pl.BlockSpec(v.shape, lambda i, h, nd=v.ndim: (0,) * nd)

    return pl.pallas_call(
        functools.partial(_hg_kernel, t_total=t, n_ctx=n_ctx),
        grid=(bsz, heads),
        in_specs=[spec(0), spec(1), spec(2), spec(3), spec(4),
                  pl.BlockSpec((2, HG_DK), lambda i, h: (0, h)),
                  pl.BlockSpec((1, HG_DK), lambda i, h: (0, h))] + [table(v) for v in tables],
        out_specs=pl.BlockSpec((None, t, HG_DK), lambda i, h: (i, 0, h)),
        out_shape=jax.ShapeDtypeStruct((bsz, t, mix_w), BF16),
        scratch_shapes=[pltpu.VMEM((t, HG_DK), F32)] * 2,
        compiler_params=_params("parallel", "arbitrary"), name="hgrn2",
    )(u, u, u, u, u, lower, norm_w.reshape(1, mix_w), *tables)


def _ada_kernel(c_ref, w_ref, b_ref, o_ref):
    o_ref[...] = _dot(_silu(c_ref[...]).astype(BF16), w_ref[...].astype(BF16)) + b_ref[...]


def _ada(cond, ada_w, ada_b, tn=1024):
    r, d = cond.shape
    depth, _, n = ada_w.shape
    return pl.pallas_call(
        _ada_kernel,
        grid=(depth, n // tn),
        in_specs=[pl.BlockSpec((r, d), lambda l, j: (0, 0)),
                  pl.BlockSpec((None, d, tn), lambda l, j: (l, 0, j)),
                  pl.BlockSpec((None, 1, tn), lambda l, j: (l, 0, j))],
        out_specs=pl.BlockSpec((None, r, tn), lambda l, j: (l, 0, j)),
        out_shape=jax.ShapeDtypeStruct((depth, r, n), F32),
        compiler_params=_params("parallel", "parallel"), name="ada",
    )(cond, ada_w, ada_b.reshape(depth, 1, n))


def _rope_tables(n_ctx, n_lat):
    r, col = jnp.meshgrid(jnp.arange(n_lat // GRID_W), jnp.arange(GRID_W), indexing='ij')
    n_freq = RET_DK // 4
    freqs = ROPE_BASE ** (-jnp.arange(n_freq, dtype=F32) / n_freq)
    ang = jnp.concatenate([r.reshape(-1, 1) * freqs, col.reshape(-1, 1) * freqs], axis=-1)
    cos, sin = jnp.cos(ang), jnp.sin(ang)
    cos2 = jnp.concatenate([jnp.ones((n_ctx, RET_DK), F32), jnp.concatenate([cos, cos], axis=-1)], axis=0)
    sin2 = jnp.concatenate([jnp.zeros((n_ctx, RET_DK), F32), jnp.concatenate([-sin, sin], axis=-1)], axis=0)
    return cos2, sin2


def kernel(x, c, ctx, c_ctx, ada_w, ada_b, norm1_w, norm2_w, w_in, ssd_conv_w, ssd_conv_b, ssd_a_log, ssd_dt_bias, ssd_d, ssd_norm_w, lru_conv_w, lru_conv_b, lru_wa, lru_ba, lru_wx, lru_bx, lru_lambda, hg_lb_logits, hg_norm_w, w_branch, w_out, ffn_wgu, ffn_w2, router_w, moe_wgu, moe_w2, final_norm_w):
    bsz, n_lat, d = x.shape
    n_ctx = ctx.shape[1]
    t = n_ctx + n_lat
    depth = w_in.shape[0]
    mix_w = d // 2
    ssd_heads = mix_w // SSD_HD
    xbc_w = mix_w + 2 * SSD_GROUPS * SSD_STATE
    ret_w, ssd_w, dt_w, lru_w, hg_w, gate_w = 4 * mix_w, mix_w + xbc_w, 2 * ssd_heads, 2 * mix_w, 5 * mix_w, N_BRANCH * d
    dt_col = ret_w + ssd_w
    w_lo = w_in[:, :, :dt_col + LANES].astype(BF16)
    w_hi = w_in[:, :, dt_col + dt_w:].astype(BF16)
    w_branch_bf, w_out_bf, ffn_w2_bf = w_branch.astype(BF16), w_out.astype(BF16), ffn_w2.astype(BF16)

    cos2, sin2 = _rope_tables(n_ctx, n_lat)
    p_lb = jax.nn.softmax(hg_lb_logits.astype(F32), axis=1)
    lower_bounds = jnp.cumsum(p_lb, axis=1) - p_lb[:, :1]

    pad = (-(1 + bsz)) % SUBLANES
    cond = jnp.concatenate([c_ctx[None], c, jnp.zeros((pad, d), F32)], axis=0)
    mods = _ada(cond, ada_w, ada_b).reshape(depth, -1, N_MOD, d)

    h = jnp.concatenate([ctx.astype(x.dtype), x], axis=1)
    for layer in range(depth):
        m_ctx = jnp.broadcast_to(mods[layer, 0], (bsz, N_MOD, d))
        m_lat = mods[layer, 1:1 + bsz]
        mod_rows = jnp.stack([m_ctx, m_lat], axis=1)
        mod_cols = jnp.stack([m_ctx, m_lat], axis=2)

        xn = _norm_call(h, norm1_w[layer], mod_rows, shift=0, scale=1, n_ctx=n_ctx).reshape(bsz * t, d)
        u_ret = _matmul(xn, w_lo, layer, 0, ret_w).reshape(bsz, t, ret_w)
        u_ssd = _matmul(xn, w_lo, layer, ret_w, ssd_w).reshape(bsz, t, ssd_w)
        u_dt = _matmul(xn, w_lo, layer, dt_col, LANES)[:, :dt_w].reshape(bsz, t, dt_w)
        u_lru = _matmul(xn, w_hi, layer, 0, lru_w).reshape(bsz, t, lru_w)
        u_hg = _matmul(xn, w_hi, layer, lru_w, hg_w).reshape(bsz, t, hg_w)
        u_gates = _matmul(xn, w_hi, layer, lru_w + hg_w, gate_w)

        b_ret = _retention(u_ret, 0, mix_w, cos2, sin2, n_ctx=n_ctx)
        xact = _dwconv(u_ssd, mix_w, xbc_w, ssd_conv_w[layer], ssd_conv_b[layer], n_ctx=n_ctx, act=True)
        b_ssd = _ssd(u_ssd, 0, xact, u_dt, ssd_dt_bias[layer], ssd_a_log[layer], ssd_d[layer],
                     ssd_norm_w[layer], mix_w, n_ctx=n_ctx)
        xc = _dwconv(u_lru, 0, mix_w, lru_conv_w[layer], lru_conv_b[layer], n_ctx=n_ctx, act=False)
        b_lru = _rglru(xc, u_lru, mix_w, lru_wa[layer], lru_ba[layer], lru_wx[layer], lru_bx[layer],
                       lru_lambda[layer], mix_w, n_ctx=n_ctx)
        b_hg = _hgrn2(u_hg, 0, mix_w, lower_bounds[:, layer], hg_norm_w[layer], n_ctx=n_ctx)

        branches = [v.reshape(bsz * t, mix_w) for v in (b_ret, b_ssd, b_lru, b_hg)]
        mixed = _merge(branches, u_gates, w_branch_bf, layer, tm=768)
        h2 = _matmul_residual(mixed, w_out_bf, layer, h.reshape(bsz * t, d), mod_cols, 2, t_total=t, n_ctx=n_ctx)

        if layer % 2 == 0:
            xn = _norm_call(h2.reshape(bsz, t, d), norm2_w[layer], mod_rows, shift=3, scale=4, n_ctx=n_ctx)
            act = _swiglu_up(xn.reshape(bsz * t, d), ffn_wgu, layer // 2)
            h = _matmul_residual(act, ffn_w2_bf, layer // 2, h2, mod_cols, 5,
                                 t_total=t, n_ctx=n_ctx).reshape(bsz, t, d)
        else:
            xn, gates = _norm_call(h2.reshape(bsz, t, d), norm2_w[layer], mod_rows, shift=3, scale=4, n_ctx=n_ctx,
                                   router_w=router_w[layer // 2], out_dtype=F32)
            h = _moe_block(xn, gates, h2.reshape(bsz, t, d), moe_wgu, moe_w2, layer // 2, mod_rows, 5, n_ctx=n_ctx)
    return _final_norm_call(h, final_norm_w, n_ctx)
```

```python
import functools
import math

import numpy as np
import jax
import jax.numpy as jnp
from jax import lax
from jax.experimental import pallas as pl
from jax.experimental.pallas import tpu as pltpu

F32 = jnp.float32
BF16 = jnp.bfloat16
HI = lax.Precision.HIGHEST

EPS = 1e-6
GRID_W = 64
ROPE_BASE = 10000.0
RET_DK = 128
SSD_HD = 64
SSD_GROUPS = 4
SSD_STATE = 128
CONV_W = 4
LRU_BLOCKS = 8
LRU_C = 8.0
HG_DK = 128
N_BRANCH = 4
N_EXPERTS = 8
N_MOD = 6

LANES = 128
SUBLANES = 8
VMEM_LIMIT = 52 * 1024 * 1024

SCAN_CHUNK = 256
SSD_CHUNK = 256
HG_CHUNK = 128
HG_LEVELS = 7


def _params(*sem):
    return pltpu.CompilerParams(dimension_semantics=sem, vmem_limit_bytes=VMEM_LIMIT)


def _dot(a, b, prec=None):
    return jnp.dot(a, b, preferred_element_type=F32, precision=prec)


def _dot_nt(a, b, prec=None):
    return lax.dot_general(a, b, (((1,), (1,)), ((), ())), preferred_element_type=F32, precision=prec)


def _dot_tn(a, b, prec=None):
    return lax.dot_general(a, b, (((0,), (0,)), ((), ())), preferred_element_type=F32, precision=prec)


def _sigmoid(x):
    return 1.0 / (1.0 + jnp.exp(-x))


def _silu(x):
    return x * _sigmoid(x)


def _softplus(x):
    return jnp.maximum(x, 0.0) + jnp.log1p(jnp.exp(-jnp.abs(x)))


def _gelu_tanh(x):
    return 0.5 * x * (1.0 + jnp.tanh(math.sqrt(2.0 / math.pi) * (x + 0.044715 * (x * x * x))))


def _bwd_chunk(s, n_chunks, n_ctx_chunks):
    return jnp.where(s < n_ctx_chunks, n_ctx_chunks - 1 - s, n_chunks - 1 - (s - n_ctx_chunks))


def _scan_both_ways(n_chunks, n_ctx_chunks, c, chunk_fn, acc_f, acc_b, zero_state, post_fn):
    def body(s, carry):
        st_f, st_b = carry
        off_f = pl.multiple_of(s * c, c)
        off_b = pl.multiple_of(_bwd_chunk(s, n_chunks, n_ctx_chunks) * c, c)
        o_f, st_f = chunk_fn(off_f, st_f, 0)
        o_b, st_b = chunk_fn(off_b, st_b, 1)
        acc_f[pl.ds(off_f, c), :] = o_f
        acc_b[pl.ds(off_b, c), :] = o_b
        return st_f, st_b

    lax.fori_loop(0, n_chunks, body, (zero_state, zero_state), unroll=3)

    def finish(s, _):
        post_fn(pl.multiple_of(s * c, c))
        return 0

    lax.fori_loop(0, n_chunks, finish, 0)


def _scan_order_tables(c):
    idx = np.arange(c)
    lower = (idx[None, :] <= idx[:, None]).astype(np.float32)
    return jnp.asarray(np.stack([lower, lower.T]), BF16)


def _split_bf16(x, parts):
    out = []
    for _ in range(parts):
        p = x.astype(BF16)
        out.append(p)
        x = x - p.astype(F32)
    return out


def _norm_kernel(h_ref, w_ref, mod_ref, o_ref, *, shift, scale):
    x = h_ref[...]
    y = x * lax.rsqrt(jnp.mean(x * x, axis=-1, keepdims=True) + EPS) * w_ref[...]
    o_ref[...] = (y * (1.0 + mod_ref[scale:scale + 1, :]) + mod_ref[shift:shift + 1, :]).astype(o_ref.dtype)


def _norm_router_kernel(h_ref, w_ref, mod_ref, rw_ref, o_ref, gate_ref, *, shift, scale):
    x = h_ref[...]
    y = x * lax.rsqrt(jnp.mean(x * x, axis=-1, keepdims=True) + EPS) * w_ref[...]
    xn = y * (1.0 + mod_ref[scale:scale + 1, :]) + mod_ref[shift:shift + 1, :]
    o_ref[...] = xn.astype(o_ref.dtype)
    logits = _dot(xn.astype(BF16), rw_ref[...])
    lane = lax.broadcasted_iota(jnp.int32, logits.shape, 1)
    neg = jnp.float32(-jnp.inf)
    logits = jnp.where(lane < N_EXPERTS, logits, neg)
    v1 = jnp.max(logits, axis=-1, keepdims=True)
    i1 = jnp.min(jnp.where(logits == v1, lane, LANES), axis=-1, keepdims=True)
    rest = jnp.where(lane == i1, neg, logits)
    v2 = jnp.max(rest, axis=-1, keepdims=True)
    i2 = jnp.min(jnp.where(rest == v2, lane, LANES), axis=-1, keepdims=True)
    e2 = jnp.exp(v2 - v1)
    w1 = 1.0 / (1.0 + e2)
    w2 = e2 / (1.0 + e2)
    gate_ref[...] = (jnp.where(lane == i1, w1, 0.0) + jnp.where(lane == i2, w2, 0.0)
                     + jnp.where(lane == N_EXPERTS, i1.astype(F32), 0.0)
                     + jnp.where(lane == N_EXPERTS + 1, i2.astype(F32), 0.0))


def _norm_call(h, w, mod, *, shift, scale, n_ctx, router_w=None, out_dtype=BF16):
    b, t, d = h.shape
    tr = n_ctx
    grid = (b, t // tr)
    h_spec = pl.BlockSpec((None, tr, d), lambda i, j: (i, j, 0))
    w_spec = pl.BlockSpec((1, d), lambda i, j: (0, 0))
    mod_spec = pl.BlockSpec((None, None, N_MOD, d), lambda i, j: (i, jnp.minimum(j, 1), 0, 0))
    if router_w is None:
        return pl.pallas_call(
            functools.partial(_norm_kernel, shift=shift, scale=scale),
            grid=grid, in_specs=[h_spec, w_spec, mod_spec], out_specs=h_spec,
            out_shape=jax.ShapeDtypeStruct(h.shape, out_dtype),
            compiler_params=_params("parallel", "parallel"), name="norm_mod",
        )(h, w.reshape(1, d), mod)
    rw = jnp.zeros((d, LANES), BF16).at[:, :N_EXPERTS].set(router_w.astype(BF16))
    return pl.pallas_call(
        functools.partial(_norm_router_kernel, shift=shift, scale=scale),
        grid=grid,
        in_specs=[h_spec, w_spec, mod_spec, pl.BlockSpec((d, LANES), lambda i, j: (0, 0))],
        out_specs=[h_spec, pl.BlockSpec((None, tr, LANES), lambda i, j: (i, j, 0))],
        out_shape=[jax.ShapeDtypeStruct(h.shape, out_dtype), jax.ShapeDtypeStruct((b, t, LANES), F32)],
        compiler_params=_params("parallel", "parallel"), name="norm_mod_router",
    )(h, w.reshape(1, d), mod, rw)


def _final_norm_kernel(h_ref, w_ref, o_ref):
    x = h_ref[...]
    o_ref[...] = x * lax.rsqrt(jnp.mean(x * x, axis=-1, keepdims=True) + EPS) * w_ref[...]


def _final_norm_call(h, w, n_ctx):
    b, t, d = h.shape
    tr = n_ctx
    return pl.pallas_call(
        _final_norm_kernel,
        grid=(b, (t - n_ctx) // tr),
        in_specs=[pl.BlockSpec((None, tr, d), lambda i, j: (i, j + 1, 0)), pl.BlockSpec((1, d), lambda i, j: (0, 0))],
        out_specs=pl.BlockSpec((None, tr, d), lambda i, j: (i, j, 0)),
        out_shape=jax.ShapeDtypeStruct((b, t - n_ctx, d), F32),
        compiler_params=_params("parallel", "parallel"), name="final_norm",
    )(h, w.reshape(1, d))


def _pick(n, prefs):
    for p in prefs:
        if n % p == 0:
            return p
    return n


def _mm_kernel(a_ref, w_ref, o_ref, acc_ref, *, nk):
    k = pl.program_id(2)
    part = _dot(a_ref[...], w_ref[...])
    if nk == 1:
        o_ref[...] = part.astype(o_ref.dtype)
        return

    @pl.when(k == 0)
    def _():
        acc_ref[...] = part

    @pl.when(k > 0)
    def _():
        acc_ref[...] += part

    @pl.when(k == nk - 1)
    def _():
        o_ref[...] = acc_ref[...].astype(o_ref.dtype)


def _matmul(a, w, layer, col0, n, out_dtype=F32, tm=None, tn=None, tk=None):
    m, kd = a.shape
    tm = tm or _pick(m, (1152, 1024, 768, 512, 256))
    tn = tn or _pick(math.gcd(n, col0) if col0 else n, (1024, 512, 256, 128))
    tk = tk or (kd if kd <= 2048 else _pick(kd, (512, 256, 128)))
    nk = kd // tk
    c0 = col0 // tn
    return pl.pallas_call(
        functools.partial(_mm_kernel, nk=nk),
        grid=(n // tn, m // tm, nk),
        in_specs=[pl.BlockSpec((tm, tk), lambda j, i, k: (i, k)),
                  pl.BlockSpec((None, tk, tn), lambda j, i, k: (layer, k, c0 + j))],
        out_specs=pl.BlockSpec((tm, tn), lambda j, i, k: (i, j)),
        out_shape=jax.ShapeDtypeStruct((m, n), out_dtype),
        scratch_shapes=[pltpu.VMEM((tm, tn), F32)],
        compiler_params=_params("parallel", "parallel", "arbitrary"), name="matmul",
    )(a, w)


def _row_is_ctx(tm, t_total, n_ctx):
    row0 = pl.program_id(1) * tm
    pos = (row0 + lax.broadcasted_iota(jnp.int32, (tm, 1), 0)) % t_total
    return pos < n_ctx


def _mm_res_kernel(a_ref, w_ref, res_ref, mod_ref, o_ref, acc_ref, *, nk, tm, t_total, n_ctx):
    k = pl.program_id(2)
    part = _dot(a_ref[...], w_ref[...])

    def finish(total):
        mod = jnp.where(_row_is_ctx(tm, t_total, n_ctx), mod_ref[0:1, :], mod_ref[1:2, :])
        o_ref[...] = res_ref[...] + mod * total

    if nk == 1:
        finish(part)
        return

    @pl.when(k == 0)
    def _():
        acc_ref[...] = part

    @pl.when(k > 0)
    def _():
        acc_ref[...] += part

    @pl.when(k == nk - 1)
    def _():
        finish(acc_ref[...])


def _matmul_residual(a, w, layer, res, mod, mod_idx, *, t_total, n_ctx, tm=None, tn=None, tk=None):
    m, kd = a.shape
    n = w.shape[2]
    tm = tm or _pick(t_total, (1152, 768, 256))
    tn = tn or _pick(n, (1024, 512, 256, 128))
    tk = tk or (kd if kd <= 2048 else _pick(kd, (512, 256, 128)))
    nk = kd // tk
    per_b = t_total // tm
    return pl.pallas_call(
        functools.partial(_mm_res_kernel, nk=nk, tm=tm, t_total=t_total, n_ctx=n_ctx),
        grid=(n // tn, m // tm, nk),
        in_specs=[pl.BlockSpec((tm, tk), lambda j, i, k: (i, k)),
                  pl.BlockSpec((None, tk, tn), lambda j, i, k: (layer, k, j)),
                  pl.BlockSpec((tm, tn), lambda j, i, k: (i, j)),
                  pl.BlockSpec((None, None, 2, tn), lambda j, i, k: (i // per_b, mod_idx, 0, j))],
        out_specs=pl.BlockSpec((tm, tn), lambda j, i, k: (i, j)),
        out_shape=jax.ShapeDtypeStruct((m, n), F32),
        scratch_shapes=[pltpu.VMEM((tm, tn), F32)],
        compiler_params=_params("parallel", "parallel", "arbitrary"), name="matmul_residual",
    )(a, w, res, mod)


def _swiglu_kernel(a_ref, wg_ref, wu_ref, o_ref, wg_bf, wu_bf):
    @pl.when(pl.program_id(1) == 0)
    def _():
        wg_bf[...] = wg_ref[...].astype(BF16)
        wu_bf[...] = wu_ref[...].astype(BF16)

    a = a_ref[...]
    o_ref[...] = (_silu(_dot(a, wg_bf[...])) * _dot(a, wu_bf[...])).astype(o_ref.dtype)


def _swiglu_up(a, wgu, layer, tm=None, tn=None):
    m, kd = a.shape
    f = wgu.shape[2] // 2
    tm = tm or _pick(m, (1152, 1024, 768, 512, 256))
    tn = tn or _pick(f, (512, 256, 128))
    nf = f // tn
    return pl.pallas_call(
        _swiglu_kernel,
        grid=(nf, m // tm),
        in_specs=[pl.BlockSpec((tm, kd), lambda j, i: (i, 0)),
                  pl.BlockSpec((None, kd, tn), lambda j, i: (layer, 0, j)),
                  pl.BlockSpec((None, kd, tn), lambda j, i: (layer, 0, j + nf))],
        out_specs=pl.BlockSpec((tm, tn), lambda j, i: (i, j)),
        out_shape=jax.ShapeDtypeStruct((m, f), BF16),
        scratch_shapes=[pltpu.VMEM((kd, tn), BF16)] * 2,
        compiler_params=_params("arbitrary", "arbitrary"), name="swiglu_up",
    )(a, wgu, wgu)


MOE_TM = 512
GATHER_ROWS = 256


def _route(gates, n_tok):
    tm = MOE_TM
    n_slot = 2 * n_tok
    n_tiles = n_slot // tm + N_EXPERTS
    i32 = jnp.int32
    eid = gates[:, N_EXPERTS:N_EXPERTS + 2].astype(i32).reshape(n_slot)
    order = jnp.argsort(eid, stable=True).astype(i32)
    rank = jnp.argsort(order).astype(i32)
    onehot = eid[:, None] == jnp.arange(N_EXPERTS, dtype=i32)[None, :]
    counts = jnp.sum(onehot, axis=0, dtype=i32)
    padded = (counts + tm - 1) // tm * tm
    ends = jnp.cumsum(padded)
    group_end = jnp.cumsum(counts)
    shift = (ends - padded) - (group_end - counts)
    dest = (rank + jnp.sum(jnp.where(onehot, shift[None, :], 0), axis=1, dtype=i32)).reshape(n_tok, 2)
    tile_start = jnp.arange(n_tiles, dtype=i32) * tm
    tile_expert = jnp.minimum(jnp.sum(tile_start[:, None] >= ends[None, :], axis=1, dtype=i32), N_EXPERTS - 1)
    row_expert = jnp.repeat(tile_expert, tm)
    pos = jnp.arange(n_tiles * tm, dtype=i32) - shift[row_expert]
    row_src = jnp.where(pos < group_end[row_expert], order[jnp.clip(pos, 0, n_slot - 1)] // 2, 0)
    tile_first = jnp.concatenate([jnp.ones((1,), i32), (tile_expert[1:] != tile_expert[:-1]).astype(i32)])
    return row_src, dest, tile_expert, tile_first, (ends[-1:] // tm).astype(i32)


def _gather_kernel(idx_ref, src_ref, o_ref, sem, *, rows):
    base = pl.program_id(0) * rows

    def row_copy(r, src_row):
        return pltpu.make_async_copy(src_ref.at[pl.ds(src_row, 1), :], o_ref.at[pl.ds(r, 1), :], sem)

    def issue(r, carry):
        row_copy(r, idx_ref[base + r]).start()
        return carry

    def drain(r, carry):
        row_copy(r, 0).wait()
        return carry

    lax.fori_loop(0, rows, issue, 0, unroll=8)
    lax.fori_loop(0, rows, drain, 0, unroll=8)


def _gather_rows(src, idx):
    p, w = idx.shape[0], src.shape[1]
    rows = GATHER_ROWS
    return pl.pallas_call(
        functools.partial(_gather_kernel, rows=rows),
        grid_spec=pltpu.PrefetchScalarGridSpec(
            num_scalar_prefetch=1, grid=(p // rows,),
            in_specs=[pl.BlockSpec(memory_space=pl.ANY)],
            out_specs=pl.BlockSpec((rows, w), lambda i, idx: (i, 0)),
            scratch_shapes=[pltpu.SemaphoreType.DMA(())]),
        out_shape=jax.ShapeDtypeStruct((p, w), src.dtype),
        compiler_params=_params("arbitrary"), name="gather_rows",
    )(idx, src)


def _moe_up_kernel(te_ref, tf_ref, nu_ref, a_ref, wg_ref, wu_ref, o_ref, wg_bf, wu_bf):
    i = pl.program_id(1)

    @pl.when(tf_ref[i] == 1)
    def _():
        wg_bf[...] = wg_ref[...].astype(BF16)
        wu_bf[...] = wu_ref[...].astype(BF16)

    @pl.when(i < nu_ref[0])
    def _():
        a = a_ref[...].astype(BF16)
        o_ref[...] = (_silu(_dot(a, wg_bf[...])) * _dot(a, wu_bf[...])).astype(o_ref.dtype)

    @pl.when(i >= nu_ref[0])
    def _():
        o_ref[...] = jnp.zeros_like(o_ref)


def _moe_up(xg, wgu, layer, tile_expert, tile_first, n_used, tn=512):
    p, kd = xg.shape
    f = wgu.shape[3] // 2
    nf = f // tn
    tm = MOE_TM
    return pl.pallas_call(
        _moe_up_kernel,
        grid_spec=pltpu.PrefetchScalarGridSpec(
            num_scalar_prefetch=3, grid=(nf, p // tm),
            in_specs=[pl.BlockSpec((tm, kd), lambda j, i, te, tf, nu: (i, 0)),
                      pl.BlockSpec((None, None, kd, tn), lambda j, i, te, tf, nu: (layer, te[i], 0, j)),
                      pl.BlockSpec((None, None, kd, tn), lambda j, i, te, tf, nu: (layer, te[i], 0, j + nf))],
            out_specs=pl.BlockSpec((tm, tn), lambda j, i, te, tf, nu: (i, j)),
            scratch_shapes=[pltpu.VMEM((kd, tn), BF16)] * 2),
        out_shape=jax.ShapeDtypeStruct((p, f), BF16),
        compiler_params=_params("arbitrary", "arbitrary"), name="moe_up",
    )(tile_expert, tile_first, n_used, xg, wgu, wgu)


def _moe_down_kernel(te_ref, tf_ref, nu_ref, a_ref, w_ref, o_ref, w_bf):
    i = pl.program_id(1)

    @pl.when(tf_ref[i] == 1)
    def _():
        w_bf[...] = w_ref[...].astype(BF16)

    @pl.when(i < nu_ref[0])
    def _():
        o_ref[...] = _dot(a_ref[...], w_bf[...])

    @pl.when(i >= nu_ref[0])
    def _():
        o_ref[...] = jnp.zeros_like(o_ref)


def _moe_down(act, w2, layer, tile_expert, tile_first, n_used, tn=512):
    p, f = act.shape
    d = w2.shape[3]
    tm = MOE_TM
    return pl.pallas_call(
        _moe_down_kernel,
        grid_spec=pltpu.PrefetchScalarGridSpec(
            num_scalar_prefetch=3, grid=(d // tn, p // tm),
            in_specs=[pl.BlockSpec((tm, f), lambda j, i, te, tf, nu: (i, 0)),
                      pl.BlockSpec((None, None, f, tn), lambda j, i, te, tf, nu: (layer, te[i], 0, j))],
            out_specs=pl.BlockSpec((tm, tn), lambda j, i, te, tf, nu: (i, j)),
            scratch_shapes=[pltpu.VMEM((f, tn), BF16)]),
        out_shape=jax.ShapeDtypeStruct((p, d), F32),
        compiler_params=_params("arbitrary", "arbitrary"), name="moe_down",
    )(tile_expert, tile_first, n_used, act, w2)


def _moe_combine_kernel(y0_ref, y1_ref, gate_ref, res_ref, mod_ref, o_ref, *, mod_idx):
    gates = gate_ref[...]
    lane = lax.broadcasted_iota(jnp.int32, gates.shape, 1)
    i0 = gates[:, N_EXPERTS:N_EXPERTS + 1].astype(jnp.int32)
    i1 = gates[:, N_EXPERTS + 1:N_EXPERTS + 2].astype(jnp.int32)
    w0 = jnp.sum(jnp.where(lane == i0, gates, 0.0), axis=-1, keepdims=True)
    w1 = jnp.sum(jnp.where(lane == i1, gates, 0.0), axis=-1, keepdims=True)
    o_ref[...] = res_ref[...] + mod_ref[mod_idx:mod_idx + 1, :] * (w0 * y0_ref[...] + w1 * y1_ref[...])


def _moe_combine(y0, y1, gates, res, mod, mod_idx, *, n_ctx):
    b, t, d = res.shape
    tr = n_ctx
    spec = pl.BlockSpec((None, tr, d), lambda i, j: (i, j, 0))
    return pl.pallas_call(
        functools.partial(_moe_combine_kernel, mod_idx=mod_idx),
        grid=(b, t // tr),
        in_specs=[spec, spec, pl.BlockSpec((None, tr, LANES), lambda i, j: (i, j, 0)), spec,
                  pl.BlockSpec((None, None, N_MOD, d), lambda i, j: (i, jnp.minimum(j, 1), 0, 0))],
        out_specs=spec,
        out_shape=jax.ShapeDtypeStruct((b, t, d), F32),
        compiler_params=_params("parallel", "parallel"), name="moe_combine",
    )(y0, y1, gates, res, mod)


def _moe_block(xn, gates, res, wgu, w2, layer, mod, mod_idx, *, n_ctx):
    b, t, d = xn.shape
    m = b * t
    row_src, dest, tile_expert, tile_first, n_used = _route(gates.reshape(m, LANES), m)
    xg = _gather_rows(xn.reshape(m, d), row_src)
    act = _moe_up(xg, wgu, layer, tile_expert, tile_first, n_used)
    yg = _moe_down(act, w2, layer, tile_expert, tile_first, n_used)
    y0 = _gather_rows(yg, dest[:, 0]).reshape(b, t, d)
    y1 = _gather_rows(yg, dest[:, 1]).reshape(b, t, d)
    return _moe_combine(y0, y1, gates, res, mod, mod_idx, n_ctx=n_ctx)


def _merge_kernel(b0, b1, b2, b3, w_ref, g0, g1, g2, g3, o_ref):
    acc = None
    for n, (br, g) in enumerate(((b0, g0), (b1, g1), (b2, g2), (b3, g3))):
        term = _sigmoid(g[...]) * _dot(br[...], w_ref[n])
        acc = term if acc is None else acc + term
    o_ref[...] = acc.astype(o_ref.dtype)


def _merge(branches, gates, wb, layer, tm=None, tn=None):
    m, kd = branches[0].shape
    d = wb.shape[-1]
    tm = tm or _pick(m, (1152, 1024, 768, 512, 256))
    tn = tn or _pick(d, (512, 256, 128))
    nd = d // tn
    br_spec = pl.BlockSpec((tm, kd), lambda j, i: (i, 0))
    gate_specs = [pl.BlockSpec((tm, tn), functools.partial(lambda j, i, n: (i, j + n * nd), n=n))
                  for n in range(N_BRANCH)]
    return pl.pallas_call(
        _merge_kernel,
        grid=(nd, m // tm),
        in_specs=[br_spec] * N_BRANCH + [pl.BlockSpec((None, N_BRANCH, kd, tn), lambda j, i: (layer, 0, 0, j))]
        + gate_specs,
        out_specs=pl.BlockSpec((tm, tn), lambda j, i: (i, j)),
        out_shape=jax.ShapeDtypeStruct((m, d), BF16),
        compiler_params=_params("parallel", "parallel"), name="merge",
    )(*branches, wb, gates, gates, gates, gates)


def _conv_kernel(u_ref, w_ref, b_ref, o_ref, *, t_total, n_ctx, act):
    x = u_ref[...]
    pos = lax.broadcasted_iota(jnp.int32, (t_total, 1), 0)
    seg_lo = jnp.where(pos < n_ctx, 0, n_ctx)
    seg_hi = jnp.where(pos < n_ctx, n_ctx, t_total)
    left = (CONV_W - 1) // 2
    acc = None
    for j in range(CONV_W):
        d = j - left
        xs = x if d == 0 else pltpu.roll(x, (-d) % t_total, 0)
        ok = (pos + d >= seg_lo) & (pos + d < seg_hi)
        term = jnp.where(ok, xs, 0.0) * w_ref[j:j + 1, :]
        acc = term if acc is None else acc + term
    acc = acc + b_ref[...]
    o_ref[...] = _silu(acc) if act else acc


def _dwconv(u, col0, width, w, b, *, n_ctx, act, wt=256):
    bsz, t, _ = u.shape
    c0 = col0 // wt
    return pl.pallas_call(
        functools.partial(_conv_kernel, t_total=t, n_ctx=n_ctx, act=act),
        grid=(bsz, width // wt),
        in_specs=[pl.BlockSpec((None, t, wt), lambda i, j: (i, 0, j + c0)),
                  pl.BlockSpec((CONV_W, wt), lambda i, j: (0, j)),
                  pl.BlockSpec((1, wt), lambda i, j: (0, j))],
        out_specs=pl.BlockSpec((None, t, wt), lambda i, j: (i, 0, j)),
        out_shape=jax.ShapeDtypeStruct((bsz, t, width), F32),
        compiler_params=_params("parallel", "parallel"), name="dwconv",
    )(u, w, b.reshape(1, width))


def _ret_kernel(lg_ref, q_ref, k_ref, v_ref, g_ref, cos_ref, sin_ref, o_ref, qs, ks, acc_f, acc_b,
                *, t_total, n_ctx):
    c = SCAN_CHUNK
    n_chunks, n_ctx_chunks = t_total // c, n_ctx // c
    lg = lg_ref[pl.program_id(1)]
    cos, sin = cos_ref[...], sin_ref[...]
    q, k = q_ref[...], k_ref[...]
    qs[...] = q * cos + pltpu.roll(q, RET_DK // 2, 1) * sin
    ks[...] = (k * cos + pltpu.roll(k, RET_DK // 2, 1) * sin) * (RET_DK ** -0.5)

    ii = lax.broadcasted_iota(jnp.int32, (c, c), 0)
    jj = lax.broadcasted_iota(jnp.int32, (c, c), 1)
    dist = (ii - jj).astype(F32)
    r = lax.broadcasted_iota(jnp.int32, (c, 1), 0).astype(F32)
    decay_all = jnp.exp(c * lg)

    def chunk(off, s_in, fwd):
        qc = qs[pl.ds(off, c), :]
        kc = ks[pl.ds(off, c), :]
        vc = v_ref[pl.ds(off, c), :].astype(BF16)
        if fwd:
            mask, steps, q_pow, k_pow = ii >= jj, dist, r + 1.0, c - 1.0 - r
        else:
            mask, steps, q_pow, k_pow = jj >= ii, -dist, c - r, r
        decay = jnp.where(mask, jnp.exp(jnp.where(mask, steps, 0.0) * lg), 0.0)
        scores = _dot_nt(qc.astype(BF16), kc.astype(BF16)) * decay
        o = _dot(scores.astype(BF16), vc) + jnp.exp(q_pow * lg) * _dot(qc.astype(BF16), s_in.astype(BF16))
        s_out = decay_all * s_in + _dot_tn((kc * jnp.exp(k_pow * lg)).astype(BF16), vc)
        return o, s_out

    def post(off):
        o = acc_f[pl.ds(off, c), :] + acc_b[pl.ds(off, c), :]
        y = o * lax.rsqrt(jnp.mean(o * o, axis=-1, keepdims=True) + EPS)
        o_ref[pl.ds(off, c), :] = (y * _silu(g_ref[pl.ds(off, c), :])).astype(o_ref.dtype)

    _scan_both_ways(n_chunks, n_ctx_chunks, c, lambda off, st, d: chunk(off, st, d == 0), acc_f, acc_b,
                    jnp.zeros((RET_DK, RET_DK), F32), post)


def _retention(u, col0, mix_w, cos2, sin2, *, n_ctx):
    bsz, t, _ = u.shape
    heads = mix_w // RET_DK
    c0 = col0 // RET_DK
    log_gamma = jnp.log1p(-jnp.exp2(-5.0 - jnp.arange(heads, dtype=F32)))

    def spec(k):
        return pl.BlockSpec((None, t, RET_DK), lambda i, h, lg, k=k: (i, 0, c0 + k * heads + h))

    table = pl.BlockSpec((t, RET_DK), lambda i, h, lg: (0, 0))
    return pl.pallas_call(
        functools.partial(_ret_kernel, t_total=t, n_ctx=n_ctx),
        grid_spec=pltpu.PrefetchScalarGridSpec(
            num_scalar_prefetch=1, grid=(bsz, heads),
            in_specs=[spec(0), spec(1), spec(2), spec(3), table, table],
            out_specs=pl.BlockSpec((None, t, RET_DK), lambda i, h, lg: (i, 0, h)),
            scratch_shapes=[pltpu.VMEM((t, RET_DK), F32)] * 4),
        out_shape=jax.ShapeDtypeStruct((bsz, t, mix_w), BF16),
        compiler_params=_params("parallel", "arbitrary"), name="retention",
    )(log_gamma, u, u, u, u, cos2, sin2)


def _ssd_kernel(xs_ref, bm_ref, cm_ref, z_ref, dt_ref, dtb_ref, alog_ref, dsk_ref, nw_ref, sel_ref, ord_ref, o_ref,
                acc_f, acc_b, dts, las, *, t_total, n_ctx):
    c = ord_ref.shape[1]
    rep = sel_ref.shape[1] // 2
    gw = rep * SSD_HD
    n_chunks, n_ctx_chunks = t_total // c, n_ctx // c
    dt_all = _softplus(dt_ref[...] + dtb_ref[...])
    dts[...] = dt_all
    las[...] = dt_all * (-jnp.exp(alog_ref[...]))

    ii = lax.broadcasted_iota(jnp.int32, (c, c), 0)
    jj = lax.broadcasted_iota(jnp.int32, (c, c), 1)
    lane_head = lax.broadcasted_iota(jnp.int32, (1, gw), 1) // SSD_HD

    def chunk(off, s_in, d):
        fwd = d == 0
        mask = (ii >= jj) if fwd else (jj >= ii)
        la_parts = _split_bf16(las[pl.ds(off, c), :], 3)
        cum = sum(_dot(ord_ref[d], p) for p in la_parts)
        cum_rows = sum(_dot_tn(p, ord_ref[1 - d]) for p in la_parts)
        sel = sel_ref[d]
        cum_e = sum(_dot(p, sel) for p in _split_bf16(cum, 3))
        dt_e = sum(_dot(p, sel) for p in _split_bf16(dts[pl.ds(off, c), :], 3))
        last_e = cum_e[c - 1:c, :] if fwd else cum_e[0:1, :]
        xs = xs_ref[pl.ds(off, c), :]
        vdt = xs * dt_e
        cm = cm_ref[pl.ds(off, c), :].astype(BF16)
        bm = bm_ref[pl.ds(off, c), :].astype(BF16)
        qk = _dot_nt(cm, bm)
        o = jnp.exp(cum_e) * _dot(cm, s_in.astype(BF16))
        for i in range(rep):
            col = d * rep + i
            diff = cum[:, col:col + 1] - cum_rows[col:col + 1, :]
            decay = jnp.where(mask, jnp.exp(jnp.where(mask, diff, 0.0)), 0.0)
            o = o + _dot((qk * decay).astype(BF16), jnp.where(lane_head == i, vdt, 0.0).astype(BF16))
        s_out = jnp.exp(last_e) * s_in + _dot_tn(bm, (vdt * jnp.exp(last_e - cum_e)).astype(BF16))
        return o, s_out

    def post(off):
        o = acc_f[pl.ds(off, c), :] + acc_b[pl.ds(off, c), :]
        y = (o + dsk_ref[...] * xs_ref[pl.ds(off, c), :]) * _silu(z_ref[pl.ds(off, c), :])
        y = y * lax.rsqrt(jnp.mean(y * y, axis=-1, keepdims=True) + EPS)
        o_ref[pl.ds(off, c), :] = (y * nw_ref[...]).astype(o_ref.dtype)

    _scan_both_ways(n_chunks, n_ctx_chunks, c, chunk, acc_f, acc_b, jnp.zeros((SSD_STATE, gw), F32), post)


def _ssd(u_z, z_col0, xact, u_dt, dt_bias, a_log, d_skip, norm_w, mix_w, *, n_ctx):
    bsz, t, _ = xact.shape
    heads = mix_w // SSD_HD
    rep = heads // SSD_GROUPS
    gw = rep * SSD_HD
    def per_group(v):
        lead = v.shape[:-1]
        v = v.reshape(*lead, 2, SSD_GROUPS, rep)
        return jnp.moveaxis(v, -2, 0).reshape(SSD_GROUPS, *lead, 2 * rep)
    dt_g = jnp.moveaxis(per_group(u_dt), 0, 1)
    dtb_g = per_group(dt_bias.reshape(1, 2 * heads))
    alog_g = per_group(a_log.reshape(1, 2 * heads))
    sel = np.zeros((2, 2 * rep, gw), np.float32)
    for d in range(2):
        for i in range(rep):
            sel[d, d * rep + i, i * SSD_HD:(i + 1) * SSD_HD] = 1.0
    dsk = jnp.repeat(d_skip, SSD_HD).reshape(1, mix_w)
    zc, bc, cc = z_col0 // gw, mix_w // SSD_STATE, mix_w // SSD_STATE + SSD_GROUPS
    small = lambda: pl.BlockSpec((None, 1, 2 * rep), lambda i, g: (g, 0, 0))
    return pl.pallas_call(
        functools.partial(_ssd_kernel, t_total=t, n_ctx=n_ctx),
        grid=(bsz, SSD_GROUPS),
        in_specs=[pl.BlockSpec((None, t, gw), lambda i, g: (i, 0, g)),
                  pl.BlockSpec((None, t, SSD_STATE), lambda i, g: (i, 0, bc + g)),
                  pl.BlockSpec((None, t, SSD_STATE), lambda i, g: (i, 0, cc + g)),
                  pl.BlockSpec((None, t, gw), lambda i, g: (i, 0, zc + g)),
                  pl.BlockSpec((None, None, t, 2 * rep), lambda i, g: (i, g, 0, 0)),
                  small(), small(),
                  pl.BlockSpec((1, gw), lambda i, g: (0, g)),
                  pl.BlockSpec((1, gw), lambda i, g: (0, g)),
                  pl.BlockSpec((2, 2 * rep, gw), lambda i, g: (0, 0, 0)),
                  pl.BlockSpec((2, SSD_CHUNK, SSD_CHUNK), lambda i, g: (0, 0, 0))],
        out_specs=pl.BlockSpec((None, t, gw), lambda i, g: (i, 0, g)),
        out_shape=jax.ShapeDtypeStruct((bsz, t, mix_w), BF16),
        scratch_shapes=[pltpu.VMEM((t, gw), F32), pltpu.VMEM((t, gw), F32),
                        pltpu.VMEM((t, 2 * rep), F32), pltpu.VMEM((t, 2 * rep), F32)],
        compiler_params=_params("parallel", "arbitrary"), name="ssd",
    )(xact, xact, xact, u_z, dt_g, dtb_g, alog_g, dsk, norm_w.reshape(1, mix_w), jnp.asarray(sel, BF16),
      _scan_order_tables(SSD_CHUNK))


def _tile_scan(a, b, row, reverse):
    for s in (1, 2, 4):
        if reverse:
            ok, shift = row < SUBLANES - s, SUBLANES - s
        else:
            ok, shift = row >= s, s
        b = b + a * jnp.where(ok, pltpu.roll(b, shift, 0), 0.0)
        a = a * jnp.where(ok, pltpu.roll(a, shift, 0), 1.0)
    return a, b


def _lru_kernel(xc_ref, y_ref, w_ref, bias_ref, lam_ref, o_ref, acc, a_f, b_f, a_b, b_b, *, t_total, n_ctx):
    c = SCAN_CHUNK
    nb, bw = w_ref.shape[0], w_ref.shape[1]
    n_chunks, n_ctx_chunks = t_total // c, n_ctx // c
    n_tiles = c // SUBLANES
    log_sig = -_softplus(-lam_ref[...])
    acc[...] = jnp.zeros_like(acc)
    row = lax.broadcasted_iota(jnp.int32, (SUBLANES, bw), 0)

    def gates(off, d, dst_a, dst_b):
        for blk in range(nb):
            cols = slice(blk * bw, (blk + 1) * bw)
            x = xc_ref[pl.ds(off, c), cols]
            g = _dot(x.astype(BF16), w_ref[blk, :, d * 2 * bw:(d + 1) * 2 * bw]) \
                + bias_ref[blk, :, d * 2 * bw:(d + 1) * 2 * bw]
            log_a = LRU_C * _sigmoid(g[:, :bw]) * log_sig[d:d + 1, cols]
            a = jnp.exp(log_a)
            dst_a[:, cols] = a
            dst_b[:, cols] = jnp.sqrt(-jnp.tanh(log_a) * (a * a + 1.0)) * (_sigmoid(g[:, bw:]) * x)

    def step(s, carry):
        off_f = pl.multiple_of(s * c, c)
        off_b = pl.multiple_of(_bwd_chunk(s, n_chunks, n_ctx_chunks) * c, c)
        gates(off_f, 0, a_f, b_f)
        gates(off_b, 1, a_b, b_b)

        def tile(t, carry):
            h_f, h_b = carry
            r_f = pl.multiple_of(t * SUBLANES, SUBLANES)
            r_b = pl.multiple_of((n_tiles - 1 - t) * SUBLANES, SUBLANES)
            new_f, new_b = [], []
            for blk in range(nb):
                cols = slice(blk * bw, (blk + 1) * bw)
                a, b = _tile_scan(a_f[pl.ds(r_f, SUBLANES), cols], b_f[pl.ds(r_f, SUBLANES), cols], row, False)
                h = b + a * h_f[blk]
                acc[pl.ds(off_f + r_f, SUBLANES), cols] += h
                new_f.append(h[SUBLANES - 1:SUBLANES, :])
                a, b = _tile_scan(a_b[pl.ds(r_b, SUBLANES), cols], b_b[pl.ds(r_b, SUBLANES), cols], row, True)
                h = b + a * h_b[blk]
                acc[pl.ds(off_b + r_b, SUBLANES), cols] += h
                new_b.append(h[0:1, :])
            return tuple(new_f), tuple(new_b)

        return lax.fori_loop(0, n_tiles, tile, carry)

    zero = tuple(jnp.zeros((1, bw), F32) for _ in range(nb))
    lax.fori_loop(0, n_chunks, step, (zero, zero))

    def finish(s, _):
        off = pl.multiple_of(s * c, c)
        o_ref[pl.ds(off, c), :] = (acc[pl.ds(off, c), :] * _gelu_tanh(y_ref[pl.ds(off, c), :])).astype(o_ref.dtype)
        return 0

    lax.fori_loop(0, n_chunks, finish, 0)


def _rglru(xc, u_y, y_col0, wa, ba, wx, bx, lam, mix_w, *, n_ctx, blocks_per_step=4):
    bsz, t, _ = xc.shape
    bw = mix_w // LRU_BLOCKS
    gw = blocks_per_step * bw
    w = jnp.concatenate([wa[0], wx[0], wa[1], wx[1]], axis=-1).astype(BF16)
    blk = lambda v: v.reshape(2, LRU_BLOCKS, 1, bw)
    bias = jnp.concatenate([blk(ba)[0], blk(bx)[0], blk(ba)[1], blk(bx)[1]], axis=-1)
    yc = y_col0 // gw
    return pl.pallas_call(
        functools.partial(_lru_kernel, t_total=t, n_ctx=n_ctx),
        grid=(bsz, LRU_BLOCKS // blocks_per_step),
        in_specs=[pl.BlockSpec((None, t, gw), lambda i, j: (i, 0, j)),
                  pl.BlockSpec((None, t, gw), lambda i, j: (i, 0, yc + j)),
                  pl.BlockSpec((blocks_per_step, bw, 4 * bw), lambda i, j: (j, 0, 0)),
                  pl.BlockSpec((blocks_per_step, 1, 4 * bw), lambda i, j: (j, 0, 0)),
                  pl.BlockSpec((2, gw), lambda i, j: (0, j))],
        out_specs=pl.BlockSpec((None, t, gw), lambda i, j: (i, 0, j)),
        out_shape=jax.ShapeDtypeStruct((bsz, t, mix_w), BF16),
        scratch_shapes=[pltpu.VMEM((t, gw), F32)] + [pltpu.VMEM((SCAN_CHUNK, gw), F32)] * 4,
        compiler_params=_params("parallel", "arbitrary"), name="rglru",
    )(xc, u_y, w, bias, lam)


def _hg_tables():
    c = HG_CHUNK
    idx = np.arange(c)
    dif = np.zeros((2, HG_LEVELS, c, c), np.float32)
    isq = np.zeros((2, HG_LEVELS, c, HG_DK), np.float32)
    msk = np.zeros((2, HG_LEVELS, c, c), np.float32)
    for lvl in range(HG_LEVELS):
        half = 1 << lvl
        mid = (idx // (2 * half)) * 2 * half + half
        late = idx >= mid
        same = (idx[:, None] // (2 * half)) == (idx[None, :] // (2 * half))
        for d, boundary, is_query in ((0, mid - 1, late), (1, mid, ~late)):
            sign = np.where(is_query, 1.0, -1.0)
            dif[d, lvl, idx, idx] += sign
            dif[d, lvl, idx, boundary] -= sign
            isq[d, lvl] = is_query[:, None]
        msk[0, lvl] = same & late[:, None] & ~late[None, :]
        msk[1, lvl] = same & ~late[:, None] & late[None, :]
    lower = (idx[None, :] <= idx[:, None]).astype(np.float32)
    order = np.stack([lower, lower.T])
    rep3 = lambda v: jnp.asarray(np.concatenate([v, v, v], axis=-1), BF16)
    return rep3(order), rep3(dif.reshape(2, HG_LEVELS * c, c)), jnp.asarray(isq), jnp.asarray(msk)


def _hg_kernel(q_ref, f0_ref, f1_ref, v_ref, g_ref, lb_ref, nw_ref, ord_ref, dif_ref, isq_ref, msk_ref,
               o_ref, acc_f, acc_b, *, t_total, n_ctx):
    c = HG_CHUNK
    n_chunks, n_ctx_chunks = t_total // c, n_ctx // c
    eye = lax.broadcasted_iota(jnp.int32, (c, c), 0) == lax.broadcasted_iota(jnp.int32, (c, c), 1)

    def chunk(off, st, d):
        fwd = d == 0
        lb = lb_ref[d:d + 1, :]
        f = (f0_ref if fwd else f1_ref)[pl.ds(off, c), :]
        log_f = jnp.log(lb + (1.0 - lb) * _sigmoid(f))
        k = (1.0 - lb) * _sigmoid(-f)
        q = _silu(q_ref[pl.ds(off, c), :])
        v = v_ref[pl.ds(off, c), :].astype(BF16)
        cum = _dot(ord_ref[d], jnp.concatenate(_split_bf16(log_f, 3), axis=0))
        expo = _dot(dif_ref[d], jnp.concatenate(_split_bf16(cum, 3), axis=0))
        q_minus_k = q - k
        scores = jnp.where(eye, _dot_nt(q.astype(BF16), k.astype(BF16)), 0.0)
        for lvl in range(HG_LEVELS):
            rows = slice(lvl * c, (lvl + 1) * c)
            y = ((k + isq_ref[d, lvl] * q_minus_k) * jnp.exp(expo[rows, :])).astype(BF16)
            scores = scores + _dot_nt(y, y) * msk_ref[d, lvl]
        last = cum[c - 1:c, :] if fwd else cum[0:1, :]
        o = _dot(scores.astype(BF16), v) + _dot_nt((q * jnp.exp(cum)).astype(BF16), st.astype(BF16))
        st = st * jnp.exp(last) + _dot_tn(v, (k * jnp.exp(last - cum)).astype(BF16))
        return o, st

    def post(off):
        o = acc_f[pl.ds(off, c), :] + acc_b[pl.ds(off, c), :]
        y = o * lax.rsqrt(jnp.mean(o * o, axis=-1, keepdims=True) + EPS) * nw_ref[...]
        o_ref[pl.ds(off, c), :] = (y * _silu(g_ref[pl.ds(off, c), :])).astype(o_ref.dtype)

    _scan_both_ways(n_chunks, n_ctx_chunks, c, chunk, acc_f, acc_b, jnp.zeros((HG_DK, HG_DK), F32), post)


def _hgrn2(u, col0, mix_w, lower, norm_w, *, n_ctx):
    bsz, t, _ = u.shape
    heads = mix_w // HG_DK
    c0 = col0 // HG_DK
    tables = _hg_tables()

    def spec(k):
        return pl.BlockSpec((None, t, HG_DK), lambda i, h, k=k: (i, 0, c0 + k * heads + h))

    def table(v):
        return pl.BlockSpec(v.shape, lambda i, h, nd=v.ndim: (0,) * nd)

    return pl.pallas_call(
        functools.partial(_hg_kernel, t_total=t, n_ctx=n_ctx),
        grid=(bsz, heads),
        in_specs=[spec(0), spec(1), spec(2), spec(3), spec(4),
                  pl.BlockSpec((2, HG_DK), lambda i, h: (0, h)),
                  pl.BlockSpec((1, HG_DK), lambda i, h: (0, h))] + [table(v) for v in tables],
        out_specs=pl.BlockSpec((None, t, HG_DK), lambda i, h: (i, 0, h)),
        out_shape=jax.ShapeDtypeStruct((bsz, t, mix_w), BF16),
        scratch_shapes=[pltpu.VMEM((t, HG_DK), F32)] * 2,
        compiler_params=_params("parallel", "arbitrary"), name="hgrn2",
    )(u, u, u, u, u, lower, norm_w.reshape(1, mix_w), *tables)


def _ada_kernel(c_ref, w_ref, b_ref, o_ref):
    o_ref[...] = _dot(_silu(c_ref[...]).astype(BF16), w_ref[...].astype(BF16)) + b_ref[...]


def _ada(cond, ada_w, ada_b, tn=1024):
    r, d = cond.shape
    depth, _, n = ada_w.shape
    return pl.pallas_call(
        _ada_kernel,
        grid=(depth, n // tn),
        in_specs=[pl.BlockSpec((r, d), lambda l, j: (0, 0)),
                  pl.BlockSpec((None, d, tn), lambda l, j: (l, 0, j)),
                  pl.BlockSpec((None, 1, tn), lambda l, j: (l, 0, j))],
        out_specs=pl.BlockSpec((None, r, tn), lambda l, j: (l, 0, j)),
        out_shape=jax.ShapeDtypeStruct((depth, r, n), F32),
        compiler_params=_params("parallel", "parallel"), name="ada",
    )(cond, ada_w, ada_b.reshape(depth, 1, n))


def _rope_tables(n_ctx, n_lat):
    r, col = jnp.meshgrid(jnp.arange(n_lat // GRID_W), jnp.arange(GRID_W), indexing='ij')
    n_freq = RET_DK // 4
    freqs = ROPE_BASE ** (-jnp.arange(n_freq, dtype=F32) / n_freq)
    ang = jnp.concatenate([r.reshape(-1, 1) * freqs, col.reshape(-1, 1) * freqs], axis=-1)
    cos, sin = jnp.cos(ang), jnp.sin(ang)
    cos2 = jnp.concatenate([jnp.ones((n_ctx, RET_DK), F32), jnp.concatenate([cos, cos], axis=-1)], axis=0)
    sin2 = jnp.concatenate([jnp.zeros((n_ctx, RET_DK), F32), jnp.concatenate([-sin, sin], axis=-1)], axis=0)
    return cos2, sin2


def kernel(x, c, ctx, c_ctx, ada_w, ada_b, norm1_w, norm2_w, w_in, ssd_conv_w, ssd_conv_b, ssd_a_log, ssd_dt_bias, ssd_d, ssd_norm_w, lru_conv_w, lru_conv_b, lru_wa, lru_ba, lru_wx, lru_bx, lru_lambda, hg_lb_logits, hg_norm_w, w_branch, w_out, ffn_wgu, ffn_w2, router_w, moe_wgu, moe_w2, final_norm_w):
    bsz, n_lat, d = x.shape
    n_ctx = ctx.shape[1]
    t = n_ctx + n_lat
    depth = w_in.shape[0]
    mix_w = d // 2
    ssd_heads = mix_w // SSD_HD
    xbc_w = mix_w + 2 * SSD_GROUPS * SSD_STATE
    ret_w, ssd_w, dt_w, lru_w, hg_w, gate_w = 4 * mix_w, mix_w + xbc_w, 2 * ssd_heads, 2 * mix_w, 5 * mix_w, N_BRANCH * d
    dt_col = ret_w + ssd_w
    w_lo = w_in[:, :, :dt_col + LANES].astype(BF16)
    w_hi = w_in[:, :, dt_col + dt_w:].astype(BF16)
    w_branch_bf, w_out_bf, ffn_w2_bf = w_branch.astype(BF16), w_out.astype(BF16), ffn_w2.astype(BF16)

    cos2, sin2 = _rope_tables(n_ctx, n_lat)
    p_lb = jax.nn.softmax(hg_lb_logits.astype(F32), axis=1)
    lower_bounds = jnp.cumsum(p_lb, axis=1) - p_lb[:, :1]

    pad = (-(1 + bsz)) % SUBLANES
    cond = jnp.concatenate([c_ctx[None], c, jnp.zeros((pad, d), F32)], axis=0)
    mods = _ada(cond, ada_w, ada_b).reshape(depth, -1, N_MOD, d)

    h = jnp.concatenate([ctx.astype(x.dtype), x], axis=1)
    for layer in range(depth):
        m_ctx = jnp.broadcast_to(mods[layer, 0], (bsz, N_MOD, d))
        m_lat = mods[layer, 1:1 + bsz]
        mod_rows = jnp.stack([m_ctx, m_lat], axis=1)
        mod_cols = jnp.stack([m_ctx, m_lat], axis=2)

        xn = _norm_call(h, norm1_w[layer], mod_rows, shift=0, scale=1, n_ctx=n_ctx).reshape(bsz * t, d)
        u_ret = _matmul(xn, w_lo, layer, 0, ret_w).reshape(bsz, t, ret_w)
        u_ssd = _matmul(xn, w_lo, layer, ret_w, ssd_w).reshape(bsz, t, ssd_w)
        u_dt = _matmul(xn, w_lo, layer, dt_col, LANES)[:, :dt_w].reshape(bsz, t, dt_w)
        u_lru = _matmul(xn, w_hi, layer, 0, lru_w).reshape(bsz, t, lru_w)
        u_hg = _matmul(xn, w_hi, layer, lru_w, hg_w).reshape(bsz, t, hg_w)
        u_gates = _matmul(xn, w_hi, layer, lru_w + hg_w, gate_w)

        b_ret = _retention(u_ret, 0, mix_w, cos2, sin2, n_ctx=n_ctx)
        xact = _dwconv(u_ssd, mix_w, xbc_w, ssd_conv_w[layer], ssd_conv_b[layer], n_ctx=n_ctx, act=True)
        b_ssd = _ssd(u_ssd, 0, xact, u_dt, ssd_dt_bias[layer], ssd_a_log[layer], ssd_d[layer],
                     ssd_norm_w[layer], mix_w, n_ctx=n_ctx)
        xc = _dwconv(u_lru, 0, mix_w, lru_conv_w[layer], lru_conv_b[layer], n_ctx=n_ctx, act=False)
        b_lru = _rglru(xc, u_lru, mix_w, lru_wa[layer], lru_ba[layer], lru_wx[layer], lru_bx[layer],
                       lru_lambda[layer], mix_w, n_ctx=n_ctx)
        b_hg = _hgrn2(u_hg, 0, mix_w, lower_bounds[:, layer], hg_norm_w[layer], n_ctx=n_ctx)

        branches = [v.reshape(bsz * t, mix_w) for v in (b_ret, b_ssd, b_lru, b_hg)]
        mixed = _merge(branches, u_gates, w_branch_bf, layer, tm=768)
        h2 = _matmul_residual(mixed, w_out_bf, layer, h.reshape(bsz * t, d), mod_cols, 2, t_total=t, n_ctx=n_ctx)

        if layer % 2 == 0:
            xn = _norm_call(h2.reshape(bsz, t, d), norm2_w[layer], mod_rows, shift=3, scale=4, n_ctx=n_ctx)
            act = _swiglu_up(xn.reshape(bsz * t, d), ffn_wgu, layer // 2)
            h = _matmul_residual(act, ffn_w2_bf, layer // 2, h2, mod_cols, 5, t_total=t, n_ctx=n_ctx,
                                 tm=768, tn=512, tk=act.shape[1]).reshape(bsz, t, d)
        else:
            xn, gates = _norm_call(h2.reshape(bsz, t, d), norm2_w[layer], mod_rows, shift=3, scale=4, n_ctx=n_ctx,
                                   router_w=router_w[layer // 2], out_dtype=F32)
            h = _moe_block(xn, gates, h2.reshape(bsz, t, d), moe_wgu, moe_w2, layer // 2, mod_rows, 5, n_ctx=n_ctx)
    return _final_norm_call(h, final_norm_w, n_ctx)
```

```python
import functools
import math

import numpy as np
import jax
import jax.numpy as jnp
from jax import lax
from jax.experimental import pallas as pl
from jax.experimental.pallas import tpu as pltpu

F32 = jnp.float32
BF16 = jnp.bfloat16
HI = lax.Precision.HIGHEST

EPS = 1e-6
GRID_W = 64
ROPE_BASE = 10000.0
RET_DK = 128
SSD_HD = 64
SSD_GROUPS = 4
SSD_STATE = 128
CONV_W = 4
LRU_BLOCKS = 8
LRU_C = 8.0
HG_DK = 128
N_BRANCH = 4
N_EXPERTS = 8
N_MOD = 6

LANES = 128
SUBLANES = 8
VMEM_LIMIT = 52 * 1024 * 1024

SCAN_CHUNK = 256
SSD_CHUNK = 256
HG_CHUNK = 128
HG_LEVELS = 7


def _params(*sem):
    return pltpu.CompilerParams(dimension_semantics=sem, vmem_limit_bytes=VMEM_LIMIT)


def _dot(a, b, prec=None):
    return jnp.dot(a, b, preferred_element_type=F32, precision=prec)


def _dot_nt(a, b, prec=None):
    return lax.dot_general(a, b, (((1,), (1,)), ((), ())), preferred_element_type=F32, precision=prec)


def _dot_tn(a, b, prec=None):
    return lax.dot_general(a, b, (((0,), (0,)), ((), ())), preferred_element_type=F32, precision=prec)


def _sigmoid(x):
    return 1.0 / (1.0 + jnp.exp(-x))


def _silu(x):
    return x * _sigmoid(x)


def _softplus(x):
    return jnp.maximum(x, 0.0) + jnp.log1p(jnp.exp(-jnp.abs(x)))


def _gelu_tanh(x):
    return 0.5 * x * (1.0 + jnp.tanh(math.sqrt(2.0 / math.pi) * (x + 0.044715 * (x * x * x))))


def _bwd_chunk(s, n_chunks, n_ctx_chunks):
    return jnp.where(s < n_ctx_chunks, n_ctx_chunks - 1 - s, n_chunks - 1 - (s - n_ctx_chunks))


def _scan_both_ways(n_chunks, n_ctx_chunks, c, chunk_fn, acc_f, acc_b, zero_state, post_fn):
    def body(s, carry):
        st_f, st_b = carry
        off_f = pl.multiple_of(s * c, c)
        off_b = pl.multiple_of(_bwd_chunk(s, n_chunks, n_ctx_chunks) * c, c)
        o_f, st_f = chunk_fn(off_f, st_f, 0)
        o_b, st_b = chunk_fn(off_b, st_b, 1)
        acc_f[pl.ds(off_f, c), :] = o_f
        acc_b[pl.ds(off_b, c), :] = o_b
        return st_f, st_b

    lax.fori_loop(0, n_chunks, body, (zero_state, zero_state), unroll=3)

    def finish(s, _):
        post_fn(pl.multiple_of(s * c, c))
        return 0

    lax.fori_loop(0, n_chunks, finish, 0)


def _scan_order_tables(c):
    idx = np.arange(c)
    lower = (idx[None, :] <= idx[:, None]).astype(np.float32)
    return jnp.asarray(np.stack([lower, lower.T]), BF16)


def _split_bf16(x, parts):
    out = []
    for _ in range(parts):
        p = x.astype(BF16)
        out.append(p)
        x = x - p.astype(F32)
    return out


def _norm_kernel(h_ref, w_ref, mod_ref, o_ref, *, shift, scale):
    x = h_ref[...]
    y = x * lax.rsqrt(jnp.mean(x * x, axis=-1, keepdims=True) + EPS) * w_ref[...]
    o_ref[...] = (y * (1.0 + mod_ref[scale:scale + 1, :]) + mod_ref[shift:shift + 1, :]).astype(o_ref.dtype)


def _norm_router_kernel(h_ref, w_ref, mod_ref, rw_ref, o_ref, gate_ref, *, shift, scale):
    x = h_ref[...]
    y = x * lax.rsqrt(jnp.mean(x * x, axis=-1, keepdims=True) + EPS) * w_ref[...]
    xn = y * (1.0 + mod_ref[scale:scale + 1, :]) + mod_ref[shift:shift + 1, :]
    o_ref[...] = xn.astype(o_ref.dtype)
    logits = _dot(xn.astype(BF16), rw_ref[...])
    lane = lax.broadcasted_iota(jnp.int32, logits.shape, 1)
    neg = jnp.float32(-jnp.inf)
    logits = jnp.where(lane < N_EXPERTS, logits, neg)
    v1 = jnp.max(logits, axis=-1, keepdims=True)
    i1 = jnp.min(jnp.where(logits == v1, lane, LANES), axis=-1, keepdims=True)
    rest = jnp.where(lane == i1, neg, logits)
    v2 = jnp.max(rest, axis=-1, keepdims=True)
    i2 = jnp.min(jnp.where(rest == v2, lane, LANES), axis=-1, keepdims=True)
    e2 = jnp.exp(v2 - v1)
    w1 = 1.0 / (1.0 + e2)
    w2 = e2 / (1.0 + e2)
    gate_ref[...] = (jnp.where(lane == i1, w1, 0.0) + jnp.where(lane == i2, w2, 0.0)
                     + jnp.where(lane == N_EXPERTS, i1.astype(F32), 0.0)
                     + jnp.where(lane == N_EXPERTS + 1, i2.astype(F32), 0.0))


def _norm_call(h, w, mod, *, shift, scale, n_ctx, router_w=None, out_dtype=BF16):
    b, t, d = h.shape
    tr = n_ctx
    grid = (b, t // tr)
    h_spec = pl.BlockSpec((None, tr, d), lambda i, j: (i, j, 0))
    w_spec = pl.BlockSpec((1, d), lambda i, j: (0, 0))
    mod_spec = pl.BlockSpec((None, None, N_MOD, d), lambda i, j: (i, jnp.minimum(j, 1), 0, 0))
    if router_w is None:
        return pl.pallas_call(
            functools.partial(_norm_kernel, shift=shift, scale=scale),
            grid=grid, in_specs=[h_spec, w_spec, mod_spec], out_specs=h_spec,
            out_shape=jax.ShapeDtypeStruct(h.shape, out_dtype),
            compiler_params=_params("parallel", "parallel"), name="norm_mod",
        )(h, w.reshape(1, d), mod)
    rw = jnp.zeros((d, LANES), BF16).at[:, :N_EXPERTS].set(router_w.astype(BF16))
    return pl.pallas_call(
        functools.partial(_norm_router_kernel, shift=shift, scale=scale),
        grid=grid,
        in_specs=[h_spec, w_spec, mod_spec, pl.BlockSpec((d, LANES), lambda i, j: (0, 0))],
        out_specs=[h_spec, pl.BlockSpec((None, tr, LANES), lambda i, j: (i, j, 0))],
        out_shape=[jax.ShapeDtypeStruct(h.shape, out_dtype), jax.ShapeDtypeStruct((b, t, LANES), F32)],
        compiler_params=_params("parallel", "parallel"), name="norm_mod_router",
    )(h, w.reshape(1, d), mod, rw)


def _final_norm_kernel(h_ref, w_ref, o_ref):
    x = h_ref[...]
    o_ref[...] = x * lax.rsqrt(jnp.mean(x * x, axis=-1, keepdims=True) + EPS) * w_ref[...]


def _final_norm_call(h, w, n_ctx):
    b, t, d = h.shape
    tr = n_ctx
    return pl.pallas_call(
        _final_norm_kernel,
        grid=(b, (t - n_ctx) // tr),
        in_specs=[pl.BlockSpec((None, tr, d), lambda i, j: (i, j + 1, 0)), pl.BlockSpec((1, d), lambda i, j: (0, 0))],
        out_specs=pl.BlockSpec((None, tr, d), lambda i, j: (i, j, 0)),
        out_shape=jax.ShapeDtypeStruct((b, t - n_ctx, d), F32),
        compiler_params=_params("parallel", "parallel"), name="final_norm",
    )(h, w.reshape(1, d))


def _pick(n, prefs):
    for p in prefs:
        if n % p == 0:
            return p
    return n


def _mm_kernel(a_ref, w_ref, o_ref, w_bf):
    @pl.when(pl.program_id(1) == 0)
    def _():
        w_bf[...] = w_ref[...].astype(BF16)

    o_ref[...] = _dot(a_ref[...], w_bf[...]).astype(o_ref.dtype)


def _matmul(a, w, layer, col0, n, out_dtype=F32, tm=None, tn=None):
    m, kd = a.shape
    tm = tm or _pick(m, (1152, 1024, 768, 512, 256))
    tn = tn or _pick(math.gcd(n, col0) if col0 else n, (1024, 512, 256, 128))
    c0 = col0 // tn
    return pl.pallas_call(
        _mm_kernel,
        grid=(n // tn, m // tm),
        in_specs=[pl.BlockSpec((tm, kd), lambda j, i: (i, 0)),
                  pl.BlockSpec((None, kd, tn), lambda j, i: (layer, 0, c0 + j))],
        out_specs=pl.BlockSpec((tm, tn), lambda j, i: (i, j)),
        out_shape=jax.ShapeDtypeStruct((m, n), out_dtype),
        scratch_shapes=[pltpu.VMEM((kd, tn), BF16)],
        compiler_params=_params("arbitrary", "arbitrary"), name="matmul",
    )(a, w)


def _row_is_ctx(tm, t_total, n_ctx):
    row0 = pl.program_id(1) * tm
    pos = (row0 + lax.broadcasted_iota(jnp.int32, (tm, 1), 0)) % t_total
    return pos < n_ctx


def _mm_res_kernel(a_ref, w_ref, res_ref, mod_ref, o_ref, acc_ref, *, nk, tm, t_total, n_ctx):
    k = pl.program_id(2)
    part = _dot(a_ref[...], w_ref[...])

    def finish(total):
        mod = jnp.where(_row_is_ctx(tm, t_total, n_ctx), mod_ref[0:1, :], mod_ref[1:2, :])
        o_ref[...] = res_ref[...] + mod * total

    if nk == 1:
        finish(part)
        return

    @pl.when(k == 0)
    def _():
        acc_ref[...] = part

    @pl.when(k > 0)
    def _():
        acc_ref[...] += part

    @pl.when(k == nk - 1)
    def _():
        finish(acc_ref[...])


def _matmul_residual(a, w, layer, res, mod, mod_idx, *, t_total, n_ctx, tm=None, tn=None, tk=None):
    m, kd = a.shape
    n = w.shape[2]
    tm = tm or _pick(t_total, (1152, 768, 256))
    tn = tn or _pick(n, (1024, 512, 256, 128))
    tk = tk or (kd if kd <= 2048 else _pick(kd, (512, 256, 128)))
    nk = kd // tk
    per_b = t_total // tm
    return pl.pallas_call(
        functools.partial(_mm_res_kernel, nk=nk, tm=tm, t_total=t_total, n_ctx=n_ctx),
        grid=(n // tn, m // tm, nk),
        in_specs=[pl.BlockSpec((tm, tk), lambda j, i, k: (i, k)),
                  pl.BlockSpec((None, tk, tn), lambda j, i, k: (layer, k, j)),
                  pl.BlockSpec((tm, tn), lambda j, i, k: (i, j)),
                  pl.BlockSpec((None, None, 2, tn), lambda j, i, k: (i // per_b, mod_idx, 0, j))],
        out_specs=pl.BlockSpec((tm, tn), lambda j, i, k: (i, j)),
        out_shape=jax.ShapeDtypeStruct((m, n), F32),
        scratch_shapes=[pltpu.VMEM((tm, tn), F32)],
        compiler_params=_params("parallel", "parallel", "arbitrary"), name="matmul_residual",
    )(a, w, res, mod)


def _swiglu_kernel(a_ref, wg_ref, wu_ref, o_ref, wg_bf, wu_bf):
    @pl.when(pl.program_id(1) == 0)
    def _():
        wg_bf[...] = wg_ref[...].astype(BF16)
        wu_bf[...] = wu_ref[...].astype(BF16)

    a = a_ref[...]
    o_ref[...] = (_silu(_dot(a, wg_bf[...])) * _dot(a, wu_bf[...])).astype(o_ref.dtype)


def _swiglu_up(a, wgu, layer, tm=None, tn=None):
    m, kd = a.shape
    f = wgu.shape[2] // 2
    tm = tm or _pick(m, (1152, 1024, 768, 512, 256))
    tn = tn or _pick(f, (512, 256, 128))
    nf = f // tn
    return pl.pallas_call(
        _swiglu_kernel,
        grid=(nf, m // tm),
        in_specs=[pl.BlockSpec((tm, kd), lambda j, i: (i, 0)),
                  pl.BlockSpec((None, kd, tn), lambda j, i: (layer, 0, j)),
                  pl.BlockSpec((None, kd, tn), lambda j, i: (layer, 0, j + nf))],
        out_specs=pl.BlockSpec((tm, tn), lambda j, i: (i, j)),
        out_shape=jax.ShapeDtypeStruct((m, f), BF16),
        scratch_shapes=[pltpu.VMEM((kd, tn), BF16)] * 2,
        compiler_params=_params("arbitrary", "arbitrary"), name="swiglu_up",
    )(a, wgu, wgu)


MOE_TM = 512
GATHER_ROWS = 256


def _route(gates, n_tok):
    tm = MOE_TM
    n_slot = 2 * n_tok
    n_tiles = n_slot // tm + N_EXPERTS
    i32 = jnp.int32
    eid = gates[:, N_EXPERTS:N_EXPERTS + 2].astype(i32).reshape(n_slot)
    order = jnp.argsort(eid, stable=True).astype(i32)
    rank = jnp.argsort(order).astype(i32)
    onehot = eid[:, None] == jnp.arange(N_EXPERTS, dtype=i32)[None, :]
    counts = jnp.sum(onehot, axis=0, dtype=i32)
    padded = (counts + tm - 1) // tm * tm
    ends = jnp.cumsum(padded)
    group_end = jnp.cumsum(counts)
    shift = (ends - padded) - (group_end - counts)
    dest = (rank + jnp.sum(jnp.where(onehot, shift[None, :], 0), axis=1, dtype=i32)).reshape(n_tok, 2)
    tile_start = jnp.arange(n_tiles, dtype=i32) * tm
    tile_expert = jnp.minimum(jnp.sum(tile_start[:, None] >= ends[None, :], axis=1, dtype=i32), N_EXPERTS - 1)
    row_expert = jnp.repeat(tile_expert, tm)
    pos = jnp.arange(n_tiles * tm, dtype=i32) - shift[row_expert]
    rows = jnp.arange(n_tiles * tm, dtype=i32)
    row_src = jnp.where(pos < group_end[row_expert], order[jnp.clip(pos, 0, n_slot - 1)] // 2, rows % n_tok)
    tile_first = jnp.concatenate([jnp.ones((1,), i32), (tile_expert[1:] != tile_expert[:-1]).astype(i32)])
    return row_src, dest, tile_expert, tile_first, (ends[-1:] // tm).astype(i32)


def _gather_kernel(idx_ref, src_ref, o_ref, sem, *, rows):
    base = pl.program_id(0) * rows

    def row_copy(r, src_row):
        return pltpu.make_async_copy(src_ref.at[pl.ds(src_row, 1), :], o_ref.at[pl.ds(r, 1), :], sem)

    def issue(r, carry):
        row_copy(r, idx_ref[base + r]).start()
        return carry

    def drain(r, carry):
        row_copy(r, 0).wait()
        return carry

    lax.fori_loop(0, rows, issue, 0, unroll=8)
    lax.fori_loop(0, rows, drain, 0, unroll=8)


def _gather_rows(src, idx):
    p, w = idx.shape[0], src.shape[1]
    rows = GATHER_ROWS
    return pl.pallas_call(
        functools.partial(_gather_kernel, rows=rows),
        grid_spec=pltpu.PrefetchScalarGridSpec(
            num_scalar_prefetch=1, grid=(p // rows,),
            in_specs=[pl.BlockSpec(memory_space=pl.ANY)],
            out_specs=pl.BlockSpec((rows, w), lambda i, idx: (i, 0)),
            scratch_shapes=[pltpu.SemaphoreType.DMA(())]),
        out_shape=jax.ShapeDtypeStruct((p, w), src.dtype),
        compiler_params=_params("arbitrary"), name="gather_rows",
    )(idx, src)


def _moe_up_kernel(te_ref, tf_ref, nu_ref, a_ref, wg_ref, wu_ref, o_ref, wg_bf, wu_bf):
    i = pl.program_id(1)

    @pl.when(tf_ref[i] == 1)
    def _():
        wg_bf[...] = wg_ref[...].astype(BF16)
        wu_bf[...] = wu_ref[...].astype(BF16)

    @pl.when(i < nu_ref[0])
    def _():
        a = a_ref[...].astype(BF16)
        o_ref[...] = (_silu(_dot(a, wg_bf[...])) * _dot(a, wu_bf[...])).astype(o_ref.dtype)

    @pl.when(i >= nu_ref[0])
    def _():
        o_ref[...] = jnp.zeros_like(o_ref)


def _moe_up(xg, wgu, layer, tile_expert, tile_first, n_used, tn=512):
    p, kd = xg.shape
    f = wgu.shape[3] // 2
    nf = f // tn
    tm = MOE_TM
    return pl.pallas_call(
        _moe_up_kernel,
        grid_spec=pltpu.PrefetchScalarGridSpec(
            num_scalar_prefetch=3, grid=(nf, p // tm),
            in_specs=[pl.BlockSpec((tm, kd), lambda j, i, te, tf, nu: (i, 0)),
                      pl.BlockSpec((None, None, kd, tn), lambda j, i, te, tf, nu: (layer, te[i], 0, j)),
                      pl.BlockSpec((None, None, kd, tn), lambda j, i, te, tf, nu: (layer, te[i], 0, j + nf))],
            out_specs=pl.BlockSpec((tm, tn), lambda j, i, te, tf, nu: (i, j)),
            scratch_shapes=[pltpu.VMEM((kd, tn), BF16)] * 2),
        out_shape=jax.ShapeDtypeStruct((p, f), BF16),
        compiler_params=_params("arbitrary", "arbitrary"), name="moe_up",
    )(tile_expert, tile_first, n_used, xg, wgu, wgu)


def _moe_down_kernel(te_ref, tf_ref, nu_ref, a_ref, w_ref, o_ref, w_bf):
    i = pl.program_id(1)

    @pl.when(tf_ref[i] == 1)
    def _():
        w_bf[...] = w_ref[...].astype(BF16)

    @pl.when(i < nu_ref[0])
    def _():
        o_ref[...] = _dot(a_ref[...], w_bf[...])

    @pl.when(i >= nu_ref[0])
    def _():
        o_ref[...] = jnp.zeros_like(o_ref)


def _moe_down(act, w2, layer, tile_expert, tile_first, n_used, tn=512):
    p, f = act.shape
    d = w2.shape[3]
    tm = MOE_TM
    return pl.pallas_call(
        _moe_down_kernel,
        grid_spec=pltpu.PrefetchScalarGridSpec(
            num_scalar_prefetch=3, grid=(d // tn, p // tm),
            in_specs=[pl.BlockSpec((tm, f), lambda j, i, te, tf, nu: (i, 0)),
                      pl.BlockSpec((None, None, f, tn), lambda j, i, te, tf, nu: (layer, te[i], 0, j))],
            out_specs=pl.BlockSpec((tm, tn), lambda j, i, te, tf, nu: (i, j)),
            scratch_shapes=[pltpu.VMEM((f, tn), BF16)]),
        out_shape=jax.ShapeDtypeStruct((p, d), F32),
        compiler_params=_params("arbitrary", "arbitrary"), name="moe_down",
    )(tile_expert, tile_first, n_used, act, w2)


def _moe_combine_kernel(y0_ref, y1_ref, gate_ref, res_ref, mod_ref, o_ref, *, mod_idx):
    gates = gate_ref[...]
    lane = lax.broadcasted_iota(jnp.int32, gates.shape, 1)
    i0 = gates[:, N_EXPERTS:N_EXPERTS + 1].astype(jnp.int32)
    i1 = gates[:, N_EXPERTS + 1:N_EXPERTS + 2].astype(jnp.int32)
    w0 = jnp.sum(jnp.where(lane == i0, gates, 0.0), axis=-1, keepdims=True)
    w1 = jnp.sum(jnp.where(lane == i1, gates, 0.0), axis=-1, keepdims=True)
    o_ref[...] = res_ref[...] + mod_ref[mod_idx:mod_idx + 1, :] * (w0 * y0_ref[...] + w1 * y1_ref[...])


def _moe_combine(y0, y1, gates, res, mod, mod_idx, *, n_ctx):
    b, t, d = res.shape
    tr = n_ctx
    spec = pl.BlockSpec((None, tr, d), lambda i, j: (i, j, 0))
    return pl.pallas_call(
        functools.partial(_moe_combine_kernel, mod_idx=mod_idx),
        grid=(b, t // tr),
        in_specs=[spec, spec, pl.BlockSpec((None, tr, LANES), lambda i, j: (i, j, 0)), spec,
                  pl.BlockSpec((None, None, N_MOD, d), lambda i, j: (i, jnp.minimum(j, 1), 0, 0))],
        out_specs=spec,
        out_shape=jax.ShapeDtypeStruct((b, t, d), F32),
        compiler_params=_params("parallel", "parallel"), name="moe_combine",
    )(y0, y1, gates, res, mod)


def _moe_block(xn, gates, res, wgu, w2, layer, mod, mod_idx, *, n_ctx):
    b, t, d = xn.shape
    m = b * t
    row_src, dest, tile_expert, tile_first, n_used = _route(gates.reshape(m, LANES), m)
    xg = _gather_rows(xn.reshape(m, d), row_src)
    act = _moe_up(xg, wgu, layer, tile_expert, tile_first, n_used)
    yg = _moe_down(act, w2, layer, tile_expert, tile_first, n_used)
    y0 = _gather_rows(yg, dest[:, 0]).reshape(b, t, d)
    y1 = _gather_rows(yg, dest[:, 1]).reshape(b, t, d)
    return _moe_combine(y0, y1, gates, res, mod, mod_idx, n_ctx=n_ctx)


def _merge_kernel(b0, b1, b2, b3, w_ref, g0, g1, g2, g3, o_ref):
    acc = None
    for n, (br, g) in enumerate(((b0, g0), (b1, g1), (b2, g2), (b3, g3))):
        term = _sigmoid(g[...].astype(F32)) * _dot(br[...], w_ref[n])
        acc = term if acc is None else acc + term
    o_ref[...] = acc.astype(o_ref.dtype)


def _merge(branches, gates, wb, layer, tm=None, tn=None):
    m, kd = branches[0].shape
    d = wb.shape[-1]
    tm = tm or _pick(m, (1152, 1024, 768, 512, 256))
    tn = tn or _pick(d, (512, 256, 128))
    nd = d // tn
    br_spec = pl.BlockSpec((tm, kd), lambda j, i: (i, 0))
    gate_specs = [pl.BlockSpec((tm, tn), functools.partial(lambda j, i, n: (i, j + n * nd), n=n))
                  for n in range(N_BRANCH)]
    return pl.pallas_call(
        _merge_kernel,
        grid=(nd, m // tm),
        in_specs=[br_spec] * N_BRANCH + [pl.BlockSpec((None, N_BRANCH, kd, tn), lambda j, i: (layer, 0, 0, j))]
        + gate_specs,
        out_specs=pl.BlockSpec((tm, tn), lambda j, i: (i, j)),
        out_shape=jax.ShapeDtypeStruct((m, d), BF16),
        compiler_params=_params("parallel", "parallel"), name="merge",
    )(*branches, wb, gates, gates, gates, gates)


def _conv_kernel(u_ref, w_ref, b_ref, o_ref, *, t_total, n_ctx, act):
    x = u_ref[...]
    pos = lax.broadcasted_iota(jnp.int32, (t_total, 1), 0)
    seg_lo = jnp.where(pos < n_ctx, 0, n_ctx)
    seg_hi = jnp.where(pos < n_ctx, n_ctx, t_total)
    left = (CONV_W - 1) // 2
    acc = None
    for j in range(CONV_W):
        d = j - left
        xs = x if d == 0 else pltpu.roll(x, (-d) % t_total, 0)
        ok = (pos + d >= seg_lo) & (pos + d < seg_hi)
        term = jnp.where(ok, xs, 0.0) * w_ref[j:j + 1, :]
        acc = term if acc is None else acc + term
    acc = acc + b_ref[...]
    o_ref[...] = _silu(acc) if act else acc


def _dwconv(u, col0, width, w, b, *, n_ctx, act, wt=256):
    bsz, t, _ = u.shape
    c0 = col0 // wt
    return pl.pallas_call(
        functools.partial(_conv_kernel, t_total=t, n_ctx=n_ctx, act=act),
        grid=(bsz, width // wt),
        in_specs=[pl.BlockSpec((None, t, wt), lambda i, j: (i, 0, j + c0)),
                  pl.BlockSpec((CONV_W, wt), lambda i, j: (0, j)),
                  pl.BlockSpec((1, wt), lambda i, j: (0, j))],
        out_specs=pl.BlockSpec((None, t, wt), lambda i, j: (i, 0, j)),
        out_shape=jax.ShapeDtypeStruct((bsz, t, width), F32),
        compiler_params=_params("parallel", "parallel"), name="dwconv",
    )(u, w, b.reshape(1, width))


def _ret_kernel(lg_ref, q_ref, k_ref, v_ref, g_ref, cos_ref, sin_ref, o_ref, qs, ks, acc_f, acc_b,
                *, t_total, n_ctx):
    c = SCAN_CHUNK
    n_chunks, n_ctx_chunks = t_total // c, n_ctx // c
    lg = lg_ref[pl.program_id(1)]
    cos, sin = cos_ref[...], sin_ref[...]
    q, k = q_ref[...], k_ref[...]
    qs[...] = q * cos + pltpu.roll(q, RET_DK // 2, 1) * sin
    ks[...] = (k * cos + pltpu.roll(k, RET_DK // 2, 1) * sin) * (RET_DK ** -0.5)

    ii = lax.broadcasted_iota(jnp.int32, (c, c), 0)
    jj = lax.broadcasted_iota(jnp.int32, (c, c), 1)
    dist = (ii - jj).astype(F32)
    r = lax.broadcasted_iota(jnp.int32, (c, 1), 0).astype(F32)
    decay_all = jnp.exp(c * lg)

    def chunk(off, s_in, fwd):
        qc = qs[pl.ds(off, c), :]
        kc = ks[pl.ds(off, c), :]
        vc = v_ref[pl.ds(off, c), :].astype(BF16)
        if fwd:
            mask, steps, q_pow, k_pow = ii >= jj, dist, r + 1.0, c - 1.0 - r
        else:
            mask, steps, q_pow, k_pow = jj >= ii, -dist, c - r, r
        decay = jnp.where(mask, jnp.exp(jnp.where(mask, steps, 0.0) * lg), 0.0)
        scores = _dot_nt(qc.astype(BF16), kc.astype(BF16)) * decay
        o = _dot(scores.astype(BF16), vc) + jnp.exp(q_pow * lg) * _dot(qc.astype(BF16), s_in.astype(BF16))
        s_out = decay_all * s_in + _dot_tn((kc * jnp.exp(k_pow * lg)).astype(BF16), vc)
        return o, s_out

    def post(off):
        o = acc_f[pl.ds(off, c), :] + acc_b[pl.ds(off, c), :]
        y = o * lax.rsqrt(jnp.mean(o * o, axis=-1, keepdims=True) + EPS)
        o_ref[pl.ds(off, c), :] = (y * _silu(g_ref[pl.ds(off, c), :])).astype(o_ref.dtype)

    _scan_both_ways(n_chunks, n_ctx_chunks, c, lambda off, st, d: chunk(off, st, d == 0), acc_f, acc_b,
                    jnp.zeros((RET_DK, RET_DK), F32), post)


def _retention(u, col0, mix_w, cos2, sin2, *, n_ctx):
    bsz, t, _ = u.shape
    heads = mix_w // RET_DK
    c0 = col0 // RET_DK
    log_gamma = jnp.log1p(-jnp.exp2(-5.0 - jnp.arange(heads, dtype=F32)))

    def spec(k):
        return pl.BlockSpec((None, t, RET_DK), lambda i, h, lg, k=k: (i, 0, c0 + k * heads + h))

    table = pl.BlockSpec((t, RET_DK), lambda i, h, lg: (0, 0))
    return pl.pallas_call(
        functools.partial(_ret_kernel, t_total=t, n_ctx=n_ctx),
        grid_spec=pltpu.PrefetchScalarGridSpec(
            num_scalar_prefetch=1, grid=(bsz, heads),
            in_specs=[spec(0), spec(1), spec(2), spec(3), table, table],
            out_specs=pl.BlockSpec((None, t, RET_DK), lambda i, h, lg: (i, 0, h)),
            scratch_shapes=[pltpu.VMEM((t, RET_DK), F32)] * 4),
        out_shape=jax.ShapeDtypeStruct((bsz, t, mix_w), BF16),
        compiler_params=_params("parallel", "arbitrary"), name="retention",
    )(log_gamma, u, u, u, u, cos2, sin2)


def _ssd_kernel(xs_ref, bm_ref, cm_ref, z_ref, dt_ref, dtb_ref, alog_ref, dsk_ref, nw_ref, sel_ref, ord_ref, o_ref,
                acc_f, acc_b, dts, las, *, t_total, n_ctx):
    c = ord_ref.shape[1]
    rep = sel_ref.shape[1] // 2
    gw = rep * SSD_HD
    n_chunks, n_ctx_chunks = t_total // c, n_ctx // c
    dt_all = _softplus(dt_ref[...] + dtb_ref[...])
    dts[...] = dt_all
    las[...] = dt_all * (-jnp.exp(alog_ref[...]))

    ii = lax.broadcasted_iota(jnp.int32, (c, c), 0)
    jj = lax.broadcasted_iota(jnp.int32, (c, c), 1)
    lane_head = lax.broadcasted_iota(jnp.int32, (1, gw), 1) // SSD_HD

    def chunk(off, s_in, d):
        fwd = d == 0
        mask = (ii >= jj) if fwd else (jj >= ii)
        la_parts = _split_bf16(las[pl.ds(off, c), :], 3)
        cum = sum(_dot(ord_ref[d], p) for p in la_parts)
        cum_rows = sum(_dot_tn(p, ord_ref[1 - d]) for p in la_parts)
        sel = sel_ref[d]
        cum_e = sum(_dot(p, sel) for p in _split_bf16(cum, 3))
        dt_e = sum(_dot(p, sel) for p in _split_bf16(dts[pl.ds(off, c), :], 3))
        last_e = cum_e[c - 1:c, :] if fwd else cum_e[0:1, :]
        xs = xs_ref[pl.ds(off, c), :]
        vdt = xs * dt_e
        cm = cm_ref[pl.ds(off, c), :].astype(BF16)
        bm = bm_ref[pl.ds(off, c), :].astype(BF16)
        qk = _dot_nt(cm, bm)
        o = jnp.exp(cum_e) * _dot(cm, s_in.astype(BF16))
        for i in range(rep):
            col = d * rep + i
            diff = cum[:, col:col + 1] - cum_rows[col:col + 1, :]
            decay = jnp.where(mask, jnp.exp(jnp.where(mask, diff, 0.0)), 0.0)
            o = o + _dot((qk * decay).astype(BF16), jnp.where(lane_head == i, vdt, 0.0).astype(BF16))
        s_out = jnp.exp(last_e) * s_in + _dot_tn(bm, (vdt * jnp.exp(last_e - cum_e)).astype(BF16))
        return o, s_out

    def post(off):
        o = acc_f[pl.ds(off, c), :] + acc_b[pl.ds(off, c), :]
        y = (o + dsk_ref[...] * xs_ref[pl.ds(off, c), :]) * _silu(z_ref[pl.ds(off, c), :])
        y = y * lax.rsqrt(jnp.mean(y * y, axis=-1, keepdims=True) + EPS)
        o_ref[pl.ds(off, c), :] = (y * nw_ref[...]).astype(o_ref.dtype)

    _scan_both_ways(n_chunks, n_ctx_chunks, c, chunk, acc_f, acc_b, jnp.zeros((SSD_STATE, gw), F32), post)


def _ssd(u_z, z_col0, xact, u_dt, dt_bias, a_log, d_skip, norm_w, mix_w, *, n_ctx):
    bsz, t, _ = xact.shape
    heads = mix_w // SSD_HD
    rep = heads // SSD_GROUPS
    gw = rep * SSD_HD
    def per_group(v):
        lead = v.shape[:-1]
        v = v.reshape(*lead, 2, SSD_GROUPS, rep)
        return jnp.moveaxis(v, -2, 0).reshape(SSD_GROUPS, *lead, 2 * rep)
    dt_g = jnp.moveaxis(per_group(u_dt), 0, 1)
    dtb_g = per_group(dt_bias.reshape(1, 2 * heads))
    alog_g = per_group(a_log.reshape(1, 2 * heads))
    sel = np.zeros((2, 2 * rep, gw), np.float32)
    for d in range(2):
        for i in range(rep):
            sel[d, d * rep + i, i * SSD_HD:(i + 1) * SSD_HD] = 1.0
    dsk = jnp.repeat(d_skip, SSD_HD).reshape(1, mix_w)
    zc, bc, cc = z_col0 // gw, mix_w // SSD_STATE, mix_w // SSD_STATE + SSD_GROUPS
    small = lambda: pl.BlockSpec((None, 1, 2 * rep), lambda i, g: (g, 0, 0))
    return pl.pallas_call(
        functools.partial(_ssd_kernel, t_total=t, n_ctx=n_ctx),
        grid=(bsz, SSD_GROUPS),
        in_specs=[pl.BlockSpec((None, t, gw), lambda i, g: (i, 0, g)),
                  pl.BlockSpec((None, t, SSD_STATE), lambda i, g: (i, 0, bc + g)),
                  pl.BlockSpec((None, t, SSD_STATE), lambda i, g: (i, 0, cc + g)),
                  pl.BlockSpec((None, t, gw), lambda i, g: (i, 0, zc + g)),
                  pl.BlockSpec((None, None, t, 2 * rep), lambda i, g: (i, g, 0, 0)),
                  small(), small(),
                  pl.BlockSpec((1, gw), lambda i, g: (0, g)),
                  pl.BlockSpec((1, gw), lambda i, g: (0, g)),
                  pl.BlockSpec((2, 2 * rep, gw), lambda i, g: (0, 0, 0)),
                  pl.BlockSpec((2, SSD_CHUNK, SSD_CHUNK), lambda i, g: (0, 0, 0))],
        out_specs=pl.BlockSpec((None, t, gw), lambda i, g: (i, 0, g)),
        out_shape=jax.ShapeDtypeStruct((bsz, t, mix_w), BF16),
        scratch_shapes=[pltpu.VMEM((t, gw), F32), pltpu.VMEM((t, gw), F32),
                        pltpu.VMEM((t, 2 * rep), F32), pltpu.VMEM((t, 2 * rep), F32)],
        compiler_params=_params("parallel", "arbitrary"), name="ssd",
    )(xact, xact, xact, u_z, dt_g, dtb_g, alog_g, dsk, norm_w.reshape(1, mix_w), jnp.asarray(sel, BF16),
      _scan_order_tables(SSD_CHUNK))


def _tile_scan(a, b, row, reverse):
    for s in (1, 2, 4):
        if reverse:
            ok, shift = row < SUBLANES - s, SUBLANES - s
        else:
            ok, shift = row >= s, s
        b = b + a * jnp.where(ok, pltpu.roll(b, shift, 0), 0.0)
        a = a * jnp.where(ok, pltpu.roll(a, shift, 0), 1.0)
    return a, b


def _lru_kernel(xc_ref, y_ref, w_ref, bias_ref, lam_ref, o_ref, acc, a_f, b_f, a_b, b_b, *, t_total, n_ctx):
    c = SCAN_CHUNK
    nb, bw = w_ref.shape[0], w_ref.shape[1]
    n_chunks, n_ctx_chunks = t_total // c, n_ctx // c
    n_tiles = c // SUBLANES
    log_sig = -_softplus(-lam_ref[...])
    acc[...] = jnp.zeros_like(acc)
    row = lax.broadcasted_iota(jnp.int32, (SUBLANES, bw), 0)

    def gates(off, d, dst_a, dst_b):
        for blk in range(nb):
            cols = slice(blk * bw, (blk + 1) * bw)
            x = xc_ref[pl.ds(off, c), cols]
            g = _dot(x.astype(BF16), w_ref[blk, :, d * 2 * bw:(d + 1) * 2 * bw]) \
                + bias_ref[blk, :, d * 2 * bw:(d + 1) * 2 * bw]
            log_a = LRU_C * _sigmoid(g[:, :bw]) * log_sig[d:d + 1, cols]
            a = jnp.exp(log_a)
            dst_a[:, cols] = a
            dst_b[:, cols] = jnp.sqrt(-jnp.tanh(log_a) * (a * a + 1.0)) * (_sigmoid(g[:, bw:]) * x)

    def step(s, carry):
        off_f = pl.multiple_of(s * c, c)
        off_b = pl.multiple_of(_bwd_chunk(s, n_chunks, n_ctx_chunks) * c, c)
        gates(off_f, 0, a_f, b_f)
        gates(off_b, 1, a_b, b_b)

        def tile(t, carry):
            h_f, h_b = carry
            r_f = pl.multiple_of(t * SUBLANES, SUBLANES)
            r_b = pl.multiple_of((n_tiles - 1 - t) * SUBLANES, SUBLANES)
            new_f, new_b = [], []
            for blk in range(nb):
                cols = slice(blk * bw, (blk + 1) * bw)
                a, b = _tile_scan(a_f[pl.ds(r_f, SUBLANES), cols], b_f[pl.ds(r_f, SUBLANES), cols], row, False)
                h = b + a * h_f[blk]
                acc[pl.ds(off_f + r_f, SUBLANES), cols] += h
                new_f.append(h[SUBLANES - 1:SUBLANES, :])
                a, b = _tile_scan(a_b[pl.ds(r_b, SUBLANES), cols], b_b[pl.ds(r_b, SUBLANES), cols], row, True)
                h = b + a * h_b[blk]
                acc[pl.ds(off_b + r_b, SUBLANES), cols] += h
                new_b.append(h[0:1, :])
            return tuple(new_f), tuple(new_b)

        return lax.fori_loop(0, n_tiles, tile, carry)

    zero = tuple(jnp.zeros((1, bw), F32) for _ in range(nb))
    lax.fori_loop(0, n_chunks, step, (zero, zero))

    def finish(s, _):
        off = pl.multiple_of(s * c, c)
        o_ref[pl.ds(off, c), :] = (acc[pl.ds(off, c), :] * _gelu_tanh(y_ref[pl.ds(off, c), :])).astype(o_ref.dtype)
        return 0

    lax.fori_loop(0, n_chunks, finish, 0)


def _rglru(xc, u_y, y_col0, wa, ba, wx, bx, lam, mix_w, *, n_ctx, blocks_per_step=4):
    bsz, t, _ = xc.shape
    bw = mix_w // LRU_BLOCKS
    gw = blocks_per_step * bw
    w = jnp.concatenate([wa[0], wx[0], wa[1], wx[1]], axis=-1).astype(BF16)
    blk = lambda v: v.reshape(2, LRU_BLOCKS, 1, bw)
    bias = jnp.concatenate([blk(ba)[0], blk(bx)[0], blk(ba)[1], blk(bx)[1]], axis=-1)
    yc = y_col0 // gw
    return pl.pallas_call(
        functools.partial(_lru_kernel, t_total=t, n_ctx=n_ctx),
        grid=(bsz, LRU_BLOCKS // blocks_per_step),
        in_specs=[pl.BlockSpec((None, t, gw), lambda i, j: (i, 0, j)),
                  pl.BlockSpec((None, t, gw), lambda i, j: (i, 0, yc + j)),
                  pl.BlockSpec((blocks_per_step, bw, 4 * bw), lambda i, j: (j, 0, 0)),
                  pl.BlockSpec((blocks_per_step, 1, 4 * bw), lambda i, j: (j, 0, 0)),
                  pl.BlockSpec((2, gw), lambda i, j: (0, j))],
        out_specs=pl.BlockSpec((None, t, gw), lambda i, j: (i, 0, j)),
        out_shape=jax.ShapeDtypeStruct((bsz, t, mix_w), BF16),
        scratch_shapes=[pltpu.VMEM((t, gw), F32)] + [pltpu.VMEM((SCAN_CHUNK, gw), F32)] * 4,
        compiler_params=_params("parallel", "arbitrary"), name="rglru",
    )(xc, u_y, w, bias, lam)


def _hg_tables():
    c = HG_CHUNK
    idx = np.arange(c)
    dif = np.zeros((2, HG_LEVELS, c, c), np.float32)
    isq = np.zeros((2, HG_LEVELS, c, HG_DK), np.float32)
    msk = np.zeros((2, HG_LEVELS, c, c), np.float32)
    for lvl in range(HG_LEVELS):
        half = 1 << lvl
        mid = (idx // (2 * half)) * 2 * half + half
        late = idx >= mid
        same = (idx[:, None] // (2 * half)) == (idx[None, :] // (2 * half))
        for d, boundary, is_query in ((0, mid - 1, late), (1, mid, ~late)):
            sign = np.where(is_query, 1.0, -1.0)
            dif[d, lvl, idx, idx] += sign
            dif[d, lvl, idx, boundary] -= sign
            isq[d, lvl] = is_query[:, None]
        msk[0, lvl] = same & late[:, None] & ~late[None, :]
        msk[1, lvl] = same & ~late[:, None] & late[None, :]
    lower = (idx[None, :] <= idx[:, None]).astype(np.float32)
    order = np.stack([lower, lower.T])
    rep3 = lambda v: jnp.asarray(np.concatenate([v, v, v], axis=-1), BF16)
    return rep3(order), rep3(dif.reshape(2, HG_LEVELS * c, c)), jnp.asarray(isq), jnp.asarray(msk)


def _hg_kernel(q_ref, f0_ref, f1_ref, v_ref, g_ref, lb_ref, nw_ref, ord_ref, dif_ref, isq_ref, msk_ref,
               o_ref, acc_f, acc_b, *, t_total, n_ctx):
    c = HG_CHUNK
    n_chunks, n_ctx_chunks = t_total // c, n_ctx // c
    eye = lax.broadcasted_iota(jnp.int32, (c, c), 0) == lax.broadcasted_iota(jnp.int32, (c, c), 1)

    def chunk(off, st, d):
        fwd = d == 0
        lb = lb_ref[d:d + 1, :]
        f = (f0_ref if fwd else f1_ref)[pl.ds(off, c), :]
        log_f = jnp.log(lb + (1.0 - lb) * _sigmoid(f))
        k = (1.0 - lb) * _sigmoid(-f)
        q = _silu(q_ref[pl.ds(off, c), :])
        v = v_ref[pl.ds(off, c), :].astype(BF16)
        cum = _dot(ord_ref[d], jnp.concatenate(_split_bf16(log_f, 3), axis=0))
        expo = _dot(dif_ref[d], jnp.concatenate(_split_bf16(cum, 3), axis=0))
        q_minus_k = q - k
        scores = jnp.where(eye, _dot_nt(q.astype(BF16), k.astype(BF16)), 0.0)
        for lvl in range(HG_LEVELS):
            rows = slice(lvl * c, (lvl + 1) * c)
            y = ((k + isq_ref[d, lvl] * q_minus_k) * jnp.exp(expo[rows, :])).astype(BF16)
            scores = scores + _dot_nt(y, y) * msk_ref[d, lvl]
        last = cum[c - 1:c, :] if fwd else cum[0:1, :]
        o = _dot(scores.astype(BF16), v) + _dot_nt((q * jnp.exp(cum)).astype(BF16), st.astype(BF16))
        st = st * jnp.exp(last) + _dot_tn(v, (k * jnp.exp(last - cum)).astype(BF16))
        return o, st

    def post(off):
        o = acc_f[pl.ds(off, c), :] + acc_b[pl.ds(off, c), :]
        y = o * lax.rsqrt(jnp.mean(o * o, axis=-1, keepdims=True) + EPS) * nw_ref[...]
        o_ref[pl.ds(off, c), :] = (y * _silu(g_ref[pl.ds(off, c), :])).astype(o_ref.dtype)

    _scan_both_ways(n_chunks, n_ctx_chunks, c, chunk, acc_f, acc_b, jnp.zeros((HG_DK, HG_DK), F32), post)


def _hgrn2(u, col0, mix_w, lower, norm_w, *, n_ctx):
    bsz, t, _ = u.shape
    heads = mix_w // HG_DK
    c0 = col0 // HG_DK
    tables = _hg_tables()

    def spec(k):
        return pl.BlockSpec((None, t, HG_DK), lambda i, h, k=k: (i, 0, c0 + k * heads + h))

    def table(v):
        return pl.BlockSpec(v.shape, lambda i, h, nd=v.ndim: (0,) * nd)

    return pl.pallas_call(
        functools.partial(_hg_kernel, t_total=t, n_ctx=n_ctx),
        grid=(bsz, heads),
        in_specs=[spec(0), spec(1), spec(2), spec(3), spec(4),
                  pl.BlockSpec((2, HG_DK), lambda i, h: (0, h)),
                  pl.BlockSpec((1, HG_DK), lambda i, h: (0, h))] + [table(v) for v in tables],
        out_specs=pl.BlockSpec((None, t, HG_DK), lambda i, h: (i, 0, h)),
        out_shape=jax.ShapeDtypeStruct((bsz, t, mix_w), BF16),
        scratch_shapes=[pltpu.VMEM((t, HG_DK), F32)] * 2,
        compiler_params=_params("parallel", "arbitrary"), name="hgrn2",
    )(u, u, u, u, u, lower, norm_w.reshape(1, mix_w), *tables)


def _ada_kernel(c_ref, w_ref, b_ref, o_ref):
    o_ref[...] = _dot(_silu(c_ref[...]).astype(BF16), w_ref[...].astype(BF16)) + b_ref[...]


def _ada(cond, ada_w, ada_b, tn=1024):
    r, d = cond.shape
    depth, _, n = ada_w.shape
    return pl.pallas_call(
        _ada_kernel,
        grid=(depth, n // tn),
        in_specs=[pl.BlockSpec((r, d), lambda l, j: (0, 0)),
                  pl.BlockSpec((None, d, tn), lambda l, j: (l, 0, j)),
                  pl.BlockSpec((None, 1, tn), lambda l, j: (l, 0, j))],
        out_specs=pl.BlockSpec((None, r, tn), lambda l, j: (l, 0, j)),
        out_shape=jax.ShapeDtypeStruct((depth, r, n), F32),
        compiler_params=_params("parallel", "parallel"), name="ada",
    )(cond, ada_w, ada_b.reshape(depth, 1, n))


def _rope_tables(n_ctx, n_lat):
    r, col = jnp.meshgrid(jnp.arange(n_lat // GRID_W), jnp.arange(GRID_W), indexing='ij')
    n_freq = RET_DK // 4
    freqs = ROPE_BASE ** (-jnp.arange(n_freq, dtype=F32) / n_freq)
    ang = jnp.concatenate([r.reshape(-1, 1) * freqs, col.reshape(-1, 1) * freqs], axis=-1)
    cos, sin = jnp.cos(ang), jnp.sin(ang)
    cos2 = jnp.concatenate([jnp.ones((n_ctx, RET_DK), F32), jnp.concatenate([cos, cos], axis=-1)], axis=0)
    sin2 = jnp.concatenate([jnp.zeros((n_ctx, RET_DK), F32), jnp.concatenate([-sin, sin], axis=-1)], axis=0)
    return cos2, sin2


def kernel(x, c, ctx, c_ctx, ada_w, ada_b, norm1_w, norm2_w, w_in, ssd_conv_w, ssd_conv_b, ssd_a_log, ssd_dt_bias, ssd_d, ssd_norm_w, lru_conv_w, lru_conv_b, lru_wa, lru_ba, lru_wx, lru_bx, lru_lambda, hg_lb_logits, hg_norm_w, w_branch, w_out, ffn_wgu, ffn_w2, router_w, moe_wgu, moe_w2, final_norm_w):
    bsz, n_lat, d = x.shape
    n_ctx = ctx.shape[1]
    t = n_ctx + n_lat
    depth = w_in.shape[0]
    mix_w = d // 2
    ssd_heads = mix_w // SSD_HD
    xbc_w = mix_w + 2 * SSD_GROUPS * SSD_STATE
    ret_w, ssd_w, dt_w, lru_w, hg_w, gate_w = 4 * mix_w, mix_w + xbc_w, 2 * ssd_heads, 2 * mix_w, 5 * mix_w, N_BRANCH * d
    dt_col = ret_w + ssd_w
    w_lo = w_in
    w_hi = w_in[:, :, dt_col + dt_w:]
    w_branch_bf, w_out_bf, ffn_w2_bf = w_branch.astype(BF16), w_out.astype(BF16), ffn_w2.astype(BF16)

    cos2, sin2 = _rope_tables(n_ctx, n_lat)
    p_lb = jax.nn.softmax(hg_lb_logits.astype(F32), axis=1)
    lower_bounds = jnp.cumsum(p_lb, axis=1) - p_lb[:, :1]

    pad = (-(1 + bsz)) % SUBLANES
    cond = jnp.concatenate([c_ctx[None], c, jnp.zeros((pad, d), F32)], axis=0)
    mods = _ada(cond, ada_w, ada_b).reshape(depth, -1, N_MOD, d)

    h = jnp.concatenate([ctx.astype(x.dtype), x], axis=1)
    for layer in range(depth):
        m_ctx = jnp.broadcast_to(mods[layer, 0], (bsz, N_MOD, d))
        m_lat = mods[layer, 1:1 + bsz]
        mod_rows = jnp.stack([m_ctx, m_lat], axis=1)
        mod_cols = jnp.stack([m_ctx, m_lat], axis=2)

        xn = _norm_call(h, norm1_w[layer], mod_rows, shift=0, scale=1, n_ctx=n_ctx).reshape(bsz * t, d)
        u_ret = _matmul(xn, w_lo, layer, 0, ret_w).reshape(bsz, t, ret_w)
        u_ssd = _matmul(xn, w_lo, layer, ret_w, ssd_w).reshape(bsz, t, ssd_w)
        u_dt = _matmul(xn, w_lo, layer, dt_col, LANES)[:, :dt_w].reshape(bsz, t, dt_w)
        u_lru = _matmul(xn, w_hi, layer, 0, lru_w).reshape(bsz, t, lru_w)
        u_hg = _matmul(xn, w_hi, layer, lru_w, hg_w).reshape(bsz, t, hg_w)
        u_gates = _matmul(xn, w_hi, layer, lru_w + hg_w, gate_w, out_dtype=BF16)

        b_ret = _retention(u_ret, 0, mix_w, cos2, sin2, n_ctx=n_ctx)
        xact = _dwconv(u_ssd, mix_w, xbc_w, ssd_conv_w[layer], ssd_conv_b[layer], n_ctx=n_ctx, act=True)
        b_ssd = _ssd(u_ssd, 0, xact, u_dt, ssd_dt_bias[layer], ssd_a_log[layer], ssd_d[layer],
                     ssd_norm_w[layer], mix_w, n_ctx=n_ctx)
        xc = _dwconv(u_lru, 0, mix_w, lru_conv_w[layer], lru_conv_b[layer], n_ctx=n_ctx, act=False)
        b_lru = _rglru(xc, u_lru, mix_w, lru_wa[layer], lru_ba[layer], lru_wx[layer], lru_bx[layer],
                       lru_lambda[layer], mix_w, n_ctx=n_ctx)
        b_hg = _hgrn2(u_hg, 0, mix_w, lower_bounds[:, layer], hg_norm_w[layer], n_ctx=n_ctx)

        branches = [v.reshape(bsz * t, mix_w) for v in (b_ret, b_ssd, b_lru, b_hg)]
        mixed = _merge(branches, u_gates, w_branch_bf, layer, tm=768)
        h2 = _matmul_residual(mixed, w_out_bf, layer, h.reshape(bsz * t, d), mod_cols, 2, t_total=t, n_ctx=n_ctx)

        if layer % 2 == 0:
            xn = _norm_call(h2.reshape(bsz, t, d), norm2_w[layer], mod_rows, shift=3, scale=4, n_ctx=n_ctx)
            act = _swiglu_up(xn.reshape(bsz * t, d), ffn_wgu, layer // 2)
            h = _matmul_residual(act, ffn_w2_bf, layer // 2, h2, mod_cols, 5, t_total=t, n_ctx=n_ctx,
                                 tm=768, tn=512, tk=act.shape[1]).reshape(bsz, t, d)
        else:
            xn, gates = _norm_call(h2.reshape(bsz, t, d), norm2_w[layer], mod_rows, shift=3, scale=4, n_ctx=n_ctx,
                                   router_w=router_w[layer // 2], out_dtype=F32)
            h = _moe_block(xn, gates, h2.reshape(bsz, t, d), moe_wgu, moe_w2, layer // 2, mod_rows, 5, n_ctx=n_ctx)
    return _final_norm_call(h, final_norm_w, n_ctx)
```

```python
import functools
import math

import numpy as np
import jax
import jax.numpy as jnp
from jax import lax
from jax.experimental import pallas as pl
from jax.experimental.pallas import tpu as pltpu

F32 = jnp.float32
BF16 = jnp.bfloat16
HI = lax.Precision.HIGHEST

EPS = 1e-6
GRID_W = 64
ROPE_BASE = 10000.0
RET_DK = 128
SSD_HD = 64
SSD_GROUPS = 4
SSD_STATE = 128
CONV_W = 4
LRU_BLOCKS = 8
LRU_C = 8.0
HG_DK = 128
N_BRANCH = 4
N_EXPERTS = 8
N_MOD = 6

LANES = 128
SUBLANES = 8
VMEM_LIMIT = 52 * 1024 * 1024

SCAN_CHUNK = 256
SSD_CHUNK = 256
HG_CHUNK = 128
HG_LEVELS = 7


def _params(*sem):
    return pltpu.CompilerParams(dimension_semantics=sem, vmem_limit_bytes=VMEM_LIMIT)


def _dot(a, b, prec=None):
    return jnp.dot(a, b, preferred_element_type=F32, precision=prec)


def _dot_nt(a, b, prec=None):
    return lax.dot_general(a, b, (((1,), (1,)), ((), ())), preferred_element_type=F32, precision=prec)


def _dot_tn(a, b, prec=None):
    return lax.dot_general(a, b, (((0,), (0,)), ((), ())), preferred_element_type=F32, precision=prec)


def _sigmoid(x):
    return 1.0 / (1.0 + jnp.exp(-x))


def _silu(x):
    return x * _sigmoid(x)


def _softplus(x):
    return jnp.maximum(x, 0.0) + jnp.log1p(jnp.exp(-jnp.abs(x)))


def _gelu_tanh(x):
    return 0.5 * x * (1.0 + jnp.tanh(math.sqrt(2.0 / math.pi) * (x + 0.044715 * (x * x * x))))


def _bwd_chunk(s, n_chunks, n_ctx_chunks):
    return jnp.where(s < n_ctx_chunks, n_ctx_chunks - 1 - s, n_chunks - 1 - (s - n_ctx_chunks))


def _scan_both_ways(n_chunks, n_ctx_chunks, c, chunk_fn, acc_f, acc_b, zero_state, post_fn):
    def body(s, carry):
        st_f, st_b = carry
        off_f = pl.multiple_of(s * c, c)
        off_b = pl.multiple_of(_bwd_chunk(s, n_chunks, n_ctx_chunks) * c, c)
        o_f, st_f = chunk_fn(off_f, st_f, 0)
        o_b, st_b = chunk_fn(off_b, st_b, 1)
        acc_f[pl.ds(off_f, c), :] = o_f
        acc_b[pl.ds(off_b, c), :] = o_b
        return st_f, st_b

    lax.fori_loop(0, n_chunks, body, (zero_state, zero_state), unroll=3)

    def finish(s, _):
        post_fn(pl.multiple_of(s * c, c))
        return 0

    lax.fori_loop(0, n_chunks, finish, 0)


def _scan_order_tables(c):
    idx = np.arange(c)
    lower = (idx[None, :] <= idx[:, None]).astype(np.float32)
    return jnp.asarray(np.stack([lower, lower.T]), BF16)


def _split_bf16(x, parts):
    out = []
    for _ in range(parts):
        p = x.astype(BF16)
        out.append(p)
        x = x - p.astype(F32)
    return out


def _norm_kernel(h_ref, w_ref, mod_ref, o_ref, *, shift, scale):
    x = h_ref[...]
    y = x * lax.rsqrt(jnp.mean(x * x, axis=-1, keepdims=True) + EPS) * w_ref[...]
    o_ref[...] = (y * (1.0 + mod_ref[scale:scale + 1, :]) + mod_ref[shift:shift + 1, :]).astype(o_ref.dtype)


def _norm_router_kernel(h_ref, w_ref, mod_ref, rw_ref, o_ref, gate_ref, *, shift, scale):
    x = h_ref[...]
    y = x * lax.rsqrt(jnp.mean(x * x, axis=-1, keepdims=True) + EPS) * w_ref[...]
    xn = y * (1.0 + mod_ref[scale:scale + 1, :]) + mod_ref[shift:shift + 1, :]
    o_ref[...] = xn.astype(o_ref.dtype)
    logits = _dot(xn.astype(BF16), rw_ref[...])
    lane = lax.broadcasted_iota(jnp.int32, logits.shape, 1)
    neg = jnp.float32(-jnp.inf)
    logits = jnp.where(lane < N_EXPERTS, logits, neg)
    v1 = jnp.max(logits, axis=-1, keepdims=True)
    i1 = jnp.min(jnp.where(logits == v1, lane, LANES), axis=-1, keepdims=True)
    rest = jnp.where(lane == i1, neg, logits)
    v2 = jnp.max(rest, axis=-1, keepdims=True)
    i2 = jnp.min(jnp.where(rest == v2, lane, LANES), axis=-1, keepdims=True)
    e2 = jnp.exp(v2 - v1)
    w1 = 1.0 / (1.0 + e2)
    w2 = e2 / (1.0 + e2)
    gate_ref[...] = (jnp.where(lane == i1, w1, 0.0) + jnp.where(lane == i2, w2, 0.0)
                     + jnp.where(lane == N_EXPERTS, i1.astype(F32), 0.0)
                     + jnp.where(lane == N_EXPERTS + 1, i2.astype(F32), 0.0))


def _norm_call(h, w, mod, *, shift, scale, n_ctx, router_w=None, out_dtype=BF16):
    b, t, d = h.shape
    tr = n_ctx
    grid = (b, t // tr)
    h_spec = pl.BlockSpec((None, tr, d), lambda i, j: (i, j, 0))
    w_spec = pl.BlockSpec((1, d), lambda i, j: (0, 0))
    mod_spec = pl.BlockSpec((None, None, N_MOD, d), lambda i, j: (i, jnp.minimum(j, 1), 0, 0))
    if router_w is None:
        return pl.pallas_call(
            functools.partial(_norm_kernel, shift=shift, scale=scale),
            grid=grid, in_specs=[h_spec, w_spec, mod_spec], out_specs=h_spec,
            out_shape=jax.ShapeDtypeStruct(h.shape, out_dtype),
            compiler_params=_params("parallel", "parallel"), name="norm_mod",
        )(h, w.reshape(1, d), mod)
    rw = jnp.zeros((d, LANES), BF16).at[:, :N_EXPERTS].set(router_w.astype(BF16))
    return pl.pallas_call(
        functools.partial(_norm_router_kernel, shift=shift, scale=scale),
        grid=grid,
        in_specs=[h_spec, w_spec, mod_spec, pl.BlockSpec((d, LANES), lambda i, j: (0, 0))],
        out_specs=[h_spec, pl.BlockSpec((None, tr, LANES), lambda i, j: (i, j, 0))],
        out_shape=[jax.ShapeDtypeStruct(h.shape, out_dtype), jax.ShapeDtypeStruct((b, t, LANES), F32)],
        compiler_params=_params("parallel", "parallel"), name="norm_mod_router",
    )(h, w.reshape(1, d), mod, rw)


def _final_norm_kernel(h_ref, w_ref, o_ref):
    x = h_ref[...]
    o_ref[...] = x * lax.rsqrt(jnp.mean(x * x, axis=-1, keepdims=True) + EPS) * w_ref[...]


def _final_norm_call(h, w, n_ctx):
    b, t, d = h.shape
    tr = n_ctx
    return pl.pallas_call(
        _final_norm_kernel,
        grid=(b, (t - n_ctx) // tr),
        in_specs=[pl.BlockSpec((None, tr, d), lambda i, j: (i, j + 1, 0)), pl.BlockSpec((1, d), lambda i, j: (0, 0))],
        out_specs=pl.BlockSpec((None, tr, d), lambda i, j: (i, j, 0)),
        out_shape=jax.ShapeDtypeStruct((b, t - n_ctx, d), F32),
        compiler_params=_params("parallel", "parallel"), name="final_norm",
    )(h, w.reshape(1, d))


def _pick(n, prefs):
    for p in prefs:
        if n % p == 0:
            return p
    return n


def _mm_kernel(a_ref, w_ref, o_ref, w_bf):
    @pl.when(pl.program_id(1) == 0)
    def _():
        w_bf[...] = w_ref[...].astype(BF16)

    o_ref[...] = _dot(a_ref[...], w_bf[...]).astype(o_ref.dtype)


def _mm_shifted_kernel(a_ref, w_ref, tail_ref, o_ref, w_bf, *, shift):
    @pl.when(pl.program_id(1) == 0)
    def _():
        w_bf[...] = jnp.concatenate([w_ref[:, shift:], tail_ref[:, :shift]], axis=1).astype(BF16)

    o_ref[...] = _dot(a_ref[...], w_bf[...]).astype(o_ref.dtype)


def _matmul(a, w, layer, col0, n, out_dtype=F32, tm=None, tn=None):
    m, kd = a.shape
    tm = tm or _pick(m, (1152, 1024, 768, 512, 256))
    shift = col0 % LANES
    base = col0 - shift
    tn = tn or _pick(math.gcd(n, base) if base else n, (1024, 512, 256, 128))
    c0 = base // tn
    if shift:
        lanes_per_tile = tn // LANES
        return pl.pallas_call(
            functools.partial(_mm_shifted_kernel, shift=shift),
            grid=(n // tn, m // tm),
            in_specs=[pl.BlockSpec((tm, kd), lambda j, i: (i, 0)),
                      pl.BlockSpec((None, kd, tn), lambda j, i: (layer, 0, c0 + j)),
                      pl.BlockSpec((None, kd, LANES), lambda j, i: (layer, 0, (c0 + j + 1) * lanes_per_tile))],
            out_specs=pl.BlockSpec((tm, tn), lambda j, i: (i, j)),
            out_shape=jax.ShapeDtypeStruct((m, n), out_dtype),
            scratch_shapes=[pltpu.VMEM((kd, tn), BF16)],
            compiler_params=_params("arbitrary", "arbitrary"), name="matmul_shifted",
        )(a, w, w)
    return pl.pallas_call(
        _mm_kernel,
        grid=(n // tn, m // tm),
        in_specs=[pl.BlockSpec((tm, kd), lambda j, i: (i, 0)),
                  pl.BlockSpec((None, kd, tn), lambda j, i: (layer, 0, c0 + j))],
        out_specs=pl.BlockSpec((tm, tn), lambda j, i: (i, j)),
        out_shape=jax.ShapeDtypeStruct((m, n), out_dtype),
        scratch_shapes=[pltpu.VMEM((kd, tn), BF16)],
        compiler_params=_params("arbitrary", "arbitrary"), name="matmul",
    )(a, w)


def _row_is_ctx(tm, t_total, n_ctx):
    row0 = pl.program_id(1) * tm
    pos = (row0 + lax.broadcasted_iota(jnp.int32, (tm, 1), 0)) % t_total
    return pos < n_ctx


def _mm_res_kernel(a_ref, w_ref, res_ref, mod_ref, o_ref, acc_ref, *, nk, tm, t_total, n_ctx):
    k = pl.program_id(2)
    part = _dot(a_ref[...], w_ref[...])

    def finish(total):
        mod = jnp.where(_row_is_ctx(tm, t_total, n_ctx), mod_ref[0:1, :], mod_ref[1:2, :])
        o_ref[...] = res_ref[...] + mod * total

    if nk == 1:
        finish(part)
        return

    @pl.when(k == 0)
    def _():
        acc_ref[...] = part

    @pl.when(k > 0)
    def _():
        acc_ref[...] += part

    @pl.when(k == nk - 1)
    def _():
        finish(acc_ref[...])


def _matmul_residual(a, w, layer, res, mod, mod_idx, *, t_total, n_ctx, tm=None, tn=None, tk=None):
    m, kd = a.shape
    n = w.shape[2]
    tm = tm or _pick(t_total, (1152, 768, 256))
    tn = tn or _pick(n, (1024, 512, 256, 128))
    tk = tk or (kd if kd <= 2048 else _pick(kd, (512, 256, 128)))
    nk = kd // tk
    per_b = t_total // tm
    return pl.pallas_call(
        functools.partial(_mm_res_kernel, nk=nk, tm=tm, t_total=t_total, n_ctx=n_ctx),
        grid=(n // tn, m // tm, nk),
        in_specs=[pl.BlockSpec((tm, tk), lambda j, i, k: (i, k)),
                  pl.BlockSpec((None, tk, tn), lambda j, i, k: (layer, k, j)),
                  pl.BlockSpec((tm, tn), lambda j, i, k: (i, j)),
                  pl.BlockSpec((None, None, 2, tn), lambda j, i, k: (i // per_b, mod_idx, 0, j))],
        out_specs=pl.BlockSpec((tm, tn), lambda j, i, k: (i, j)),
        out_shape=jax.ShapeDtypeStruct((m, n), F32),
        scratch_shapes=[pltpu.VMEM((tm, tn), F32)],
        compiler_params=_params("parallel", "parallel", "arbitrary"), name="matmul_residual",
    )(a, w, res, mod)


def _swiglu_kernel(a_ref, wg_ref, wu_ref, o_ref, wg_bf, wu_bf):
    @pl.when(pl.program_id(1) == 0)
    def _():
        wg_bf[...] = wg_ref[...].astype(BF16)
        wu_bf[...] = wu_ref[...].astype(BF16)

    a = a_ref[...]
    o_ref[...] = (_silu(_dot(a, wg_bf[...])) * _dot(a, wu_bf[...])).astype(o_ref.dtype)


def _swiglu_up(a, wgu, layer, tm=None, tn=None):
    m, kd = a.shape
    f = wgu.shape[2] // 2
    tm = tm or _pick(m, (1152, 1024, 768, 512, 256))
    tn = tn or _pick(f, (512, 256, 128))
    nf = f // tn
    return pl.pallas_call(
        _swiglu_kernel,
        grid=(nf, m // tm),
        in_specs=[pl.BlockSpec((tm, kd), lambda j, i: (i, 0)),
                  pl.BlockSpec((None, kd, tn), lambda j, i: (layer, 0, j)),
                  pl.BlockSpec((None, kd, tn), lambda j, i: (layer, 0, j + nf))],
        out_specs=pl.BlockSpec((tm, tn), lambda j, i: (i, j)),
        out_shape=jax.ShapeDtypeStruct((m, f), BF16),
        scratch_shapes=[pltpu.VMEM((kd, tn), BF16)] * 2,
        compiler_params=_params("arbitrary", "arbitrary"), name="swiglu_up",
    )(a, wgu, wgu)


MOE_TM = 512
GATHER_ROWS = 256


def _route(gates, n_tok):
    tm = MOE_TM
    n_slot = 2 * n_tok
    n_tiles = n_slot // tm + N_EXPERTS
    i32 = jnp.int32
    eid = gates[:, N_EXPERTS:N_EXPERTS + 2].astype(i32).reshape(n_slot)
    order = jnp.argsort(eid, stable=True).astype(i32)
    rank = jnp.argsort(order).astype(i32)
    onehot = eid[:, None] == jnp.arange(N_EXPERTS, dtype=i32)[None, :]
    counts = jnp.sum(onehot, axis=0, dtype=i32)
    padded = (counts + tm - 1) // tm * tm
    ends = jnp.cumsum(padded)
    group_end = jnp.cumsum(counts)
    shift = (ends - padded) - (group_end - counts)
    dest = (rank + jnp.sum(jnp.where(onehot, shift[None, :], 0), axis=1, dtype=i32)).reshape(n_tok, 2)
    tile_start = jnp.arange(n_tiles, dtype=i32) * tm
    tile_expert = jnp.minimum(jnp.sum(tile_start[:, None] >= ends[None, :], axis=1, dtype=i32), N_EXPERTS - 1)
    row_expert = jnp.repeat(tile_expert, tm)
    pos = jnp.arange(n_tiles * tm, dtype=i32) - shift[row_expert]
    rows = jnp.arange(n_tiles * tm, dtype=i32)
    row_src = jnp.where(pos < group_end[row_expert], order[jnp.clip(pos, 0, n_slot - 1)] // 2, rows % n_tok)
    tile_first = jnp.concatenate([jnp.ones((1,), i32), (tile_expert[1:] != tile_expert[:-1]).astype(i32)])
    return row_src, dest, tile_expert, tile_first, (ends[-1:] // tm).astype(i32)


def _gather_kernel(idx_ref, src_ref, o_ref, sem, *, rows):
    base = pl.program_id(0) * rows

    def row_copy(r, src_row):
        return pltpu.make_async_copy(src_ref.at[pl.ds(src_row, 1), :], o_ref.at[pl.ds(r, 1), :], sem)

    def issue(r, carry):
        row_copy(r, idx_ref[base + r]).start()
        return carry

    def drain(r, carry):
        row_copy(r, 0).wait()
        return carry

    lax.fori_loop(0, rows, issue, 0, unroll=8)
    lax.fori_loop(0, rows, drain, 0, unroll=8)


def _gather_rows(src, idx):
    p, w = idx.shape[0], src.shape[1]
    rows = GATHER_ROWS
    return pl.pallas_call(
        functools.partial(_gather_kernel, rows=rows),
        grid_spec=pltpu.PrefetchScalarGridSpec(
            num_scalar_prefetch=1, grid=(p // rows,),
            in_specs=[pl.BlockSpec(memory_space=pl.ANY)],
            out_specs=pl.BlockSpec((rows, w), lambda i, idx: (i, 0)),
            scratch_shapes=[pltpu.SemaphoreType.DMA(())]),
        out_shape=jax.ShapeDtypeStruct((p, w), src.dtype),
        compiler_params=_params("arbitrary"), name="gather_rows",
    )(idx, src)


def _moe_up_kernel(te_ref, tf_ref, nu_ref, a_ref, wg_ref, wu_ref, o_ref, wg_bf, wu_bf):
    i = pl.program_id(1)

    @pl.when(tf_ref[i] == 1)
    def _():
        wg_bf[...] = wg_ref[...].astype(BF16)
        wu_bf[...] = wu_ref[...].astype(BF16)

    @pl.when(i < nu_ref[0])
    def _():
        a = a_ref[...].astype(BF16)
        o_ref[...] = (_silu(_dot(a, wg_bf[...])) * _dot(a, wu_bf[...])).astype(o_ref.dtype)

    @pl.when(i >= nu_ref[0])
    def _():
        o_ref[...] = jnp.zeros_like(o_ref)


def _moe_up(xg, wgu, layer, tile_expert, tile_first, n_used, tn=512):
    p, kd = xg.shape
    f = wgu.shape[3] // 2
    nf = f // tn
    tm = MOE_TM
    return pl.pallas_call(
        _moe_up_kernel,
        grid_spec=pltpu.PrefetchScalarGridSpec(
            num_scalar_prefetch=3, grid=(nf, p // tm),
            in_specs=[pl.BlockSpec((tm, kd), lambda j, i, te, tf, nu: (i, 0)),
                      pl.BlockSpec((None, None, kd, tn), lambda j, i, te, tf, nu: (layer, te[i], 0, j)),
                      pl.BlockSpec((None, None, kd, tn), lambda j, i, te, tf, nu: (layer, te[i], 0, j + nf))],
            out_specs=pl.BlockSpec((tm, tn), lambda j, i, te, tf, nu: (i, j)),
            scratch_shapes=[pltpu.VMEM((kd, tn), BF16)] * 2),
        out_shape=jax.ShapeDtypeStruct((p, f), BF16),
        compiler_params=_params("arbitrary", "arbitrary"), name="moe_up",
    )(tile_expert, tile_first, n_used, xg, wgu, wgu)


def _moe_down_kernel(te_ref, tf_ref, nu_ref, a_ref, w_ref, o_ref, w_bf):
    i = pl.program_id(1)

    @pl.when(tf_ref[i] == 1)
    def _():
        w_bf[...] = w_ref[...].astype(BF16)

    @pl.when(i < nu_ref[0])
    def _():
        o_ref[...] = _dot(a_ref[...], w_bf[...])

    @pl.when(i >= nu_ref[0])
    def _():
        o_ref[...] = jnp.zeros_like(o_ref)


def _moe_down(act, w2, layer, tile_expert, tile_first, n_used, tn=512):
    p, f = act.shape
    d = w2.shape[3]
    tm = MOE_TM
    return pl.pallas_call(
        _moe_down_kernel,
        grid_spec=pltpu.PrefetchScalarGridSpec(
            num_scalar_prefetch=3, grid=(d // tn, p // tm),
            in_specs=[pl.BlockSpec((tm, f), lambda j, i, te, tf, nu: (i, 0)),
                      pl.BlockSpec((None, None, f, tn), lambda j, i, te, tf, nu: (layer, te[i], 0, j))],
            out_specs=pl.BlockSpec((tm, tn), lambda j, i, te, tf, nu: (i, j)),
            scratch_shapes=[pltpu.VMEM((f, tn), BF16)]),
        out_shape=jax.ShapeDtypeStruct((p, d), F32),
        compiler_params=_params("arbitrary", "arbitrary"), name="moe_down",
    )(tile_expert, tile_first, n_used, act, w2)


def _moe_combine_kernel(y0_ref, y1_ref, gate_ref, res_ref, mod_ref, o_ref, *, mod_idx):
    gates = gate_ref[...]
    lane = lax.broadcasted_iota(jnp.int32, gates.shape, 1)
    i0 = gates[:, N_EXPERTS:N_EXPERTS + 1].astype(jnp.int32)
    i1 = gates[:, N_EXPERTS + 1:N_EXPERTS + 2].astype(jnp.int32)
    w0 = jnp.sum(jnp.where(lane == i0, gates, 0.0), axis=-1, keepdims=True)
    w1 = jnp.sum(jnp.where(lane == i1, gates, 0.0), axis=-1, keepdims=True)
    o_ref[...] = res_ref[...] + mod_ref[mod_idx:mod_idx + 1, :] * (w0 * y0_ref[...] + w1 * y1_ref[...])


def _moe_combine(y0, y1, gates, res, mod, mod_idx, *, n_ctx):
    b, t, d = res.shape
    tr = n_ctx
    spec = pl.BlockSpec((None, tr, d), lambda i, j: (i, j, 0))
    return pl.pallas_call(
        functools.partial(_moe_combine_kernel, mod_idx=mod_idx),
        grid=(b, t // tr),
        in_specs=[spec, spec, pl.BlockSpec((None, tr, LANES), lambda i, j: (i, j, 0)), spec,
                  pl.BlockSpec((None, None, N_MOD, d), lambda i, j: (i, jnp.minimum(j, 1), 0, 0))],
        out_specs=spec,
        out_shape=jax.ShapeDtypeStruct((b, t, d), F32),
        compiler_params=_params("parallel", "parallel"), name="moe_combine",
    )(y0, y1, gates, res, mod)


def _moe_block(xn, gates, res, wgu, w2, layer, mod, mod_idx, *, n_ctx):
    b, t, d = xn.shape
    m = b * t
    row_src, dest, tile_expert, tile_first, n_used = _route(gates.reshape(m, LANES), m)
    xg = _gather_rows(xn.reshape(m, d), row_src)
    act = _moe_up(xg, wgu, layer, tile_expert, tile_first, n_used)
    yg = _moe_down(act, w2, layer, tile_expert, tile_first, n_used)
    y0 = _gather_rows(yg, dest[:, 0]).reshape(b, t, d)
    y1 = _gather_rows(yg, dest[:, 1]).reshape(b, t, d)
    return _moe_combine(y0, y1, gates, res, mod, mod_idx, n_ctx=n_ctx)


def _merge_kernel(b0, b1, b2, b3, w_ref, g0, g1, g2, g3, o_ref):
    acc = None
    for n, (br, g) in enumerate(((b0, g0), (b1, g1), (b2, g2), (b3, g3))):
        term = _sigmoid(g[...].astype(F32)) * _dot(br[...], w_ref[n])
        acc = term if acc is None else acc + term
    o_ref[...] = acc.astype(o_ref.dtype)


def _merge(branches, gates, wb, layer, tm=None, tn=None):
    m, kd = branches[0].shape
    d = wb.shape[-1]
    tm = tm or _pick(m, (1152, 1024, 768, 512, 256))
    tn = tn or _pick(d, (512, 256, 128))
    nd = d // tn
    br_spec = pl.BlockSpec((tm, kd), lambda j, i: (i, 0))
    gate_specs = [pl.BlockSpec((tm, tn), functools.partial(lambda j, i, n: (i, j + n * nd), n=n))
                  for n in range(N_BRANCH)]
    return pl.pallas_call(
        _merge_kernel,
        grid=(nd, m // tm),
        in_specs=[br_spec] * N_BRANCH + [pl.BlockSpec((None, N_BRANCH, kd, tn), lambda j, i: (layer, 0, 0, j))]
        + gate_specs,
        out_specs=pl.BlockSpec((tm, tn), lambda j, i: (i, j)),
        out_shape=jax.ShapeDtypeStruct((m, d), BF16),
        compiler_params=_params("parallel", "parallel"), name="merge",
    )(*branches, wb, gates, gates, gates, gates)


def _conv_kernel(u_ref, w_ref, b_ref, o_ref, *, t_total, n_ctx, act):
    x = u_ref[...]
    pos = lax.broadcasted_iota(jnp.int32, (t_total, 1), 0)
    seg_lo = jnp.where(pos < n_ctx, 0, n_ctx)
    seg_hi = jnp.where(pos < n_ctx, n_ctx, t_total)
    left = (CONV_W - 1) // 2
    acc = None
    for j in range(CONV_W):
        d = j - left
        xs = x if d == 0 else pltpu.roll(x, (-d) % t_total, 0)
        ok = (pos + d >= seg_lo) & (pos + d < seg_hi)
        term = jnp.where(ok, xs, 0.0) * w_ref[j:j + 1, :]
        acc = term if acc is None else acc + term
    acc = acc + b_ref[...]
    o_ref[...] = _silu(acc) if act else acc


def _dwconv(u, col0, width, w, b, *, n_ctx, act, wt=256):
    bsz, t, _ = u.shape
    c0 = col0 // wt
    return pl.pallas_call(
        functools.partial(_conv_kernel, t_total=t, n_ctx=n_ctx, act=act),
        grid=(bsz, width // wt),
        in_specs=[pl.BlockSpec((None, t, wt), lambda i, j: (i, 0, j + c0)),
                  pl.BlockSpec((CONV_W, wt), lambda i, j: (0, j)),
                  pl.BlockSpec((1, wt), lambda i, j: (0, j))],
        out_specs=pl.BlockSpec((None, t, wt), lambda i, j: (i, 0, j)),
        out_shape=jax.ShapeDtypeStruct((bsz, t, width), F32),
        compiler_params=_params("parallel", "parallel"), name="dwconv",
    )(u, w, b.reshape(1, width))


def _ret_kernel(lg_ref, q_ref, k_ref, v_ref, g_ref, cos_ref, sin_ref, o_ref, qs, ks, acc_f, acc_b,
                *, t_total, n_ctx):
    c = SCAN_CHUNK
    n_chunks, n_ctx_chunks = t_total // c, n_ctx // c
    lg = lg_ref[pl.program_id(1)]
    cos, sin = cos_ref[...], sin_ref[...]
    q, k = q_ref[...], k_ref[...]
    qs[...] = q * cos + pltpu.roll(q, RET_DK // 2, 1) * sin
    ks[...] = (k * cos + pltpu.roll(k, RET_DK // 2, 1) * sin) * (RET_DK ** -0.5)

    ii = lax.broadcasted_iota(jnp.int32, (c, c), 0)
    jj = lax.broadcasted_iota(jnp.int32, (c, c), 1)
    dist = (ii - jj).astype(F32)
    r = lax.broadcasted_iota(jnp.int32, (c, 1), 0).astype(F32)
    decay_all = jnp.exp(c * lg)

    def chunk(off, s_in, fwd):
        qc = qs[pl.ds(off, c), :]
        kc = ks[pl.ds(off, c), :]
        vc = v_ref[pl.ds(off, c), :].astype(BF16)
        if fwd:
            mask, steps, q_pow, k_pow = ii >= jj, dist, r + 1.0, c - 1.0 - r
        else:
            mask, steps, q_pow, k_pow = jj >= ii, -dist, c - r, r
        decay = jnp.where(mask, jnp.exp(jnp.where(mask, steps, 0.0) * lg), 0.0)
        scores = _dot_nt(qc.astype(BF16), kc.astype(BF16)) * decay
        o = _dot(scores.astype(BF16), vc) + jnp.exp(q_pow * lg) * _dot(qc.astype(BF16), s_in.astype(BF16))
        s_out = decay_all * s_in + _dot_tn((kc * jnp.exp(k_pow * lg)).astype(BF16), vc)
        return o, s_out

    def post(off):
        o = acc_f[pl.ds(off, c), :] + acc_b[pl.ds(off, c), :]
        y = o * lax.rsqrt(jnp.mean(o * o, axis=-1, keepdims=True) + EPS)
        o_ref[pl.ds(off, c), :] = (y * _silu(g_ref[pl.ds(off, c), :])).astype(o_ref.dtype)

    _scan_both_ways(n_chunks, n_ctx_chunks, c, lambda off, st, d: chunk(off, st, d == 0), acc_f, acc_b,
                    jnp.zeros((RET_DK, RET_DK), F32), post)


def _retention(u, col0, mix_w, cos2, sin2, *, n_ctx):
    bsz, t, _ = u.shape
    heads = mix_w // RET_DK
    c0 = col0 // RET_DK
    log_gamma = jnp.log1p(-jnp.exp2(-5.0 - jnp.arange(heads, dtype=F32)))

    def spec(k):
        return pl.BlockSpec((None, t, RET_DK), lambda i, h, lg, k=k: (i, 0, c0 + k * heads + h))

    table = pl.BlockSpec((t, RET_DK), lambda i, h, lg: (0, 0))
    return pl.pallas_call(
        functools.partial(_ret_kernel, t_total=t, n_ctx=n_ctx),
        grid_spec=pltpu.PrefetchScalarGridSpec(
            num_scalar_prefetch=1, grid=(bsz, heads),
            in_specs=[spec(0), spec(1), spec(2), spec(3), table, table],
            out_specs=pl.BlockSpec((None, t, RET_DK), lambda i, h, lg: (i, 0, h)),
            scratch_shapes=[pltpu.VMEM((t, RET_DK), F32)] * 4),
        out_shape=jax.ShapeDtypeStruct((bsz, t, mix_w), BF16),
        compiler_params=_params("parallel", "arbitrary"), name="retention",
    )(log_gamma, u, u, u, u, cos2, sin2)


def _ssd_kernel(xs_ref, bm_ref, cm_ref, z_ref, dt_ref, dtb_ref, alog_ref, dsk_ref, nw_ref, sel_ref, ord_ref, o_ref,
                acc_f, acc_b, dts, las, *, t_total, n_ctx):
    c = ord_ref.shape[1]
    rep = sel_ref.shape[1] // 2
    gw = rep * SSD_HD
    n_chunks, n_ctx_chunks = t_total // c, n_ctx // c
    dt_all = _softplus(dt_ref[...] + dtb_ref[...])
    dts[...] = dt_all
    las[...] = dt_all * (-jnp.exp(alog_ref[...]))

    ii = lax.broadcasted_iota(jnp.int32, (c, c), 0)
    jj = lax.broadcasted_iota(jnp.int32, (c, c), 1)
    lane_head = lax.broadcasted_iota(jnp.int32, (1, gw), 1) // SSD_HD

    def chunk(off, s_in, d):
        fwd = d == 0
        mask = (ii >= jj) if fwd else (jj >= ii)
        la_parts = _split_bf16(las[pl.ds(off, c), :], 3)
        cum = sum(_dot(ord_ref[d], p) for p in la_parts)
        cum_rows = sum(_dot_tn(p, ord_ref[1 - d]) for p in la_parts)
        sel = sel_ref[d]
        cum_e = sum(_dot(p, sel) for p in _split_bf16(cum, 3))
        dt_e = sum(_dot(p, sel) for p in _split_bf16(dts[pl.ds(off, c), :], 3))
        last_e = cum_e[c - 1:c, :] if fwd else cum_e[0:1, :]
        xs = xs_ref[pl.ds(off, c), :]
        vdt = xs * dt_e
        cm = cm_ref[pl.ds(off, c), :].astype(BF16)
        bm = bm_ref[pl.ds(off, c), :].astype(BF16)
        qk = _dot_nt(cm, bm)
        o = jnp.exp(cum_e) * _dot(cm, s_in.astype(BF16))
        for i in range(rep):
            col = d * rep + i
            diff = cum[:, col:col + 1] - cum_rows[col:col + 1, :]
            decay = jnp.where(mask, jnp.exp(jnp.where(mask, diff, 0.0)), 0.0)
            o = o + _dot((qk * decay).astype(BF16), jnp.where(lane_head == i, vdt, 0.0).astype(BF16))
        s_out = jnp.exp(last_e) * s_in + _dot_tn(bm, (vdt * jnp.exp(last_e - cum_e)).astype(BF16))
        return o, s_out

    def post(off):
        o = acc_f[pl.ds(off, c), :] + acc_b[pl.ds(off, c), :]
        y = (o + dsk_ref[...] * xs_ref[pl.ds(off, c), :]) * _silu(z_ref[pl.ds(off, c), :])
        y = y * lax.rsqrt(jnp.mean(y * y, axis=-1, keepdims=True) + EPS)
        o_ref[pl.ds(off, c), :] = (y * nw_ref[...]).astype(o_ref.dtype)

    _scan_both_ways(n_chunks, n_ctx_chunks, c, chunk, acc_f, acc_b, jnp.zeros((SSD_STATE, gw), F32), post)


def _ssd(u_z, z_col0, xact, u_dt, dt_bias, a_log, d_skip, norm_w, mix_w, *, n_ctx):
    bsz, t, _ = xact.shape
    heads = mix_w // SSD_HD
    rep = heads // SSD_GROUPS
    gw = rep * SSD_HD
    def per_group(v):
        lead = v.shape[:-1]
        v = v.reshape(*lead, 2, SSD_GROUPS, rep)
        return jnp.moveaxis(v, -2, 0).reshape(SSD_GROUPS, *lead, 2 * rep)
    dt_g = jnp.moveaxis(per_group(u_dt), 0, 1)
    dtb_g = per_group(dt_bias.reshape(1, 2 * heads))
    alog_g = per_group(a_log.reshape(1, 2 * heads))
    sel = np.zeros((2, 2 * rep, gw), np.float32)
    for d in range(2):
        for i in range(rep):
            sel[d, d * rep + i, i * SSD_HD:(i + 1) * SSD_HD] = 1.0
    dsk = jnp.repeat(d_skip, SSD_HD).reshape(1, mix_w)
    zc, bc, cc = z_col0 // gw, mix_w // SSD_STATE, mix_w // SSD_STATE + SSD_GROUPS
    small = lambda: pl.BlockSpec((None, 1, 2 * rep), lambda i, g: (g, 0, 0))
    return pl.pallas_call(
        functools.partial(_ssd_kernel, t_total=t, n_ctx=n_ctx),
        grid=(bsz, SSD_GROUPS),
        in_specs=[pl.BlockSpec((None, t, gw), lambda i, g: (i, 0, g)),
                  pl.BlockSpec((None, t, SSD_STATE), lambda i, g: (i, 0, bc + g)),
                  pl.BlockSpec((None, t, SSD_STATE), lambda i, g: (i, 0, cc + g)),
                  pl.BlockSpec((None, t, gw), lambda i, g: (i, 0, zc + g)),
                  pl.BlockSpec((None, None, t, 2 * rep), lambda i, g: (i, g, 0, 0)),
                  small(), small(),
                  pl.BlockSpec((1, gw), lambda i, g: (0, g)),
                  pl.BlockSpec((1, gw), lambda i, g: (0, g)),
                  pl.BlockSpec((2, 2 * rep, gw), lambda i, g: (0, 0, 0)),
                  pl.BlockSpec((2, SSD_CHUNK, SSD_CHUNK), lambda i, g: (0, 0, 0))],
        out_specs=pl.BlockSpec((None, t, gw), lambda i, g: (i, 0, g)),
        out_shape=jax.ShapeDtypeStruct((bsz, t, mix_w), BF16),
        scratch_shapes=[pltpu.VMEM((t, gw), F32), pltpu.VMEM((t, gw), F32),
                        pltpu.VMEM((t, 2 * rep), F32), pltpu.VMEM((t, 2 * rep), F32)],
        compiler_params=_params("parallel", "arbitrary"), name="ssd",
    )(xact, xact, xact, u_z, dt_g, dtb_g, alog_g, dsk, norm_w.reshape(1, mix_w), jnp.asarray(sel, BF16),
      _scan_order_tables(SSD_CHUNK))


def _tile_scan(a, b, row, reverse):
    for s in (1, 2, 4):
        if reverse:
            ok, shift = row < SUBLANES - s, SUBLANES - s
        else:
            ok, shift = row >= s, s
        b = b + a * jnp.where(ok, pltpu.roll(b, shift, 0), 0.0)
        a = a * jnp.where(ok, pltpu.roll(a, shift, 0), 1.0)
    return a, b


def _lru_kernel(xc_ref, y_ref, w_ref, bias_ref, lam_ref, o_ref, acc, a_f, b_f, a_b, b_b, *, t_total, n_ctx):
    c = SCAN_CHUNK
    nb, bw = w_ref.shape[0], w_ref.shape[1]
    n_chunks, n_ctx_chunks = t_total // c, n_ctx // c
    n_tiles = c // SUBLANES
    log_sig = -_softplus(-lam_ref[...])
    acc[...] = jnp.zeros_like(acc)
    row = lax.broadcasted_iota(jnp.int32, (SUBLANES, bw), 0)

    def gates(off, d, dst_a, dst_b):
        for blk in range(nb):
            cols = slice(blk * bw, (blk + 1) * bw)
            x = xc_ref[pl.ds(off, c), cols]
            g = _dot(x.astype(BF16), w_ref[blk, :, d * 2 * bw:(d + 1) * 2 * bw]) \
                + bias_ref[blk, :, d * 2 * bw:(d + 1) * 2 * bw]
            log_a = LRU_C * _sigmoid(g[:, :bw]) * log_sig[d:d + 1, cols]
            a = jnp.exp(log_a)
            dst_a[:, cols] = a
            dst_b[:, cols] = jnp.sqrt(-jnp.tanh(log_a) * (a * a + 1.0)) * (_sigmoid(g[:, bw:]) * x)

    def step(s, carry):
        off_f = pl.multiple_of(s * c, c)
        off_b = pl.multiple_of(_bwd_chunk(s, n_chunks, n_ctx_chunks) * c, c)
        gates(off_f, 0, a_f, b_f)
        gates(off_b, 1, a_b, b_b)

        def tile(t, carry):
            h_f, h_b = carry
            r_f = pl.multiple_of(t * SUBLANES, SUBLANES)
            r_b = pl.multiple_of((n_tiles - 1 - t) * SUBLANES, SUBLANES)
            new_f, new_b = [], []
            for blk in range(nb):
                cols = slice(blk * bw, (blk + 1) * bw)
                a, b = _tile_scan(a_f[pl.ds(r_f, SUBLANES), cols], b_f[pl.ds(r_f, SUBLANES), cols], row, False)
                h = b + a * h_f[blk]
                acc[pl.ds(off_f + r_f, SUBLANES), cols] += h
                new_f.append(h[SUBLANES - 1:SUBLANES, :])
                a, b = _tile_scan(a_b[pl.ds(r_b, SUBLANES), cols], b_b[pl.ds(r_b, SUBLANES), cols], row, True)
                h = b + a * h_b[blk]
                acc[pl.ds(off_b + r_b, SUBLANES), cols] += h
                new_b.append(h[0:1, :])
            return tuple(new_f), tuple(new_b)

        return lax.fori_loop(0, n_tiles, tile, carry)

    zero = tuple(jnp.zeros((1, bw), F32) for _ in range(nb))
    lax.fori_loop(0, n_chunks, step, (zero, zero))

    def finish(s, _):
        off = pl.multiple_of(s * c, c)
        o_ref[pl.ds(off, c), :] = (acc[pl.ds(off, c), :] * _gelu_tanh(y_ref[pl.ds(off, c), :])).astype(o_ref.dtype)
        return 0

    lax.fori_loop(0, n_chunks, finish, 0)


def _rglru(xc, u_y, y_col0, wa, ba, wx, bx, lam, mix_w, *, n_ctx, blocks_per_step=4):
    bsz, t, _ = xc.shape
    bw = mix_w // LRU_BLOCKS
    gw = blocks_per_step * bw
    w = jnp.concatenate([wa[0], wx[0], wa[1], wx[1]], axis=-1).astype(BF16)
    blk = lambda v: v.reshape(2, LRU_BLOCKS, 1, bw)
    bias = jnp.concatenate([blk(ba)[0], blk(bx)[0], blk(ba)[1], blk(bx)[1]], axis=-1)
    yc = y_col0 // gw
    return pl.pallas_call(
        functools.partial(_lru_kernel, t_total=t, n_ctx=n_ctx),
        grid=(bsz, LRU_BLOCKS // blocks_per_step),
        in_specs=[pl.BlockSpec((None, t, gw), lambda i, j: (i, 0, j)),
                  pl.BlockSpec((None, t, gw), lambda i, j: (i, 0, yc + j)),
                  pl.BlockSpec((blocks_per_step, bw, 4 * bw), lambda i, j: (j, 0, 0)),
                  pl.BlockSpec((blocks_per_step, 1, 4 * bw), lambda i, j: (j, 0, 0)),
                  pl.BlockSpec((2, gw), lambda i, j: (0, j))],
        out_specs=pl.BlockSpec((None, t, gw), lambda i, j: (i, 0, j)),
        out_shape=jax.ShapeDtypeStruct((bsz, t, mix_w), BF16),
        scratch_shapes=[pltpu.VMEM((t, gw), F32)] + [pltpu.VMEM((SCAN_CHUNK, gw), F32)] * 4,
        compiler_params=_params("parallel", "arbitrary"), name="rglru",
    )(xc, u_y, w, bias, lam)


def _hg_tables():
    c = HG_CHUNK
    idx = np.arange(c)
    dif = np.zeros((2, HG_LEVELS, c, c), np.float32)
    isq = np.zeros((2, HG_LEVELS, c, HG_DK), np.float32)
    msk = np.zeros((2, HG_LEVELS, c, c), np.float32)
    for lvl in range(HG_LEVELS):
        half = 1 << lvl
        mid = (idx // (2 * half)) * 2 * half + half
        late = idx >= mid
        same = (idx[:, None] // (2 * half)) == (idx[None, :] // (2 * half))
        for d, boundary, is_query in ((0, mid - 1, late), (1, mid, ~late)):
            sign = np.where(is_query, 1.0, -1.0)
            dif[d, lvl, idx, idx] += sign
            dif[d, lvl, idx, boundary] -= sign
            isq[d, lvl] = is_query[:, None]
        msk[0, lvl] = same & late[:, None] & ~late[None, :]
        msk[1, lvl] = same & ~late[:, None] & late[None, :]
    lower = (idx[None, :] <= idx[:, None]).astype(np.float32)
    order = np.stack([lower, lower.T])
    rep3 = lambda v: jnp.asarray(np.concatenate([v, v, v], axis=-1), BF16)
    return rep3(order), rep3(dif.reshape(2, HG_LEVELS * c, c)), jnp.asarray(isq), jnp.asarray(msk)


def _hg_kernel(q_ref, f0_ref, f1_ref, v_ref, g_ref, lb_ref, nw_ref, ord_ref, dif_ref, isq_ref, msk_ref,
               o_ref, acc_f, acc_b, *, t_total, n_ctx):
    c = HG_CHUNK
    n_chunks, n_ctx_chunks = t_total // c, n_ctx // c
    eye = lax.broadcasted_iota(jnp.int32, (c, c), 0) == lax.broadcasted_iota(jnp.int32, (c, c), 1)

    def chunk(off, st, d):
        fwd = d == 0
        lb = lb_ref[d:d + 1, :]
        f = (f0_ref if fwd else f1_ref)[pl.ds(off, c), :]
        log_f = jnp.log(lb + (1.0 - lb) * _sigmoid(f))
        k = (1.0 - lb) * _sigmoid(-f)
        q = _silu(q_ref[pl.ds(off, c), :])
        v = v_ref[pl.ds(off, c), :].astype(BF16)
        cum = _dot(ord_ref[d], jnp.concatenate(_split_bf16(log_f, 3), axis=0))
        expo = _dot(dif_ref[d], jnp.concatenate(_split_bf16(cum, 3), axis=0))
        q_minus_k = q - k
        scores = jnp.where(eye, _dot_nt(q.astype(BF16), k.astype(BF16)), 0.0)
        for lvl in range(HG_LEVELS):
            rows = slice(lvl * c, (lvl + 1) * c)
            y = ((k + isq_ref[d, lvl] * q_minus_k) * jnp.exp(expo[rows, :])).astype(BF16)
            scores = scores + _dot_nt(y, y) * msk_ref[d, lvl]
        last = cum[c - 1:c, :] if fwd else cum[0:1, :]
        o = _dot(scores.astype(BF16), v) + _dot_nt((q * jnp.exp(cum)).astype(BF16), st.astype(BF16))
        st = st * jnp.exp(last) + _dot_tn(v, (k * jnp.exp(last - cum)).astype(BF16))
        return o, st

    def post(off):
        o = acc_f[pl.ds(off, c), :] + acc_b[pl.ds(off, c), :]
        y = o * lax.rsqrt(jnp.mean(o * o, axis=-1, keepdims=True) + EPS) * nw_ref[...]
        o_ref[pl.ds(off, c), :] = (y * _silu(g_ref[pl.ds(off, c), :])).astype(o_ref.dtype)

    _scan_both_ways(n_chunks, n_ctx_chunks, c, chunk, acc_f, acc_b, jnp.zeros((HG_DK, HG_DK), F32), post)


def _hgrn2(u, col0, mix_w, lower, norm_w, *, n_ctx):
    bsz, t, _ = u.shape
    heads = mix_w // HG_DK
    c0 = col0 // HG_DK
    tables = _hg_tables()

    def spec(k):
        return pl.BlockSpec((None, t, HG_DK), lambda i, h, k=k: (i, 0, c0 + k * heads + h))

    def table(v):
        return pl.BlockSpec(v.shape, lambda i, h, nd=v.ndim: (0,) * nd)

    return pl.pallas_call(
        functools.partial(_hg_kernel, t_total=t, n_ctx=n_ctx),
        grid=(bsz, heads),
        in_specs=[spec(0), spec(1), spec(2), spec(3), spec(4),
                  pl.BlockSpec((2, HG_DK), lambda i, h: (0, h)),
                  pl.BlockSpec((1, HG_DK), lambda i, h: (0, h))] + [table(v) for v in tables],
        out_specs=pl.BlockSpec((None, t, HG_DK), lambda i, h: (i, 0, h)),
        out_shape=jax.ShapeDtypeStruct((bsz, t, mix_w), BF16),
        scratch_shapes=[pltpu.VMEM((t, HG_DK), F32)] * 2,
        compiler_params=_params("parallel", "arbitrary"), name="hgrn2",
    )(u, u, u, u, u, lower, norm_w.reshape(1, mix_w), *tables)


def _ada_kernel(c_ref, w_ref, b_ref, o_ref):
    o_ref[...] = _dot(_silu(c_ref[...]).astype(BF16), w_ref[...].astype(BF16)) + b_ref[...]


def _ada(cond, ada_w, ada_b, tn=1024):
    r, d = cond.shape
    depth, _, n = ada_w.shape
    return pl.pallas_call(
        _ada_kernel,
        grid=(depth, n // tn),
        in_specs=[pl.BlockSpec((r, d), lambda l, j: (0, 0)),
                  pl.BlockSpec((None, d, tn), lambda l, j: (l, 0, j)),
                  pl.BlockSpec((None, 1, tn), lambda l, j: (l, 0, j))],
        out_specs=pl.BlockSpec((None, r, tn), lambda l, j: (l, 0, j)),
        out_shape=jax.ShapeDtypeStruct((depth, r, n), F32),
        compiler_params=_params("parallel", "parallel"), name="ada",
    )(cond, ada_w, ada_b.reshape(depth, 1, n))


def _rope_tables(n_ctx, n_lat):
    r, col = jnp.meshgrid(jnp.arange(n_lat // GRID_W), jnp.arange(GRID_W), indexing='ij')
    n_freq = RET_DK // 4
    freqs = ROPE_BASE ** (-jnp.arange(n_freq, dtype=F32) / n_freq)
    ang = jnp.concatenate([r.reshape(-1, 1) * freqs, col.reshape(-1, 1) * freqs], axis=-1)
    cos, sin = jnp.cos(ang), jnp.sin(ang)
    cos2 = jnp.concatenate([jnp.ones((n_ctx, RET_DK), F32), jnp.concatenate([cos, cos], axis=-1)], axis=0)
    sin2 = jnp.concatenate([jnp.zeros((n_ctx, RET_DK), F32), jnp.concatenate([-sin, sin], axis=-1)], axis=0)
    return cos2, sin2


def kernel(x, c, ctx, c_ctx, ada_w, ada_b, norm1_w, norm2_w, w_in, ssd_conv_w, ssd_conv_b, ssd_a_log, ssd_dt_bias, ssd_d, ssd_norm_w, lru_conv_w, lru_conv_b, lru_wa, lru_ba, lru_wx, lru_bx, lru_lambda, hg_lb_logits, hg_norm_w, w_branch, w_out, ffn_wgu, ffn_w2, router_w, moe_wgu, moe_w2, final_norm_w):
    bsz, n_lat, d = x.shape
    n_ctx = ctx.shape[1]
    t = n_ctx + n_lat
    depth = w_in.shape[0]
    mix_w = d // 2
    ssd_heads = mix_w // SSD_HD
    xbc_w = mix_w + 2 * SSD_GROUPS * SSD_STATE
    ret_w, ssd_w, dt_w, lru_w, hg_w, gate_w = 4 * mix_w, mix_w + xbc_w, 2 * ssd_heads, 2 * mix_w, 5 * mix_w, N_BRANCH * d
    dt_col = ret_w + ssd_w
    rest_col = dt_col + dt_w
    w_branch_bf, w_out_bf, ffn_w2_bf = w_branch.astype(BF16), w_out.astype(BF16), ffn_w2.astype(BF16)

    cos2, sin2 = _rope_tables(n_ctx, n_lat)
    p_lb = jax.nn.softmax(hg_lb_logits.astype(F32), axis=1)
    lower_bounds = jnp.cumsum(p_lb, axis=1) - p_lb[:, :1]

    pad = (-(1 + bsz)) % SUBLANES
    cond = jnp.concatenate([c_ctx[None], c, jnp.zeros((pad, d), F32)], axis=0)
    mods = _ada(cond, ada_w, ada_b).reshape(depth, -1, N_MOD, d)

    h = jnp.concatenate([ctx.astype(x.dtype), x], axis=1)
    for layer in range(depth):
        m_ctx = jnp.broadcast_to(mods[layer, 0], (bsz, N_MOD, d))
        m_lat = mods[layer, 1:1 + bsz]
        mod_rows = jnp.stack([m_ctx, m_lat], axis=1)
        mod_cols = jnp.stack([m_ctx, m_lat], axis=2)

        xn = _norm_call(h, norm1_w[layer], mod_rows, shift=0, scale=1, n_ctx=n_ctx).reshape(bsz * t, d)
        u_ret = _matmul(xn, w_in, layer, 0, ret_w).reshape(bsz, t, ret_w)
        u_ssd = _matmul(xn, w_in, layer, ret_w, ssd_w).reshape(bsz, t, ssd_w)
        u_dt = _matmul(xn, w_in, layer, dt_col, LANES)[:, :dt_w].reshape(bsz, t, dt_w)
        u_lru = _matmul(xn, w_in, layer, rest_col, lru_w).reshape(bsz, t, lru_w)
        u_hg = _matmul(xn, w_in, layer, rest_col + lru_w, hg_w).reshape(bsz, t, hg_w)
        u_gates = _matmul(xn, w_in, layer, rest_col + lru_w + hg_w, gate_w, out_dtype=BF16)

        b_ret = _retention(u_ret, 0, mix_w, cos2, sin2, n_ctx=n_ctx)
        xact = _dwconv(u_ssd, mix_w, xbc_w, ssd_conv_w[layer], ssd_conv_b[layer], n_ctx=n_ctx, act=True)
        b_ssd = _ssd(u_ssd, 0, xact, u_dt, ssd_dt_bias[layer], ssd_a_log[layer], ssd_d[layer],
                     ssd_norm_w[layer], mix_w, n_ctx=n_ctx)
        xc = _dwconv(u_lru, 0, mix_w, lru_conv_w[layer], lru_conv_b[layer], n_ctx=n_ctx, act=False)
        b_lru = _rglru(xc, u_lru, mix_w, lru_wa[layer], lru_ba[layer], lru_wx[layer], lru_bx[layer],
                       lru_lambda[layer], mix_w, n_ctx=n_ctx)
        b_hg = _hgrn2(u_hg, 0, mix_w, lower_bounds[:, layer], hg_norm_w[layer], n_ctx=n_ctx)

        branches = [v.reshape(bsz * t, mix_w) for v in (b_ret, b_ssd, b_lru, b_hg)]
        mixed = _merge(branches, u_gates, w_branch_bf, layer, tm=768)
        h2 = _matmul_residual(mixed, w_out_bf, layer, h.reshape(bsz * t, d), mod_cols, 2, t_total=t, n_ctx=n_ctx)

        if layer % 2 == 0:
            xn = _norm_call(h2.reshape(bsz, t, d), norm2_w[layer], mod_rows, shift=3, scale=4, n_ctx=n_ctx)
            act = _swiglu_up(xn.reshape(bsz * t, d), ffn_wgu, layer // 2)
            h = _matmul_residual(act, ffn_w2_bf, layer // 2, h2, mod_cols, 5, t_total=t, n_ctx=n_ctx,
                                 tm=768, tn=512, tk=act.shape[1]).reshape(bsz, t, d)
        else:
            xn, gates = _norm_call(h2.reshape(bsz, t, d), norm2_w[layer], mod_rows, shift=3, scale=4, n_ctx=n_ctx,
                                   router_w=router_w[layer // 2], out_dtype=F32)
            h = _moe_block(xn, gates, h2.reshape(bsz, t, d), moe_wgu, moe_w2, layer // 2, mod_rows, 5, n_ctx=n_ctx)
    return _final_norm_call(h, final_norm_w, n_ctx)
```

```python
import functools
import math

import numpy as np
import jax
import jax.numpy as jnp
from jax import lax
from jax.experimental import pallas as pl
from jax.experimental.pallas import tpu as pltpu

F32 = jnp.float32
BF16 = jnp.bfloat16
HI = lax.Precision.HIGHEST

EPS = 1e-6
GRID_W = 64
ROPE_BASE = 10000.0
RET_DK = 128
SSD_HD = 64
SSD_GROUPS = 4
SSD_STATE = 128
CONV_W = 4
LRU_BLOCKS = 8
LRU_C = 8.0
HG_DK = 128
N_BRANCH = 4
N_EXPERTS = 8
N_MOD = 6

LANES = 128
SUBLANES = 8
VMEM_LIMIT = 52 * 1024 * 1024

SCAN_CHUNK = 256
SSD_CHUNK = 256
HG_CHUNK = 128
HG_LEVELS = 7


def _params(*sem):
    return pltpu.CompilerParams(dimension_semantics=sem, vmem_limit_bytes=VMEM_LIMIT)


def _dot(a, b, prec=None):
    return jnp.dot(a, b, preferred_element_type=F32, precision=prec)


def _dot_nt(a, b, prec=None):
    return lax.dot_general(a, b, (((1,), (1,)), ((), ())), preferred_element_type=F32, precision=prec)


def _dot_tn(a, b, prec=None):
    return lax.dot_general(a, b, (((0,), (0,)), ((), ())), preferred_element_type=F32, precision=prec)


def _sigmoid(x):
    return 1.0 / (1.0 + jnp.exp(-x))


def _silu(x):
    return x * _sigmoid(x)


def _softplus(x):
    return jnp.maximum(x, 0.0) + jnp.log1p(jnp.exp(-jnp.abs(x)))


def _gelu_tanh(x):
    return 0.5 * x * (1.0 + jnp.tanh(math.sqrt(2.0 / math.pi) * (x + 0.044715 * (x * x * x))))


def _bwd_chunk(s, n_chunks, n_ctx_chunks):
    return jnp.where(s < n_ctx_chunks, n_ctx_chunks - 1 - s, n_chunks - 1 - (s - n_ctx_chunks))


def _scan_both_ways(n_chunks, n_ctx_chunks, c, chunk_fn, acc_f, acc_b, zero_state, post_fn):
    def body(s, carry):
        st_f, st_b = carry
        off_f = pl.multiple_of(s * c, c)
        off_b = pl.multiple_of(_bwd_chunk(s, n_chunks, n_ctx_chunks) * c, c)
        o_f, st_f = chunk_fn(off_f, st_f, 0)
        o_b, st_b = chunk_fn(off_b, st_b, 1)
        acc_f[pl.ds(off_f, c), :] = o_f
        acc_b[pl.ds(off_b, c), :] = o_b
        return st_f, st_b

    lax.fori_loop(0, n_chunks, body, (zero_state, zero_state), unroll=3)

    def finish(s, _):
        post_fn(pl.multiple_of(s * c, c))
        return 0

    lax.fori_loop(0, n_chunks, finish, 0)


def _scan_order_tables(c):
    idx = np.arange(c)
    lower = (idx[None, :] <= idx[:, None]).astype(np.float32)
    return jnp.asarray(np.stack([lower, lower.T]), BF16)


def _split_bf16(x, parts):
    out = []
    for _ in range(parts):
        p = x.astype(BF16)
        out.append(p)
        x = x - p.astype(F32)
    return out


def _norm_kernel(h_ref, w_ref, mod_ref, o_ref, *, shift, scale):
    x = h_ref[...]
    y = x * lax.rsqrt(jnp.mean(x * x, axis=-1, keepdims=True) + EPS) * w_ref[...]
    o_ref[...] = (y * (1.0 + mod_ref[scale:scale + 1, :]) + mod_ref[shift:shift + 1, :]).astype(o_ref.dtype)


def _norm_router_kernel(h_ref, w_ref, mod_ref, rw_ref, o_ref, gate_ref, *, shift, scale):
    x = h_ref[...]
    y = x * lax.rsqrt(jnp.mean(x * x, axis=-1, keepdims=True) + EPS) * w_ref[...]
    xn = y * (1.0 + mod_ref[scale:scale + 1, :]) + mod_ref[shift:shift + 1, :]
    o_ref[...] = xn.astype(o_ref.dtype)
    logits = _dot(xn.astype(BF16), rw_ref[...])
    lane = lax.broadcasted_iota(jnp.int32, logits.shape, 1)
    neg = jnp.float32(-jnp.inf)
    logits = jnp.where(lane < N_EXPERTS, logits, neg)
    v1 = jnp.max(logits, axis=-1, keepdims=True)
    i1 = jnp.min(jnp.where(logits == v1, lane, LANES), axis=-1, keepdims=True)
    rest = jnp.where(lane == i1, neg, logits)
    v2 = jnp.max(rest, axis=-1, keepdims=True)
    i2 = jnp.min(jnp.where(rest == v2, lane, LANES), axis=-1, keepdims=True)
    e2 = jnp.exp(v2 - v1)
    w1 = 1.0 / (1.0 + e2)
    w2 = e2 / (1.0 + e2)
    gate_ref[...] = (jnp.where(lane == i1, w1, 0.0) + jnp.where(lane == i2, w2, 0.0)
                     + jnp.where(lane == N_EXPERTS, i1.astype(F32), 0.0)
                     + jnp.where(lane == N_EXPERTS + 1, i2.astype(F32), 0.0))


def _norm_call(h, w, mod, *, shift, scale, n_ctx, router_w=None, out_dtype=BF16):
    b, t, d = h.shape
    tr = n_ctx
    grid = (b, t // tr)
    h_spec = pl.BlockSpec((None, tr, d), lambda i, j: (i, j, 0))
    w_spec = pl.BlockSpec((1, d), lambda i, j: (0, 0))
    mod_spec = pl.BlockSpec((None, None, N_MOD, d), lambda i, j: (i, jnp.minimum(j, 1), 0, 0))
    if router_w is None:
        return pl.pallas_call(
            functools.partial(_norm_kernel, shift=shift, scale=scale),
            grid=grid, in_specs=[h_spec, w_spec, mod_spec], out_specs=h_spec,
            out_shape=jax.ShapeDtypeStruct(h.shape, out_dtype),
            compiler_params=_params("parallel", "parallel"), name="norm_mod",
        )(h, w.reshape(1, d), mod)
    rw = jnp.zeros((d, LANES), BF16).at[:, :N_EXPERTS].set(router_w.astype(BF16))
    return pl.pallas_call(
        functools.partial(_norm_router_kernel, shift=shift, scale=scale),
        grid=grid,
        in_specs=[h_spec, w_spec, mod_spec, pl.BlockSpec((d, LANES), lambda i, j: (0, 0))],
        out_specs=[h_spec, pl.BlockSpec((None, tr, LANES), lambda i, j: (i, j, 0))],
        out_shape=[jax.ShapeDtypeStruct(h.shape, out_dtype), jax.ShapeDtypeStruct((b, t, LANES), F32)],
        compiler_params=_params("parallel", "parallel"), name="norm_mod_router",
    )(h, w.reshape(1, d), mod, rw)


def _final_norm_kernel(h_ref, w_ref, o_ref):
    x = h_ref[...]
    o_ref[...] = x * lax.rsqrt(jnp.mean(x * x, axis=-1, keepdims=True) + EPS) * w_ref[...]


def _final_norm_call(h, w, n_ctx):
    b, t, d = h.shape
    tr = n_ctx
    return pl.pallas_call(
        _final_norm_kernel,
        grid=(b, (t - n_ctx) // tr),
        in_specs=[pl.BlockSpec((None, tr, d), lambda i, j: (i, j + 1, 0)), pl.BlockSpec((1, d), lambda i, j: (0, 0))],
        out_specs=pl.BlockSpec((None, tr, d), lambda i, j: (i, j, 0)),
        out_shape=jax.ShapeDtypeStruct((b, t - n_ctx, d), F32),
        compiler_params=_params("parallel", "parallel"), name="final_norm",
    )(h, w.reshape(1, d))


def _pick(n, prefs):
    for p in prefs:
        if n % p == 0:
            return p
    return n


def _mm_kernel(a_ref, w_ref, o_ref, w_bf):
    @pl.when(pl.program_id(1) == 0)
    def _():
        w_bf[...] = w_ref[...].astype(BF16)

    o_ref[...] = _dot_nt(a_ref[...], w_bf[...]).astype(o_ref.dtype)


def _mm_shifted_kernel(a_ref, w_ref, tail_ref, o_ref, w_bf, *, shift):
    @pl.when(pl.program_id(1) == 0)
    def _():
        w_bf[...] = jnp.concatenate([w_ref[shift:, :], tail_ref[...]], axis=0).astype(BF16)

    o_ref[...] = _dot_nt(a_ref[...], w_bf[...]).astype(o_ref.dtype)


def _matmul(a, w_t, layer, col0, n, out_dtype=F32, tm=None, tn=None):
    m, kd = a.shape
    tm = tm or _pick(m, (1152, 1024, 768, 512, 256))
    shift = col0 % LANES
    base = col0 - shift
    tn = tn or _pick(math.gcd(n, base) if base else n, (1024, 512, 256, 128))
    c0 = base // tn
    if shift:
        return pl.pallas_call(
            functools.partial(_mm_shifted_kernel, shift=shift),
            grid=(n // tn, m // tm),
            in_specs=[pl.BlockSpec((tm, kd), lambda j, i: (i, 0)),
                      pl.BlockSpec((None, tn, kd), lambda j, i: (layer, c0 + j, 0)),
                      pl.BlockSpec((None, shift, kd), lambda j, i: (layer, (c0 + j + 1) * (tn // shift), 0))],
            out_specs=pl.BlockSpec((tm, tn), lambda j, i: (i, j)),
            out_shape=jax.ShapeDtypeStruct((m, n), out_dtype),
            scratch_shapes=[pltpu.VMEM((tn, kd), BF16)],
            compiler_params=_params("arbitrary", "arbitrary"), name="matmul_shifted",
        )(a, w_t, w_t)
    return pl.pallas_call(
        _mm_kernel,
        grid=(n // tn, m // tm),
        in_specs=[pl.BlockSpec((tm, kd), lambda j, i: (i, 0)),
                  pl.BlockSpec((None, tn, kd), lambda j, i: (layer, c0 + j, 0))],
        out_specs=pl.BlockSpec((tm, tn), lambda j, i: (i, j)),
        out_shape=jax.ShapeDtypeStruct((m, n), out_dtype),
        scratch_shapes=[pltpu.VMEM((tn, kd), BF16)],
        compiler_params=_params("arbitrary", "arbitrary"), name="matmul",
    )(a, w_t)


def _row_is_ctx(tm, t_total, n_ctx):
    row0 = pl.program_id(1) * tm
    pos = (row0 + lax.broadcasted_iota(jnp.int32, (tm, 1), 0)) % t_total
    return pos < n_ctx


def _mm_res_kernel(a_ref, w_ref, res_ref, mod_ref, o_ref, acc_ref, *, nk, tm, t_total, n_ctx):
    k = pl.program_id(2)
    part = _dot(a_ref[...], w_ref[...])

    def finish(total):
        mod = jnp.where(_row_is_ctx(tm, t_total, n_ctx), mod_ref[0:1, :], mod_ref[1:2, :])
        o_ref[...] = res_ref[...] + mod * total

    if nk == 1:
        finish(part)
        return

    @pl.when(k == 0)
    def _():
        acc_ref[...] = part

    @pl.when(k > 0)
    def _():
        acc_ref[...] += part

    @pl.when(k == nk - 1)
    def _():
        finish(acc_ref[...])


def _matmul_residual(a, w, layer, res, mod, mod_idx, *, t_total, n_ctx, tm=None, tn=None, tk=None):
    m, kd = a.shape
    n = w.shape[2]
    tm = tm or _pick(t_total, (1152, 768, 256))
    tn = tn or _pick(n, (1024, 512, 256, 128))
    tk = tk or (kd if kd <= 2048 else _pick(kd, (512, 256, 128)))
    nk = kd // tk
    per_b = t_total // tm
    return pl.pallas_call(
        functools.partial(_mm_res_kernel, nk=nk, tm=tm, t_total=t_total, n_ctx=n_ctx),
        grid=(n // tn, m // tm, nk),
        in_specs=[pl.BlockSpec((tm, tk), lambda j, i, k: (i, k)),
                  pl.BlockSpec((None, tk, tn), lambda j, i, k: (layer, k, j)),
                  pl.BlockSpec((tm, tn), lambda j, i, k: (i, j)),
                  pl.BlockSpec((None, None, 2, tn), lambda j, i, k: (i // per_b, mod_idx, 0, j))],
        out_specs=pl.BlockSpec((tm, tn), lambda j, i, k: (i, j)),
        out_shape=jax.ShapeDtypeStruct((m, n), F32),
        scratch_shapes=[pltpu.VMEM((tm, tn), F32)],
        compiler_params=_params("parallel", "parallel", "arbitrary"), name="matmul_residual",
    )(a, w, res, mod)


def _swiglu_kernel(a_ref, wg_ref, wu_ref, o_ref, wg_bf, wu_bf):
    @pl.when(pl.program_id(1) == 0)
    def _():
        wg_bf[...] = wg_ref[...].astype(BF16)
        wu_bf[...] = wu_ref[...].astype(BF16)

    a = a_ref[...]
    o_ref[...] = (_silu(_dot(a, wg_bf[...])) * _dot(a, wu_bf[...])).astype(o_ref.dtype)


def _swiglu_up(a, wgu, layer, tm=None, tn=None):
    m, kd = a.shape
    f = wgu.shape[2] // 2
    tm = tm or _pick(m, (1152, 1024, 768, 512, 256))
    tn = tn or _pick(f, (512, 256, 128))
    nf = f // tn
    return pl.pallas_call(
        _swiglu_kernel,
        grid=(nf, m // tm),
        in_specs=[pl.BlockSpec((tm, kd), lambda j, i: (i, 0)),
                  pl.BlockSpec((None, kd, tn), lambda j, i: (layer, 0, j)),
                  pl.BlockSpec((None, kd, tn), lambda j, i: (layer, 0, j + nf))],
        out_specs=pl.BlockSpec((tm, tn), lambda j, i: (i, j)),
        out_shape=jax.ShapeDtypeStruct((m, f), BF16),
        scratch_shapes=[pltpu.VMEM((kd, tn), BF16)] * 2,
        compiler_params=_params("arbitrary", "arbitrary"), name="swiglu_up",
    )(a, wgu, wgu)


MOE_TM = 512
GATHER_ROWS = 256


def _route(gates, n_tok):
    tm = MOE_TM
    n_slot = 2 * n_tok
    n_tiles = n_slot // tm + N_EXPERTS
    i32 = jnp.int32
    eid = gates[:, N_EXPERTS:N_EXPERTS + 2].astype(i32).reshape(n_slot)
    order = jnp.argsort(eid, stable=True).astype(i32)
    rank = jnp.argsort(order).astype(i32)
    onehot = eid[:, None] == jnp.arange(N_EXPERTS, dtype=i32)[None, :]
    counts = jnp.sum(onehot, axis=0, dtype=i32)
    padded = (counts + tm - 1) // tm * tm
    ends = jnp.cumsum(padded)
    group_end = jnp.cumsum(counts)
    shift = (ends - padded) - (group_end - counts)
    dest = (rank + jnp.sum(jnp.where(onehot, shift[None, :], 0), axis=1, dtype=i32)).reshape(n_tok, 2)
    tile_start = jnp.arange(n_tiles, dtype=i32) * tm
    tile_expert = jnp.minimum(jnp.sum(tile_start[:, None] >= ends[None, :], axis=1, dtype=i32), N_EXPERTS - 1)
    row_expert = jnp.repeat(tile_expert, tm)
    pos = jnp.arange(n_tiles * tm, dtype=i32) - shift[row_expert]
    rows = jnp.arange(n_tiles * tm, dtype=i32)
    row_src = jnp.where(pos < group_end[row_expert], order[jnp.clip(pos, 0, n_slot - 1)] // 2, rows % n_tok)
    tile_first = jnp.concatenate([jnp.ones((1,), i32), (tile_expert[1:] != tile_expert[:-1]).astype(i32)])
    return row_src, dest, tile_expert, tile_first, (ends[-1:] // tm).astype(i32)


def _gather_kernel(idx_ref, src_ref, o_ref, sem, *, rows):
    base = pl.program_id(0) * rows

    def row_copy(r, src_row):
        return pltpu.make_async_copy(src_ref.at[pl.ds(src_row, 1), :], o_ref.at[pl.ds(r, 1), :], sem)

    def issue(r, carry):
        row_copy(r, idx_ref[base + r]).start()
        return carry

    def drain(r, carry):
        row_copy(r, 0).wait()
        return carry

    lax.fori_loop(0, rows, issue, 0, unroll=8)
    lax.fori_loop(0, rows, drain, 0, unroll=8)


def _gather_rows(src, idx):
    p, w = idx.shape[0], src.shape[1]
    rows = GATHER_ROWS
    return pl.pallas_call(
        functools.partial(_gather_kernel, rows=rows),
        grid_spec=pltpu.PrefetchScalarGridSpec(
            num_scalar_prefetch=1, grid=(p // rows,),
            in_specs=[pl.BlockSpec(memory_space=pl.ANY)],
            out_specs=pl.BlockSpec((rows, w), lambda i, idx: (i, 0)),
            scratch_shapes=[pltpu.SemaphoreType.DMA(())]),
        out_shape=jax.ShapeDtypeStruct((p, w), src.dtype),
        compiler_params=_params("arbitrary"), name="gather_rows",
    )(idx, src)


def _moe_up_kernel(te_ref, tf_ref, nu_ref, a_ref, wg_ref, wu_ref, o_ref, wg_bf, wu_bf):
    i = pl.program_id(1)

    @pl.when(tf_ref[i] == 1)
    def _():
        wg_bf[...] = wg_ref[...].astype(BF16)
        wu_bf[...] = wu_ref[...].astype(BF16)

    @pl.when(i < nu_ref[0])
    def _():
        a = a_ref[...].astype(BF16)
        o_ref[...] = (_silu(_dot(a, wg_bf[...])) * _dot(a, wu_bf[...])).astype(o_ref.dtype)

    @pl.when(i >= nu_ref[0])
    def _():
        o_ref[...] = jnp.zeros_like(o_ref)


def _moe_up(xg, wgu, layer, tile_expert, tile_first, n_used, tn=512):
    p, kd = xg.shape
    f = wgu.shape[3] // 2
    nf = f // tn
    tm = MOE_TM
    return pl.pallas_call(
        _moe_up_kernel,
        grid_spec=pltpu.PrefetchScalarGridSpec(
            num_scalar_prefetch=3, grid=(nf, p // tm),
            in_specs=[pl.BlockSpec((tm, kd), lambda j, i, te, tf, nu: (i, 0)),
                      pl.BlockSpec((None, None, kd, tn), lambda j, i, te, tf, nu: (layer, te[i], 0, j)),
                      pl.BlockSpec((None, None, kd, tn), lambda j, i, te, tf, nu: (layer, te[i], 0, j + nf))],
            out_specs=pl.BlockSpec((tm, tn), lambda j, i, te, tf, nu: (i, j)),
            scratch_shapes=[pltpu.VMEM((kd, tn), BF16)] * 2),
        out_shape=jax.ShapeDtypeStruct((p, f), BF16),
        compiler_params=_params("arbitrary", "arbitrary"), name="moe_up",
    )(tile_expert, tile_first, n_used, xg, wgu, wgu)


def _moe_down_kernel(te_ref, tf_ref, nu_ref, a_ref, w_ref, o_ref, w_bf):
    i = pl.program_id(1)

    @pl.when(tf_ref[i] == 1)
    def _():
        w_bf[...] = w_ref[...].astype(BF16)

    @pl.when(i < nu_ref[0])
    def _():
        o_ref[...] = _dot(a_ref[...], w_bf[...])

    @pl.when(i >= nu_ref[0])
    def _():
        o_ref[...] = jnp.zeros_like(o_ref)


def _moe_down(act, w2, layer, tile_expert, tile_first, n_used, tn=512):
    p, f = act.shape
    d = w2.shape[3]
    tm = MOE_TM
    return pl.pallas_call(
        _moe_down_kernel,
        grid_spec=pltpu.PrefetchScalarGridSpec(
            num_scalar_prefetch=3, grid=(d // tn, p // tm),
            in_specs=[pl.BlockSpec((tm, f), lambda j, i, te, tf, nu: (i, 0)),
                      pl.BlockSpec((None, None, f, tn), lambda j, i, te, tf, nu: (layer, te[i], 0, j))],
            out_specs=pl.BlockSpec((tm, tn), lambda j, i, te, tf, nu: (i, j)),
            scratch_shapes=[pltpu.VMEM((f, tn), BF16)]),
        out_shape=jax.ShapeDtypeStruct((p, d), F32),
        compiler_params=_params("arbitrary", "arbitrary"), name="moe_down",
    )(tile_expert, tile_first, n_used, act, w2)


def _moe_combine_kernel(y0_ref, y1_ref, gate_ref, res_ref, mod_ref, o_ref, *, mod_idx):
    gates = gate_ref[...]
    lane = lax.broadcasted_iota(jnp.int32, gates.shape, 1)
    i0 = gates[:, N_EXPERTS:N_EXPERTS + 1].astype(jnp.int32)
    i1 = gates[:, N_EXPERTS + 1:N_EXPERTS + 2].astype(jnp.int32)
    w0 = jnp.sum(jnp.where(lane == i0, gates, 0.0), axis=-1, keepdims=True)
    w1 = jnp.sum(jnp.where(lane == i1, gates, 0.0), axis=-1, keepdims=True)
    o_ref[...] = res_ref[...] + mod_ref[mod_idx:mod_idx + 1, :] * (w0 * y0_ref[...] + w1 * y1_ref[...])


def _moe_combine(y0, y1, gates, res, mod, mod_idx, *, n_ctx):
    b, t, d = res.shape
    tr = n_ctx
    spec = pl.BlockSpec((None, tr, d), lambda i, j: (i, j, 0))
    return pl.pallas_call(
        functools.partial(_moe_combine_kernel, mod_idx=mod_idx),
        grid=(b, t // tr),
        in_specs=[spec, spec, pl.BlockSpec((None, tr, LANES), lambda i, j: (i, j, 0)), spec,
                  pl.BlockSpec((None, None, N_MOD, d), lambda i, j: (i, jnp.minimum(j, 1), 0, 0))],
        out_specs=spec,
        out_shape=jax.ShapeDtypeStruct((b, t, d), F32),
        compiler_params=_params("parallel", "parallel"), name="moe_combine",
    )(y0, y1, gates, res, mod)


def _moe_block(xn, gates, res, wgu, w2, layer, mod, mod_idx, *, n_ctx):
    b, t, d = xn.shape
    m = b * t
    row_src, dest, tile_expert, tile_first, n_used = _route(gates.reshape(m, LANES), m)
    xg = _gather_rows(xn.reshape(m, d), row_src)
    act = _moe_up(xg, wgu, layer, tile_expert, tile_first, n_used)
    yg = _moe_down(act, w2, layer, tile_expert, tile_first, n_used)
    y0 = _gather_rows(yg, dest[:, 0]).reshape(b, t, d)
    y1 = _gather_rows(yg, dest[:, 1]).reshape(b, t, d)
    return _moe_combine(y0, y1, gates, res, mod, mod_idx, n_ctx=n_ctx)


def _merge_kernel(b0, b1, b2, b3, w_ref, g0, g1, g2, g3, o_ref):
    acc = None
    for n, (br, g) in enumerate(((b0, g0), (b1, g1), (b2, g2), (b3, g3))):
        term = _sigmoid(g[...].astype(F32)) * _dot(br[...], w_ref[n])
        acc = term if acc is None else acc + term
    o_ref[...] = acc.astype(o_ref.dtype)


def _merge(branches, gates, wb, layer, tm=None, tn=None):
    m, kd = branches[0].shape
    d = wb.shape[-1]
    tm = tm or _pick(m, (1152, 1024, 768, 512, 256))
    tn = tn or _pick(d, (512, 256, 128))
    nd = d // tn
    br_spec = pl.BlockSpec((tm, kd), lambda j, i: (i, 0))
    gate_specs = [pl.BlockSpec((tm, tn), functools.partial(lambda j, i, n: (i, j + n * nd), n=n))
                  for n in range(N_BRANCH)]
    return pl.pallas_call(
        _merge_kernel,
        grid=(nd, m // tm),
        in_specs=[br_spec] * N_BRANCH + [pl.BlockSpec((None, N_BRANCH, kd, tn), lambda j, i: (layer, 0, 0, j))]
        + gate_specs,
        out_specs=pl.BlockSpec((tm, tn), lambda j, i: (i, j)),
        out_shape=jax.ShapeDtypeStruct((m, d), BF16),
        compiler_params=_params("parallel", "parallel"), name="merge",
    )(*branches, wb, gates, gates, gates, gates)


def _conv_kernel(u_ref, w_ref, b_ref, o_ref, *, t_total, n_ctx, act):
    x = u_ref[...]
    pos = lax.broadcasted_iota(jnp.int32, (t_total, 1), 0)
    seg_lo = jnp.where(pos < n_ctx, 0, n_ctx)
    seg_hi = jnp.where(pos < n_ctx, n_ctx, t_total)
    left = (CONV_W - 1) // 2
    acc = None
    for j in range(CONV_W):
        d = j - left
        xs = x if d == 0 else pltpu.roll(x, (-d) % t_total, 0)
        ok = (pos + d >= seg_lo) & (pos + d < seg_hi)
        term = jnp.where(ok, xs, 0.0) * w_ref[j:j + 1, :]
        acc = term if acc is None else acc + term
    acc = acc + b_ref[...]
    o_ref[...] = _silu(acc) if act else acc


def _dwconv(u, col0, width, w, b, *, n_ctx, act, wt=256):
    bsz, t, _ = u.shape
    c0 = col0 // wt
    return pl.pallas_call(
        functools.partial(_conv_kernel, t_total=t, n_ctx=n_ctx, act=act),
        grid=(bsz, width // wt),
        in_specs=[pl.BlockSpec((None, t, wt), lambda i, j: (i, 0, j + c0)),
                  pl.BlockSpec((CONV_W, wt), lambda i, j: (0, j)),
                  pl.BlockSpec((1, wt), lambda i, j: (0, j))],
        out_specs=pl.BlockSpec((None, t, wt), lambda i, j: (i, 0, j)),
        out_shape=jax.ShapeDtypeStruct((bsz, t, width), F32),
        compiler_params=_params("parallel", "parallel"), name="dwconv",
    )(u, w, b.reshape(1, width))


def _ret_kernel(lg_ref, q_ref, k_ref, v_ref, g_ref, cos_ref, sin_ref, o_ref, qs, ks, acc_f, acc_b,
                *, t_total, n_ctx):
    c = SCAN_CHUNK
    n_chunks, n_ctx_chunks = t_total // c, n_ctx // c
    lg = lg_ref[pl.program_id(1)]
    cos, sin = cos_ref[...], sin_ref[...]
    q, k = q_ref[...], k_ref[...]
    qs[...] = q * cos + pltpu.roll(q, RET_DK // 2, 1) * sin
    ks[...] = (k * cos + pltpu.roll(k, RET_DK // 2, 1) * sin) * (RET_DK ** -0.5)

    ii = lax.broadcasted_iota(jnp.int32, (c, c), 0)
    jj = lax.broadcasted_iota(jnp.int32, (c, c), 1)
    dist = (ii - jj).astype(F32)
    r = lax.broadcasted_iota(jnp.int32, (c, 1), 0).astype(F32)
    decay_all = jnp.exp(c * lg)

    def chunk(off, s_in, fwd):
        qc = qs[pl.ds(off, c), :]
        kc = ks[pl.ds(off, c), :]
        vc = v_ref[pl.ds(off, c), :].astype(BF16)
        if fwd:
            mask, steps, q_pow, k_pow = ii >= jj, dist, r + 1.0, c - 1.0 - r
        else:
            mask, steps, q_pow, k_pow = jj >= ii, -dist, c - r, r
        decay = jnp.where(mask, jnp.exp(jnp.where(mask, steps, 0.0) * lg), 0.0)
        scores = _dot_nt(qc.astype(BF16), kc.astype(BF16)) * decay
        o = _dot(scores.astype(BF16), vc) + jnp.exp(q_pow * lg) * _dot(qc.astype(BF16), s_in.astype(BF16))
        s_out = decay_all * s_in + _dot_tn((kc * jnp.exp(k_pow * lg)).astype(BF16), vc)
        return o, s_out

    def post(off):
        o = acc_f[pl.ds(off, c), :] + acc_b[pl.ds(off, c), :]
        y = o * lax.rsqrt(jnp.mean(o * o, axis=-1, keepdims=True) + EPS)
        o_ref[pl.ds(off, c), :] = (y * _silu(g_ref[pl.ds(off, c), :])).astype(o_ref.dtype)

    _scan_both_ways(n_chunks, n_ctx_chunks, c, lambda off, st, d: chunk(off, st, d == 0), acc_f, acc_b,
                    jnp.zeros((RET_DK, RET_DK), F32), post)


def _retention(u, col0, mix_w, cos2, sin2, *, n_ctx):
    bsz, t, _ = u.shape
    heads = mix_w // RET_DK
    c0 = col0 // RET_DK
    log_gamma = jnp.log1p(-jnp.exp2(-5.0 - jnp.arange(heads, dtype=F32)))

    def spec(k):
        return pl.BlockSpec((None, t, RET_DK), lambda i, h, lg, k=k: (i, 0, c0 + k * heads + h))

    table = pl.BlockSpec((t, RET_DK), lambda i, h, lg: (0, 0))
    return pl.pallas_call(
        functools.partial(_ret_kernel, t_total=t, n_ctx=n_ctx),
        grid_spec=pltpu.PrefetchScalarGridSpec(
            num_scalar_prefetch=1, grid=(bsz, heads),
            in_specs=[spec(0), spec(1), spec(2), spec(3), table, table],
            out_specs=pl.BlockSpec((None, t, RET_DK), lambda i, h, lg: (i, 0, h)),
            scratch_shapes=[pltpu.VMEM((t, RET_DK), F32)] * 4),
        out_shape=jax.ShapeDtypeStruct((bsz, t, mix_w), BF16),
        compiler_params=_params("parallel", "arbitrary"), name="retention",
    )(log_gamma, u, u, u, u, cos2, sin2)


def _ssd_kernel(xs_ref, bm_ref, cm_ref, z_ref, dt_ref, dtb_ref, alog_ref, dsk_ref, nw_ref, sel_ref, ord_ref, o_ref,
                acc_f, acc_b, dts, las, *, t_total, n_ctx):
    c = ord_ref.shape[1]
    rep = sel_ref.shape[1] // 2
    gw = rep * SSD_HD
    n_chunks, n_ctx_chunks = t_total // c, n_ctx // c
    dt_all = _softplus(dt_ref[...] + dtb_ref[...])
    dts[...] = dt_all
    las[...] = dt_all * (-jnp.exp(alog_ref[...]))

    ii = lax.broadcasted_iota(jnp.int32, (c, c), 0)
    jj = lax.broadcasted_iota(jnp.int32, (c, c), 1)
    lane_head = lax.broadcasted_iota(jnp.int32, (1, gw), 1) // SSD_HD

    def chunk(off, s_in, d):
        fwd = d == 0
        mask = (ii >= jj) if fwd else (jj >= ii)
        la_parts = _split_bf16(las[pl.ds(off, c), :], 3)
        cum = sum(_dot(ord_ref[d], p) for p in la_parts)
        cum_rows = sum(_dot_tn(p, ord_ref[1 - d]) for p in la_parts)
        sel = sel_ref[d]
        cum_e = sum(_dot(p, sel) for p in _split_bf16(cum, 3))
        dt_e = sum(_dot(p, sel) for p in _split_bf16(dts[pl.ds(off, c), :], 3))
        last_e = cum_e[c - 1:c, :] if fwd else cum_e[0:1, :]
        xs = xs_ref[pl.ds(off, c), :]
        vdt = xs * dt_e
        cm = cm_ref[pl.ds(off, c), :].astype(BF16)
        bm = bm_ref[pl.ds(off, c), :].astype(BF16)
        qk = _dot_nt(cm, bm)
        o = jnp.exp(cum_e) * _dot(cm, s_in.astype(BF16))
        for i in range(rep):
            col = d * rep + i
            diff = cum[:, col:col + 1] - cum_rows[col:col + 1, :]
            decay = jnp.where(mask, jnp.exp(jnp.where(mask, diff, 0.0)), 0.0)
            o = o + _dot((qk * decay).astype(BF16), jnp.where(lane_head == i, vdt, 0.0).astype(BF16))
        s_out = jnp.exp(last_e) * s_in + _dot_tn(bm, (vdt * jnp.exp(last_e - cum_e)).astype(BF16))
        return o, s_out

    def post(off):
        o = acc_f[pl.ds(off, c), :] + acc_b[pl.ds(off, c), :]
        y = (o + dsk_ref[...] * xs_ref[pl.ds(off, c), :]) * _silu(z_ref[pl.ds(off, c), :])
        y = y * lax.rsqrt(jnp.mean(y * y, axis=-1, keepdims=True) + EPS)
        o_ref[pl.ds(off, c), :] = (y * nw_ref[...]).astype(o_ref.dtype)

    _scan_both_ways(n_chunks, n_ctx_chunks, c, chunk, acc_f, acc_b, jnp.zeros((SSD_STATE, gw), F32), post)


def _ssd(u_z, z_col0, xact, u_dt, dt_bias, a_log, d_skip, norm_w, mix_w, *, n_ctx):
    bsz, t, _ = xact.shape
    heads = mix_w // SSD_HD
    rep = heads // SSD_GROUPS
    gw = rep * SSD_HD
    def per_group(v):
        lead = v.shape[:-1]
        v = v.reshape(*lead, 2, SSD_GROUPS, rep)
        return jnp.moveaxis(v, -2, 0).reshape(SSD_GROUPS, *lead, 2 * rep)
    dt_g = jnp.moveaxis(per_group(u_dt), 0, 1)
    dtb_g = per_group(dt_bias.reshape(1, 2 * heads))
    alog_g = per_group(a_log.reshape(1, 2 * heads))
    sel = np.zeros((2, 2 * rep, gw), np.float32)
    for d in range(2):
        for i in range(rep):
            sel[d, d * rep + i, i * SSD_HD:(i + 1) * SSD_HD] = 1.0
    dsk = jnp.repeat(d_skip, SSD_HD).reshape(1, mix_w)
    zc, bc, cc = z_col0 // gw, mix_w // SSD_STATE, mix_w // SSD_STATE + SSD_GROUPS
    small = lambda: pl.BlockSpec((None, 1, 2 * rep), lambda i, g: (g, 0, 0))
    return pl.pallas_call(
        functools.partial(_ssd_kernel, t_total=t, n_ctx=n_ctx),
        grid=(bsz, SSD_GROUPS),
        in_specs=[pl.BlockSpec((None, t, gw), lambda i, g: (i, 0, g)),
                  pl.BlockSpec((None, t, SSD_STATE), lambda i, g: (i, 0, bc + g)),
                  pl.BlockSpec((None, t, SSD_STATE), lambda i, g: (i, 0, cc + g)),
                  pl.BlockSpec((None, t, gw), lambda i, g: (i, 0, zc + g)),
                  pl.BlockSpec((None, None, t, 2 * rep), lambda i, g: (i, g, 0, 0)),
                  small(), small(),
                  pl.BlockSpec((1, gw), lambda i, g: (0, g)),
                  pl.BlockSpec((1, gw), lambda i, g: (0, g)),
                  pl.BlockSpec((2, 2 * rep, gw), lambda i, g: (0, 0, 0)),
                  pl.BlockSpec((2, SSD_CHUNK, SSD_CHUNK), lambda i, g: (0, 0, 0))],
        out_specs=pl.BlockSpec((None, t, gw), lambda i, g: (i, 0, g)),
        out_shape=jax.ShapeDtypeStruct((bsz, t, mix_w), BF16),
        scratch_shapes=[pltpu.VMEM((t, gw), F32), pltpu.VMEM((t, gw), F32),
                        pltpu.VMEM((t, 2 * rep), F32), pltpu.VMEM((t, 2 * rep), F32)],
        compiler_params=_params("parallel", "arbitrary"), name="ssd",
    )(xact, xact, xact, u_z, dt_g, dtb_g, alog_g, dsk, norm_w.reshape(1, mix_w), jnp.asarray(sel, BF16),
      _scan_order_tables(SSD_CHUNK))


def _tile_scan(a, b, row, reverse):
    for s in (1, 2, 4):
        if reverse:
            ok, shift = row < SUBLANES - s, SUBLANES - s
        else:
            ok, shift = row >= s, s
        b = b + a * jnp.where(ok, pltpu.roll(b, shift, 0), 0.0)
        a = a * jnp.where(ok, pltpu.roll(a, shift, 0), 1.0)
    return a, b


def _lru_kernel(xc_ref, y_ref, w_ref, bias_ref, lam_ref, o_ref, acc, a_f, b_f, a_b, b_b, *, t_total, n_ctx):
    c = SCAN_CHUNK
    nb, bw = w_ref.shape[0], w_ref.shape[1]
    n_chunks, n_ctx_chunks = t_total // c, n_ctx // c
    n_tiles = c // SUBLANES
    log_sig = -_softplus(-lam_ref[...])
    acc[...] = jnp.zeros_like(acc)
    row = lax.broadcasted_iota(jnp.int32, (SUBLANES, bw), 0)

    def gates(off, d, dst_a, dst_b):
        for blk in range(nb):
            cols = slice(blk * bw, (blk + 1) * bw)
            x = xc_ref[pl.ds(off, c), cols]
            g = _dot(x.astype(BF16), w_ref[blk, :, d * 2 * bw:(d + 1) * 2 * bw]) \
                + bias_ref[blk, :, d * 2 * bw:(d + 1) * 2 * bw]
            log_a = LRU_C * _sigmoid(g[:, :bw]) * log_sig[d:d + 1, cols]
            a = jnp.exp(log_a)
            dst_a[:, cols] = a
            dst_b[:, cols] = jnp.sqrt(-jnp.tanh(log_a) * (a * a + 1.0)) * (_sigmoid(g[:, bw:]) * x)

    def step(s, carry):
        off_f = pl.multiple_of(s * c, c)
        off_b = pl.multiple_of(_bwd_chunk(s, n_chunks, n_ctx_chunks) * c, c)
        gates(off_f, 0, a_f, b_f)
        gates(off_b, 1, a_b, b_b)

        def tile(t, carry):
            h_f, h_b = carry
            r_f = pl.multiple_of(t * SUBLANES, SUBLANES)
            r_b = pl.multiple_of((n_tiles - 1 - t) * SUBLANES, SUBLANES)
            new_f, new_b = [], []
            for blk in range(nb):
                cols = slice(blk * bw, (blk + 1) * bw)
                a, b = _tile_scan(a_f[pl.ds(r_f, SUBLANES), cols], b_f[pl.ds(r_f, SUBLANES), cols], row, False)
                h = b + a * h_f[blk]
                acc[pl.ds(off_f + r_f, SUBLANES), cols] += h
                new_f.append(h[SUBLANES - 1:SUBLANES, :])
                a, b = _tile_scan(a_b[pl.ds(r_b, SUBLANES), cols], b_b[pl.ds(r_b, SUBLANES), cols], row, True)
                h = b + a * h_b[blk]
                acc[pl.ds(off_b + r_b, SUBLANES), cols] += h
                new_b.append(h[0:1, :])
            return tuple(new_f), tuple(new_b)

        return lax.fori_loop(0, n_tiles, tile, carry)

    zero = tuple(jnp.zeros((1, bw), F32) for _ in range(nb))
    lax.fori_loop(0, n_chunks, step, (zero, zero))

    def finish(s, _):
        off = pl.multiple_of(s * c, c)
        o_ref[pl.ds(off, c), :] = (acc[pl.ds(off, c), :] * _gelu_tanh(y_ref[pl.ds(off, c), :])).astype(o_ref.dtype)
        return 0

    lax.fori_loop(0, n_chunks, finish, 0)


def _rglru(xc, u_y, y_col0, wa, ba, wx, bx, lam, mix_w, *, n_ctx, blocks_per_step=4):
    bsz, t, _ = xc.shape
    bw = mix_w // LRU_BLOCKS
    gw = blocks_per_step * bw
    w = jnp.concatenate([wa[0], wx[0], wa[1], wx[1]], axis=-1).astype(BF16)
    blk = lambda v: v.reshape(2, LRU_BLOCKS, 1, bw)
    bias = jnp.concatenate([blk(ba)[0], blk(bx)[0], blk(ba)[1], blk(bx)[1]], axis=-1)
    yc = y_col0 // gw
    return pl.pallas_call(
        functools.partial(_lru_kernel, t_total=t, n_ctx=n_ctx),
        grid=(bsz, LRU_BLOCKS // blocks_per_step),
        in_specs=[pl.BlockSpec((None, t, gw), lambda i, j: (i, 0, j)),
                  pl.BlockSpec((None, t, gw), lambda i, j: (i, 0, yc + j)),
                  pl.BlockSpec((blocks_per_step, bw, 4 * bw), lambda i, j: (j, 0, 0)),
                  pl.BlockSpec((blocks_per_step, 1, 4 * bw), lambda i, j: (j, 0, 0)),
                  pl.BlockSpec((2, gw), lambda i, j: (0, j))],
        out_specs=pl.BlockSpec((None, t, gw), lambda i, j: (i, 0, j)),
        out_shape=jax.ShapeDtypeStruct((bsz, t, mix_w), BF16),
        scratch_shapes=[pltpu.VMEM((t, gw), F32)] + [pltpu.VMEM((SCAN_CHUNK, gw), F32)] * 4,
        compiler_params=_params("parallel", "arbitrary"), name="rglru",
    )(xc, u_y, w, bias, lam)


def _hg_tables():
    c = HG_CHUNK
    idx = np.arange(c)
    dif = np.zeros((2, HG_LEVELS, c, c), np.float32)
    isq = np.zeros((2, HG_LEVELS, c, HG_DK), np.float32)
    msk = np.zeros((2, HG_LEVELS, c, c), np.float32)
    for lvl in range(HG_LEVELS):
        half = 1 << lvl
        mid = (idx // (2 * half)) * 2 * half + half
        late = idx >= mid
        same = (idx[:, None] // (2 * half)) == (idx[None, :] // (2 * half))
        for d, boundary, is_query in ((0, mid - 1, late), (1, mid, ~late)):
            sign = np.where(is_query, 1.0, -1.0)
            dif[d, lvl, idx, idx] += sign
            dif[d, lvl, idx, boundary] -= sign
            isq[d, lvl] = is_query[:, None]
        msk[0, lvl] = same & late[:, None] & ~late[None, :]
        msk[1, lvl] = same & ~late[:, None] & late[None, :]
    lower = (idx[None, :] <= idx[:, None]).astype(np.float32)
    order = np.stack([lower, lower.T])
    rep3 = lambda v: jnp.asarray(np.concatenate([v, v, v], axis=-1), BF16)
    return rep3(order), rep3(dif.reshape(2, HG_LEVELS * c, c)), jnp.asarray(isq), jnp.asarray(msk)


def _hg_kernel(q_ref, f0_ref, f1_ref, v_ref, g_ref, lb_ref, nw_ref, ord_ref, dif_ref, isq_ref, msk_ref,
               o_ref, acc_f, acc_b, *, t_total, n_ctx):
    c = HG_CHUNK
    n_chunks, n_ctx_chunks = t_total // c, n_ctx // c
    eye = lax.broadcasted_iota(jnp.int32, (c, c), 0) == lax.broadcasted_iota(jnp.int32, (c, c), 1)

    def chunk(off, st, d):
        fwd = d == 0
        lb = lb_ref[d:d + 1, :]
        f = (f0_ref if fwd else f1_ref)[pl.ds(off, c), :]
        log_f = jnp.log(lb + (1.0 - lb) * _sigmoid(f))
        k = (1.0 - lb) * _sigmoid(-f)
        q = _silu(q_ref[pl.ds(off, c), :])
        v = v_ref[pl.ds(off, c), :].astype(BF16)
        cum = _dot(ord_ref[d], jnp.concatenate(_split_bf16(log_f, 3), axis=0))
        expo = _dot(dif_ref[d], jnp.concatenate(_split_bf16(cum, 3), axis=0))
        q_minus_k = q - k
        scores = jnp.where(eye, _dot_nt(q.astype(BF16), k.astype(BF16)), 0.0)
        for lvl in range(HG_LEVELS):
            rows = slice(lvl * c, (lvl + 1) * c)
            y = ((k + isq_ref[d, lvl] * q_minus_k) * jnp.exp(expo[rows, :])).astype(BF16)
            scores = scores + _dot_nt(y, y) * msk_ref[d, lvl]
        last = cum[c - 1:c, :] if fwd else cum[0:1, :]
        o = _dot(scores.astype(BF16), v) + _dot_nt((q * jnp.exp(cum)).astype(BF16), st.astype(BF16))
        st = st * jnp.exp(last) + _dot_tn(v, (k * jnp.exp(last - cum)).astype(BF16))
        return o, st

    def post(off):
        o = acc_f[pl.ds(off, c), :] + acc_b[pl.ds(off, c), :]
        y = o * lax.rsqrt(jnp.mean(o * o, axis=-1, keepdims=True) + EPS) * nw_ref[...]
        o_ref[pl.ds(off, c), :] = (y * _silu(g_ref[pl.ds(off, c), :])).astype(o_ref.dtype)

    _scan_both_ways(n_chunks, n_ctx_chunks, c, chunk, acc_f, acc_b, jnp.zeros((HG_DK, HG_DK), F32), post)


def _hgrn2(u, col0, mix_w, lower, norm_w, *, n_ctx):
    bsz, t, _ = u.shape
    heads = mix_w // HG_DK
    c0 = col0 // HG_DK
    tables = _hg_tables()

    def spec(k):
        return pl.BlockSpec((None, t, HG_DK), lambda i, h, k=k: (i, 0, c0 + k * heads + h))

    def table(v):
        return pl.BlockSpec(v.shape, lambda i, h, nd=v.ndim: (0,) * nd)

    return pl.pallas_call(
        functools.partial(_hg_kernel, t_total=t, n_ctx=n_ctx),
        grid=(bsz, heads),
        in_specs=[spec(0), spec(1), spec(2), spec(3), spec(4),
                  pl.BlockSpec((2, HG_DK), lambda i, h: (0, h)),
                  pl.BlockSpec((1, HG_DK), lambda i, h: (0, h))] + [table(v) for v in tables],
        out_specs=pl.BlockSpec((None, t, HG_DK), lambda i, h: (i, 0, h)),
        out_shape=jax.ShapeDtypeStruct((bsz, t, mix_w), BF16),
        scratch_shapes=[pltpu.VMEM((t, HG_DK), F32)] * 2,
        compiler_params=_params("parallel", "arbitrary"), name="hgrn2",
    )(u, u, u, u, u, lower, norm_w.reshape(1, mix_w), *tables)


def _ada_kernel(c_ref, w_ref, b_ref, o_ref):
    o_ref[...] = _dot(_silu(c_ref[...]).astype(BF16), w_ref[...].astype(BF16)) + b_ref[...]


def _ada(cond, ada_w, ada_b, tn=1024):
    r, d = cond.shape
    depth, _, n = ada_w.shape
    return pl.pallas_call(
        _ada_kernel,
        grid=(depth, n // tn),
        in_specs=[pl.BlockSpec((r, d), lambda l, j: (0, 0)),
                  pl.BlockSpec((None, d, tn), lambda l, j: (l, 0, j)),
                  pl.BlockSpec((None, 1, tn), lambda l, j: (l, 0, j))],
        out_specs=pl.BlockSpec((None, r, tn), lambda l, j: (l, 0, j)),
        out_shape=jax.ShapeDtypeStruct((depth, r, n), F32),
        compiler_params=_params("parallel", "parallel"), name="ada",
    )(cond, ada_w, ada_b.reshape(depth, 1, n))


def _rope_tables(n_ctx, n_lat):
    r, col = jnp.meshgrid(jnp.arange(n_lat // GRID_W), jnp.arange(GRID_W), indexing='ij')
    n_freq = RET_DK // 4
    freqs = ROPE_BASE ** (-jnp.arange(n_freq, dtype=F32) / n_freq)
    ang = jnp.concatenate([r.reshape(-1, 1) * freqs, col.reshape(-1, 1) * freqs], axis=-1)
    cos, sin = jnp.cos(ang), jnp.sin(ang)
    cos2 = jnp.concatenate([jnp.ones((n_ctx, RET_DK), F32), jnp.concatenate([cos, cos], axis=-1)], axis=0)
    sin2 = jnp.concatenate([jnp.zeros((n_ctx, RET_DK), F32), jnp.concatenate([-sin, sin], axis=-1)], axis=0)
    return cos2, sin2


def kernel(x, c, ctx, c_ctx, ada_w, ada_b, norm1_w, norm2_w, w_in, ssd_conv_w, ssd_conv_b, ssd_a_log, ssd_dt_bias, ssd_d, ssd_norm_w, lru_conv_w, lru_conv_b, lru_wa, lru_ba, lru_wx, lru_bx, lru_lambda, hg_lb_logits, hg_norm_w, w_branch, w_out, ffn_wgu, ffn_w2, router_w, moe_wgu, moe_w2, final_norm_w):
    bsz, n_lat, d = x.shape
    n_ctx = ctx.shape[1]
    t = n_ctx + n_lat
    depth = w_in.shape[0]
    mix_w = d // 2
    ssd_heads = mix_w // SSD_HD
    xbc_w = mix_w + 2 * SSD_GROUPS * SSD_STATE
    ret_w, ssd_w, dt_w, lru_w, hg_w, gate_w = 4 * mix_w, mix_w + xbc_w, 2 * ssd_heads, 2 * mix_w, 5 * mix_w, N_BRANCH * d
    dt_col = ret_w + ssd_w
    rest_col = dt_col + dt_w
    w_in_t = jnp.swapaxes(w_in, 1, 2)
    w_branch_bf, w_out_bf, ffn_w2_bf = w_branch.astype(BF16), w_out.astype(BF16), ffn_w2.astype(BF16)

    cos2, sin2 = _rope_tables(n_ctx, n_lat)
    p_lb = jax.nn.softmax(hg_lb_logits.astype(F32), axis=1)
    lower_bounds = jnp.cumsum(p_lb, axis=1) - p_lb[:, :1]

    pad = (-(1 + bsz)) % SUBLANES
    cond = jnp.concatenate([c_ctx[None], c, jnp.zeros((pad, d), F32)], axis=0)
    mods = _ada(cond, ada_w, ada_b).reshape(depth, -1, N_MOD, d)

    h = jnp.concatenate([ctx.astype(x.dtype), x], axis=1)
    for layer in range(depth):
        m_ctx = jnp.broadcast_to(mods[layer, 0], (bsz, N_MOD, d))
        m_lat = mods[layer, 1:1 + bsz]
        mod_rows = jnp.stack([m_ctx, m_lat], axis=1)
        mod_cols = jnp.stack([m_ctx, m_lat], axis=2)

        xn = _norm_call(h, norm1_w[layer], mod_rows, shift=0, scale=1, n_ctx=n_ctx).reshape(bsz * t, d)
        u_ret = _matmul(xn, w_in_t, layer, 0, ret_w).reshape(bsz, t, ret_w)
        u_ssd = _matmul(xn, w_in_t, layer, ret_w, ssd_w).reshape(bsz, t, ssd_w)
        u_dt = _matmul(xn, w_in_t, layer, dt_col, LANES)[:, :dt_w].reshape(bsz, t, dt_w)
        u_lru = _matmul(xn, w_in_t, layer, rest_col, lru_w).reshape(bsz, t, lru_w)
        u_hg = _matmul(xn, w_in_t, layer, rest_col + lru_w, hg_w).reshape(bsz, t, hg_w)
        u_gates = _matmul(xn, w_in_t, layer, rest_col + lru_w + hg_w, gate_w, out_dtype=BF16)

        b_ret = _retention(u_ret, 0, mix_w, cos2, sin2, n_ctx=n_ctx)
        xact = _dwconv(u_ssd, mix_w, xbc_w, ssd_conv_w[layer], ssd_conv_b[layer], n_ctx=n_ctx, act=True)
        b_ssd = _ssd(u_ssd, 0, xact, u_dt, ssd_dt_bias[layer], ssd_a_log[layer], ssd_d[layer],
                     ssd_norm_w[layer], mix_w, n_ctx=n_ctx)
        xc = _dwconv(u_lru, 0, mix_w, lru_conv_w[layer], lru_conv_b[layer], n_ctx=n_ctx, act=False)
        b_lru = _rglru(xc, u_lru, mix_w, lru_wa[layer], lru_ba[layer], lru_wx[layer], lru_bx[layer],
                       lru_lambda[layer], mix_w, n_ctx=n_ctx)
        b_hg = _hgrn2(u_hg, 0, mix_w, lower_bounds[:, layer], hg_norm_w[layer], n_ctx=n_ctx)

        branches = [v.reshape(bsz * t, mix_w) for v in (b_ret, b_ssd, b_lru, b_hg)]
        mixed = _merge(branches, u_gates, w_branch_bf, layer, tm=768)
        h2 = _matmul_residual(mixed, w_out_bf, layer, h.reshape(bsz * t, d), mod_cols, 2, t_total=t, n_ctx=n_ctx)

        if layer % 2 == 0:
            xn = _norm_call(h2.reshape(bsz, t, d), norm2_w[layer], mod_rows, shift=3, scale=4, n_ctx=n_ctx)
            act = _swiglu_up(xn.reshape(bsz * t, d), ffn_wgu, layer // 2)
            h = _matmul_residual(act, ffn_w2_bf, layer // 2, h2, mod_cols, 5, t_total=t, n_ctx=n_ctx,
                                 tm=768, tn=512, tk=act.shape[1]).reshape(bsz, t, d)
        else:
            xn, gates = _norm_call(h2.reshape(bsz, t, d), norm2_w[layer], mod_rows, shift=3, scale=4, n_ctx=n_ctx,
                                   router_w=router_w[layer // 2], out_dtype=F32)
            h = _moe_block(xn, gates, h2.reshape(bsz, t, d), moe_wgu, moe_w2, layer // 2, mod_rows, 5, n_ctx=n_ctx)
    return _final_norm_call(h, final_norm_w, n_ctx)
```
